```python
import math
import jax, jax.numpy as jnp
from jax import lax
import numpy as np

D_MODEL = 1024
BATCH = 4
SEQ = 4096
DEPTH = 1
DEC_BATCH = 128
DEC_SEQ = 8
PAST_LEN = 8192
PAGE_SIZE = 128

HEAD_DIM = 64
A_HEADS = 8
B_HEADS = 8
B_KV_HEADS = 2
B_GROUP = B_HEADS // B_KV_HEADS
A_PATTERNS = ((128, 1), (512, 4), (2048, 16))
A_WINDOW_MAX = 2048
B_WINDOW = 128
MIX_WIDTH = (A_HEADS + B_HEADS) * HEAD_DIM
QKV_COLS = 3 * A_HEADS * HEAD_DIM + (B_HEADS + 2 * B_KV_HEADS) * HEAD_DIM
SPLITS = [A_HEADS * HEAD_DIM, 2 * A_HEADS * HEAD_DIM, 3 * A_HEADS * HEAD_DIM,
          3 * A_HEADS * HEAD_DIM + B_HEADS * HEAD_DIM,
          3 * A_HEADS * HEAD_DIM + (B_HEADS + B_KV_HEADS) * HEAD_DIM]
Q_BLOCK = 128
PEER_HEADS = 8
PEER_NKEYS = 128
PEER_EXPERTS = PEER_NKEYS * PEER_NKEYS
PEER_QDIM = 256
PEER_HALF = PEER_QDIM // 2
PEER_TOPK = 16
PEER_BLOCK = 256
NORM_EPS = 1e-6
NEG = -1e30
SCALE = HEAD_DIM ** -0.5

kernel_name = 'hybrid_dilated_swa_peer_step'


def rms_norm(x, g):
    xf = x.astype(jnp.float32)
    y = xf * lax.rsqrt(jnp.mean(xf * xf, axis=-1, keepdims=True) + NORM_EPS)
    return (y * g.astype(jnp.float32)).astype(x.dtype)


def alibi_slopes(n):
    return 2.0 ** (-8.0 * jnp.arange(1, n + 1, dtype=jnp.float32) / n)


def attn_inputs(x, norm_g, w_in, g_qa, g_ka, g_qb, g_kb):
    B, L, _ = x.shape
    xn = rms_norm(x, norm_g)
    h = jnp.einsum('bld,dc->blc', xn, w_in)
    qa, ka, va, qb, kb, vb = jnp.split(h, SPLITS, axis=-1)
    qa = rms_norm(qa.reshape(B, L, A_HEADS, HEAD_DIM), g_qa)
    ka = rms_norm(ka.reshape(B, L, A_HEADS, HEAD_DIM), g_ka)
    va = va.reshape(B, L, A_HEADS, HEAD_DIM)
    qb = rms_norm(qb.reshape(B, L, B_KV_HEADS, B_GROUP, HEAD_DIM), g_qb)
    kb = rms_norm(kb.reshape(B, L, B_KV_HEADS, HEAD_DIM), g_kb)
    vb = vb.reshape(B, L, B_KV_HEADS, HEAD_DIM)
    return qa, ka, va, qb, kb, vb


def dilated_block(qb, qidx, k, v, slopes):
    ms, dens, nums = [], [], []
    for (w, d) in A_PATTERNS:
        n = w // d + 1
        dist = jnp.arange(n) * d
        kidx = qidx[:, None] - dist[None, :]
        valid = kidx >= 0
        kidx = jnp.maximum(kidx, 0)
        kg = jnp.take(k, kidx, axis=1)
        vg = jnp.take(v, kidx, axis=1)
        s = jnp.einsum('bqhd,bqnhd->bqhn', qb, kg).astype(jnp.float32) * SCALE
        s = s - slopes[:, None] * dist[None, :].astype(jnp.float32)
        s = jnp.where(valid[None, :, None, :], s, NEG)
        m = s.max(axis=-1)
        p = jnp.exp(s - m[..., None])
        ms.append(m)
        dens.append(p.sum(axis=-1))
        nums.append(jnp.einsum('bqhn,bqnhd->bqhd', p, vg.astype(jnp.float32)))
    ms = jnp.stack(ms)
    wts = jnp.exp(ms - ms.max(axis=0))
    den = (jnp.stack(dens) * wts).sum(axis=0)
    num = (jnp.stack(nums) * wts[..., None]).sum(axis=0)
    return (num / den[..., None]).astype(qb.dtype)


def dilated_attention(q, k, v, q_start, slopes):
    B, Lq, H, Dh = q.shape
    if Lq > Q_BLOCK and Lq % Q_BLOCK == 0:
        nb = Lq // Q_BLOCK
        qblocks = q.reshape(B, nb, Q_BLOCK, H, Dh).transpose(1, 0, 2, 3, 4)
        starts = q_start + jnp.arange(nb) * Q_BLOCK

        def body(args):
            qblk, s0 = args
            return dilated_block(qblk, s0 + jnp.arange(Q_BLOCK), k, v, slopes)

        out = lax.map(body, (qblocks, starts))
        return out.transpose(1, 0, 2, 3, 4).reshape(B, Lq, H, Dh)
    return dilated_block(q, q_start + jnp.arange(Lq), k, v, slopes)


def sink_attend(q, k, v, dist, valid, slopes, sinks):
    s = jnp.einsum('...qkgd,...nkd->...kgqn', q, k).astype(jnp.float32) * SCALE
    s = s - slopes[:, :, None, None] * dist.astype(jnp.float32)
    s = jnp.where(valid, s, NEG)
    sink = sinks.astype(jnp.float32)[:, :, None]
    m = jnp.maximum(s.max(axis=-1), sink)
    p = jnp.exp(s - m[..., None])
    den = p.sum(axis=-1) + jnp.exp(sink - m)
    o = jnp.einsum('...kgqn,...nkd->...qkgd', p, v.astype(jnp.float32))
    return (o / jnp.moveaxis(den, -1, -3)[..., None]).astype(q.dtype)


def swa_prompt(q, k, v, slopes, sinks):
    B, L, KV, G, Dh = q.shape
    blk = B_WINDOW
    nb = L // blk
    q6 = q.reshape(B, nb, blk, KV, G, Dh)
    kb = k.reshape(B, nb, blk, KV, Dh)
    vb = v.reshape(B, nb, blk, KV, Dh)
    pad = ((0, 0), (1, 0), (0, 0), (0, 0), (0, 0))
    kcat = jnp.concatenate([jnp.pad(kb, pad)[:, :-1], kb], axis=2)
    vcat = jnp.concatenate([jnp.pad(vb, pad)[:, :-1], vb], axis=2)
    dist = jnp.arange(blk)[:, None] + blk - jnp.arange(2 * blk)[None, :]
    band = (dist >= 0) & (dist <= B_WINDOW)
    exists = (jnp.arange(nb)[:, None, None] > 0) | (jnp.arange(2 * blk)[None, None, :] >= blk)
    valid = (band[None] & exists)[:, None, None]
    o = sink_attend(q6, kcat, vcat, dist, valid, slopes, sinks)
    return o.reshape(B, L, KV, G, Dh)


def swa_buffer(q, k_all, v_all, q_start, slopes, sinks):
    Lq = q.shape[1]
    Lk = k_all.shape[1]
    dist = (q_start + jnp.arange(Lq))[:, None] - jnp.arange(Lk)[None, :]
    valid = (dist >= 0) & (dist <= B_WINDOW)
    return sink_attend(q, k_all, v_all, dist, valid, slopes, sinks)


def peer_ffn(xn, w_pq, peer_keys, u_tab, v_tab):
    B, L, D = xn.shape
    xt = xn.reshape(-1, D)
    T = xt.shape[0]
    xt = jnp.pad(xt, ((0, (-T) % PEER_BLOCK), (0, 0)))
    xb = xt.reshape(-1, PEER_BLOCK, D)

    def block(xblk):
        q = (xblk @ w_pq).reshape(PEER_BLOCK, PEER_HEADS, 2, PEER_HALF)
        s = jnp.einsum('thpc,hpnc->thpn', q, peer_keys).astype(jnp.float32)
        sv, si = lax.top_k(s, PEER_TOPK)
        cand = sv[:, :, 0, :, None] + sv[:, :, 1, None, :]
        cidx = si[:, :, 0, :, None] * PEER_NKEYS + si[:, :, 1, None, :]
        cand = cand.reshape(PEER_BLOCK, PEER_HEADS, PEER_TOPK * PEER_TOPK)
        cidx = cidx.reshape(PEER_BLOCK, PEER_HEADS, PEER_TOPK * PEER_TOPK)
        fv, fi = lax.top_k(cand, PEER_TOPK)
        eidx = jnp.take_along_axis(cidx, fi, axis=-1)
        g = jax.nn.softmax(fv, axis=-1)
        u = u_tab[eidx]
        a = jax.nn.gelu(jnp.einsum('td,thkd->thk', xblk, u).astype(jnp.float32), approximate=False)
        coef = (g * a).astype(xblk.dtype)
        return jnp.einsum('thk,thkd->td', coef, v_tab[eidx])

    out = lax.map(block, xb).reshape(-1, D)[:T]
    return out.reshape(B, L, D)


def mix_out(oa, ob, w_o):
    B, L = oa.shape[:2]
    cat = jnp.concatenate([oa.reshape(B, L, -1), ob.reshape(B, L, -1)], axis=-1)
    return jnp.einsum('blc,cd->bld', cat, w_o)


def setup_inputs(seed: int = 0) -> dict:
    key = jax.random.key(seed)
    ks = jax.random.split(key, 20)
    la = min(A_WINDOW_MAX, PAST_LEN)
    lb = min(B_WINDOW, PAST_LEN)
    f32 = jnp.float32
    nrm = lambda k, shape: jax.random.normal(k, shape, f32)
    return {
        'x_prompt': nrm(ks[0], (BATCH, SEQ, D_MODEL)),
        'x_sample': nrm(ks[1], (DEC_BATCH, DEC_SEQ, D_MODEL)),
        'cache_a_k': nrm(ks[2], (DEPTH, DEC_BATCH, la, A_HEADS, HEAD_DIM)),
        'cache_a_v': nrm(ks[3], (DEPTH, DEC_BATCH, la, A_HEADS, HEAD_DIM)),
        'cache_b_k': nrm(ks[4], (DEPTH, DEC_BATCH, lb, B_KV_HEADS, HEAD_DIM)),
        'cache_b_v': nrm(ks[5], (DEPTH, DEC_BATCH, lb, B_KV_HEADS, HEAD_DIM)),
        'norm_attn': 1.0 + 0.02 * nrm(ks[6], (DEPTH, D_MODEL)),
        'w_in': nrm(ks[7], (DEPTH, D_MODEL, QKV_COLS)) * D_MODEL ** -0.5,
        'g_qa': 1.0 + 0.02 * nrm(ks[8], (DEPTH, HEAD_DIM)),
        'g_ka': 1.0 + 0.02 * nrm(ks[9], (DEPTH, HEAD_DIM)),
        'g_qb': 1.0 + 0.02 * nrm(ks[10], (DEPTH, HEAD_DIM)),
        'g_kb': 1.0 + 0.02 * nrm(ks[11], (DEPTH, HEAD_DIM)),
        'sinks': 0.5 * nrm(ks[12], (DEPTH, B_HEADS)),
        'w_o': nrm(ks[13], (DEPTH, MIX_WIDTH, D_MODEL)) * MIX_WIDTH ** -0.5,
        'norm_ffn': 1.0 + 0.02 * nrm(ks[14], (DEPTH, D_MODEL)),
        'peer_wq': nrm(ks[15], (DEPTH, D_MODEL, PEER_HEADS * PEER_QDIM)) * D_MODEL ** -0.5,
        'peer_keys': nrm(ks[16], (DEPTH, PEER_HEADS, 2, PEER_NKEYS, PEER_HALF)) * PEER_HALF ** -0.5,
        'peer_u': nrm(ks[17], (DEPTH, PEER_EXPERTS, D_MODEL)) * D_MODEL ** -0.5,
        'peer_v': nrm(ks[18], (DEPTH, PEER_EXPERTS, D_MODEL)) * (PEER_HEADS * PEER_TOPK) ** -0.5,
    }


def reference(x_prompt, x_sample, cache_a_k, cache_a_v, cache_b_k, cache_b_v,
              norm_attn, w_in, g_qa, g_ka, g_qb, g_kb, sinks, w_o, norm_ffn,
              peer_wq, peer_keys, peer_u, peer_v):
    slopes_a = alibi_slopes(A_HEADS)
    slopes_b = alibi_slopes(B_HEADS).reshape(B_KV_HEADS, B_GROUP)
    la = cache_a_k.shape[2]
    lb = cache_b_k.shape[2]
    xp, xs = x_prompt, x_sample
    pak, pav, pbk, pbv, sak, sav, sbk, sbv = [], [], [], [], [], [], [], []
    for l in range(DEPTH):
        sink_l = sinks[l].reshape(B_KV_HEADS, B_GROUP)
        qa, ka, va, qb, kb, vb = attn_inputs(xp, norm_attn[l], w_in[l], g_qa[l], g_ka[l], g_qb[l], g_kb[l])
        oa = dilated_attention(qa, ka, va, 0, slopes_a)
        ob = swa_prompt(qb, kb, vb, slopes_b, sink_l)
        h = xp + mix_out(oa, ob, w_o[l])
        xp = h + peer_ffn(rms_norm(h, norm_ffn[l]), peer_wq[l], peer_keys[l], peer_u[l], peer_v[l])
        L = ka.shape[1]
        pak.append(ka[:, L - min(A_WINDOW_MAX, L):])
        pav.append(va[:, L - min(A_WINDOW_MAX, L):])
        pbk.append(kb[:, L - min(B_WINDOW, L):])
        pbv.append(vb[:, L - min(B_WINDOW, L):])
        qa, ka, va, qb, kb, vb = attn_inputs(xs, norm_attn[l], w_in[l], g_qa[l], g_ka[l], g_qb[l], g_kb[l])
        ka_all = jnp.concatenate([cache_a_k[l].astype(ka.dtype), ka], axis=1)
        va_all = jnp.concatenate([cache_a_v[l].astype(va.dtype), va], axis=1)
        kb_all = jnp.concatenate([cache_b_k[l].astype(kb.dtype), kb], axis=1)
        vb_all = jnp.concatenate([cache_b_v[l].astype(vb.dtype), vb], axis=1)
        oa = dilated_attention(qa, ka_all, va_all, la, slopes_a)
        ob = swa_buffer(qb, kb_all, vb_all, lb, slopes_b, sink_l)
        h = xs + mix_out(oa, ob, w_o[l])
        xs = h + peer_ffn(rms_norm(h, norm_ffn[l]), peer_wq[l], peer_keys[l], peer_u[l], peer_v[l])
        sak.append(ka_all[:, ka_all.shape[1] - la:])
        sav.append(va_all[:, va_all.shape[1] - la:])
        sbk.append(kb_all[:, kb_all.shape[1] - lb:])
        sbv.append(vb_all[:, vb_all.shape[1] - lb:])
    return (xp, xs, jnp.stack(pak), jnp.stack(pav), jnp.stack(pbk), jnp.stack(pbv),
            jnp.stack(sak), jnp.stack(sav), jnp.stack(sbk), jnp.stack(sbv))
```

```python
import functools
import math

import jax
import jax.numpy as jnp
from jax import lax
from jax.experimental import pallas as pl
from jax.experimental.pallas import tpu as pltpu

HEAD_DIM = 64
A_HEADS = 8
B_HEADS = 8
B_KV_HEADS = 2
B_GROUP = B_HEADS // B_KV_HEADS
A_PATTERNS = ((128, 1), (512, 4), (2048, 16))
BAND = 128
A_WIDTH = A_HEADS * HEAD_DIM
PEER_HEADS = 8
PEER_TOPK = 16
NORM_EPS = 1e-6
NEG = -1e30
SCALE = HEAD_DIM ** -0.5
LANES = 128
VMEM_LIMIT_BYTES = 56 * 1024 * 1024

BF16 = jnp.bfloat16
F32 = jnp.float32


def _alibi_slopes(n):
    return [2.0 ** (-8.0 * (i + 1) / n) for i in range(n)]


def _cparams(*sem):
    return pltpu.CompilerParams(dimension_semantics=sem, vmem_limit_bytes=VMEM_LIMIT_BYTES)


def _head_rms(h, seg, gain):
    sq = h * h
    hi = sq.astype(BF16)
    lo = (sq - hi.astype(F32)).astype(BF16)
    ms = (jnp.dot(hi, seg, preferred_element_type=F32)
          + jnp.dot(lo, seg, preferred_element_type=F32)) * (1.0 / HEAD_DIM)
    return h * lax.rsqrt(ms + NORM_EPS) * gain


def _proj_kernel(x_ref, g_ref, w_ref, seg_ref, gqa_ref, gka_ref, gqb_ref, gkb_ref,
                 qa_ref, ka_ref, va_ref, qb_ref, kbx_ref, vbx_ref, kb_ref, vb_ref):
    x = x_ref[...]
    ms = jnp.mean(x * x, axis=-1, keepdims=True)
    xn = (x * lax.rsqrt(ms + NORM_EPS) * g_ref[...]).astype(BF16)
    h = jnp.dot(xn, w_ref[...], preferred_element_type=F32)
    seg = seg_ref[...]
    w = A_WIDTH
    qa_ref[...] = _head_rms(h[:, 0:w], seg, gqa_ref[...])
    ka_ref[...] = _head_rms(h[:, w:2 * w], seg, gka_ref[...])
    va_ref[...] = h[:, 2 * w:3 * w]
    qb_ref[...] = _head_rms(h[:, 3 * w:4 * w], seg, gqb_ref[...])
    kb = _head_rms(h[:, 4 * w:4 * w + LANES], seg[0:LANES, 0:LANES], gkb_ref[...])
    vb = h[:, 4 * w + LANES:4 * w + 2 * LANES]
    kb_ref[...] = kb
    vb_ref[...] = vb
    for s in range(B_GROUP):
        kbx_ref[:, s * LANES:(s + 1) * LANES] = kb
        vbx_ref[:, s * LANES:(s + 1) * LANES] = vb


def _project(x2d, norm_g, w_bf16, seg, gqa, gka, gqb, gkb, tm):
    n, d = x2d.shape
    nc = w_bf16.shape[1]
    w = A_WIDTH
    row = lambda i: (i, 0)
    const = lambda i: (0, 0)
    outs = [jax.ShapeDtypeStruct((n, w), F32)] * 6 + [jax.ShapeDtypeStruct((n, LANES), F32)] * 2
    return pl.pallas_call(
        _proj_kernel,
        grid=(n // tm,),
        in_specs=[pl.BlockSpec((tm, d), row), pl.BlockSpec((1, d), const), pl.BlockSpec((d, nc), const),
                  pl.BlockSpec((w, w), const), pl.BlockSpec((1, w), const), pl.BlockSpec((1, w), const),
                  pl.BlockSpec((1, w), const), pl.BlockSpec((1, LANES), const)],
        out_specs=[pl.BlockSpec((tm, w), row)] * 6 + [pl.BlockSpec((tm, LANES), row)] * 2,
        out_shape=outs,
        compiler_params=_cparams("parallel"),
        name="qkv_proj",
    )(x2d, norm_g, w_bf16, seg, gqa, gka, gqb, gkb)


def _band_kernel(*refs, tq, dil, slopes, with_sink, with_lse):
    if with_sink:
        sink_ref, refs = refs[0], refs[1:]
    q_ref, kc_ref, kp_ref, vc_ref, vp_ref = refs[:5]
    o_ref = refs[5]
    lse_ref = refs[6] if with_lse else None
    kbuf, vbuf, bias_ref = refs[-3:]
    i = pl.program_id(2)
    nsub = tq // BAND

    @pl.when((pl.program_id(0) == 0) & (pl.program_id(1) == 0) & (i == 0))
    def _():
        r = lax.broadcasted_iota(jnp.int32, (BAND, 2 * BAND), 0)
        c = lax.broadcasted_iota(jnp.int32, (BAND, 2 * BAND), 1)
        dist = r + BAND - c
        valid = (dist >= 0) & (dist <= BAND)
        distf = (dist * dil).astype(F32)
        for h in range(A_HEADS):
            bias_ref[h] = jnp.where(valid, -slopes[h] * distf, NEG)

    kbuf[0:BAND, :] = kp_ref[0].astype(BF16)
    kbuf[BAND:BAND + tq, :] = kc_ref[0].astype(BF16)
    vbuf[0:BAND, :] = vp_ref[0].astype(BF16)
    vbuf[BAND:BAND + tq, :] = vc_ref[0].astype(BF16)

    lane = lax.broadcasted_iota(jnp.int32, (BAND, LANES), 1)
    col = lax.broadcasted_iota(jnp.int32, (1, 2 * BAND), 1)
    prev_cols = (col < BAND).astype(F32)

    def body(j, carry):
        r0 = pl.multiple_of(j * BAND, BAND)
        pen = jnp.where((i * nsub + j) == 0, NEG, 0.0) * prev_cols
        for hp in range(A_HEADS // 2):
            sl = slice(hp * LANES, (hp + 1) * LANES)
            qs = q_ref[0, pl.ds(r0, BAND), sl]
            kw = kbuf[pl.ds(r0, 2 * BAND), sl]
            vw = vbuf[pl.ds(r0, 2 * BAND), sl]
            outs, lses = [], []
            for e in range(2):
                h = 2 * hp + e
                qm = jnp.where((lane >= HEAD_DIM) == bool(e), qs, 0.0).astype(BF16)
                s = lax.dot_general(qm, kw, (((1,), (1,)), ((), ())), preferred_element_type=F32)
                s = s * SCALE + bias_ref[h] + pen
                m = jnp.max(s, axis=-1, keepdims=True)
                if with_sink:
                    m = jnp.maximum(m, sink_ref[h])
                p = jnp.exp(s - m)
                den = jnp.sum(p, axis=-1, keepdims=True)
                if with_sink:
                    den = den + jnp.exp(sink_ref[h] - m)
                o = jnp.dot(p.astype(BF16), vw, preferred_element_type=F32)
                outs.append(o / den)
                lses.append(m + jnp.log(den))
            hi = lane >= HEAD_DIM
            o_ref[0, pl.ds(r0, BAND), sl] = jnp.where(hi, outs[1], outs[0])
            if with_lse:
                lse_ref[0, pl.ds(r0, BAND), sl] = jnp.where(hi, lses[1], lses[0])
        return carry

    lax.fori_loop(0, nsub, body, 0)


def _band_attention(q, k, v, *, dil, tq, slopes, sinks=None, with_lse=True):
    b, r, cw = q.shape
    w = A_WIDTH
    c = cw // w
    assert r % tq == 0 and tq % BAND == 0
    nsub = tq // BAND
    cur = lambda bi, ci, ii: (bi, ii, ci)
    prev = lambda bi, ci, ii: (bi, jnp.maximum(ii * nsub - 1, 0), ci)
    in_specs = [pl.BlockSpec((1, tq, w), cur), pl.BlockSpec((1, tq, w), cur), pl.BlockSpec((1, BAND, w), prev),
                pl.BlockSpec((1, tq, w), cur), pl.BlockSpec((1, BAND, w), prev)]
    args = [q, k, k, v, v]
    if sinks is not None:
        in_specs = [pl.BlockSpec(memory_space=pltpu.SMEM)] + in_specs
        args = [sinks] + args
    n_out = 2 if with_lse else 1
    out = pl.pallas_call(
        functools.partial(_band_kernel, tq=tq, dil=dil, slopes=slopes, with_sink=sinks is not None,
                          with_lse=with_lse),
        grid=(b, c, r // tq),
        in_specs=in_specs,
        out_specs=[pl.BlockSpec((1, tq, w), cur)] * n_out,
        out_shape=[jax.ShapeDtypeStruct(q.shape, F32)] * n_out,
        scratch_shapes=[pltpu.VMEM((BAND + tq, w), BF16), pltpu.VMEM((BAND + tq, w), BF16),
                        pltpu.VMEM((A_HEADS, BAND, 2 * BAND), F32)],
        compiler_params=_cparams("arbitrary", "arbitrary", "arbitrary"),
        name=f"band_attn_d{dil}",
    )(*args)
    return out if with_lse else out[0]


def _prompt_mixers(qa, ka, va, qb, kbx, vbx, sinks_perm, b, l):
    w = A_WIDTH
    slopes_a = _alibi_slopes(A_HEADS)
    sb = _alibi_slopes(B_HEADS)
    slopes_b = [sb[(h % 2) * B_GROUP + h // 2] for h in range(B_HEADS)]
    branches = []
    for (_, dil) in A_PATTERNS:
        r = l // dil
        tq = min(r, 1024)
        view = lambda t: t.reshape(b, r, dil * w)
        o, lse = _band_attention(view(qa), view(ka), view(va), dil=dil, tq=tq, slopes=slopes_a)
        branches.append((o.reshape(b * l, w), lse.reshape(b * l, w)))
    view = lambda t: t.reshape(b, l, w)
    ob = _band_attention(view(qb), view(kbx), view(vbx), dil=1, tq=min(l, 1024), slopes=slopes_b,
                         sinks=sinks_perm, with_lse=False)
    return branches, ob.reshape(b * l, w)


def _qb_perm():
    idx = []
    for g in range(B_GROUP):
        for kv in range(B_KV_HEADS):
            base = (kv * B_GROUP + g) * HEAD_DIM
            idx.extend(range(base, base + HEAD_DIM))
    return jnp.asarray(idx, jnp.int32)


NEW = 8
HQ = A_HEADS * NEW


def _block_diag_rows(q8, width):
    rep = jnp.concatenate([q8] * A_HEADS, axis=0)
    r = lax.broadcasted_iota(jnp.int32, (HQ, width), 0)
    c = lax.broadcasted_iota(jnp.int32, (HQ, width), 1)
    return jnp.where(r // NEW == c // HEAD_DIM, rep, 0.0)


def _sample_a_kernel(q_ref, kn_ref, vn_ref, ck_ref, cv_ref, nk_ref, nv_ref, o_ref,
                     b1_ref, b4_ref, b16_ref, bn_ref, knb, vnb, *, la, slopes):
    spans = [w for (w, _) in A_PATTERNS]
    bias_refs = [b1_ref, b4_ref, b16_ref]

    @pl.when(pl.program_id(0) == 0)
    def _():
        for bi, (wdw, dil) in enumerate(A_PATTERNS):
            ncol = spans[bi]
            r = lax.broadcasted_iota(jnp.int32, (HQ, ncol), 0)
            c = lax.broadcasted_iota(jnp.int32, (HQ, ncol), 1)
            dist = (r % NEW) + ncol - c
            valid = (dist <= wdw) & (dist % dil == 0)
            slope = jnp.zeros((HQ, ncol), F32)
            for h in range(A_HEADS):
                slope = jnp.where(r // NEW == h, slopes[h], slope)
            bias_refs[bi][...] = jnp.where(valid, -slope * dist.astype(F32), NEG)
            rn = lax.broadcasted_iota(jnp.int32, (HQ, LANES), 0)
            cn = lax.broadcasted_iota(jnp.int32, (HQ, LANES), 1)
            dn = (rn % NEW) - cn
            vn_ok = (dn >= 0) & (dn % dil == 0) & (cn < NEW)
            sl = jnp.zeros((HQ, LANES), F32)
            for h in range(A_HEADS):
                sl = jnp.where(rn // NEW == h, slopes[h], sl)
            bn_ref[bi] = jnp.where(vn_ok, -sl * dn.astype(F32), NEG)
        knb[...] = jnp.zeros_like(knb)
        vnb[...] = jnp.zeros_like(vnb)

    kn = kn_ref[0]
    vn = vn_ref[0]
    nk_ref[0, 0:la - NEW, :] = ck_ref[0, NEW:la, :]
    nk_ref[0, la - NEW:la, :] = kn
    nv_ref[0, 0:la - NEW, :] = cv_ref[0, NEW:la, :]
    nv_ref[0, la - NEW:la, :] = vn
    knb[0:NEW, :] = kn
    vnb[0:NEW, :] = vn

    qbd = _block_diag_rows(q_ref[0], A_WIDTH).astype(BF16)
    nt = (((1,), (1,)), ((), ()))
    s_new = lax.dot_general(qbd, knb[...].astype(BF16), nt, preferred_element_type=F32) * SCALE
    ms, dens, nums = [], [], []
    for bi in range(len(A_PATTERNS)):
        ncol = spans[bi]
        kc = ck_ref[0, la - ncol:la, :].astype(BF16)
        vc = cv_ref[0, la - ncol:la, :].astype(BF16)
        sc = lax.dot_general(qbd, kc, nt, preferred_element_type=F32) * SCALE + bias_refs[bi][...]
        sn = s_new + bn_ref[bi]
        m = jnp.maximum(jnp.max(sc, axis=-1, keepdims=True), jnp.max(sn, axis=-1, keepdims=True))
        pc = jnp.exp(sc - m)
        pn = jnp.exp(sn - m)
        dens.append(jnp.sum(pc, axis=-1, keepdims=True) + jnp.sum(pn, axis=-1, keepdims=True))
        nums.append(jnp.dot(pc.astype(BF16), vc, preferred_element_type=F32)
                    + jnp.dot(pn.astype(BF16), vnb[...].astype(BF16), preferred_element_type=F32))
        ms.append(m)
    mx = jnp.maximum(jnp.maximum(ms[0], ms[1]), ms[2])
    den = 0.0
    num = 0.0
    for bi in range(len(A_PATTERNS)):
        wt = jnp.exp(ms[bi] - mx)
        den = den + dens[bi] * wt
        num = num + nums[bi] * wt
    out = _block_diag_rows_keep(num / den, A_WIDTH)
    o_ref[0] = sum(out[h * NEW:(h + 1) * NEW, :] for h in range(A_HEADS))


def _block_diag_rows_keep(x, width):
    r = lax.broadcasted_iota(jnp.int32, (HQ, width), 0)
    c = lax.broadcasted_iota(jnp.int32, (HQ, width), 1)
    return jnp.where(r // NEW == c // HEAD_DIM, x, 0.0)


def _sample_a(q, kn, vn, ck, cv):
    s, la, w = ck.shape
    assert la >= max(wd for (wd, _) in A_PATTERNS)
    seq = lambda i: (i, 0, 0)
    small = pl.BlockSpec((1, NEW, w), seq)
    big = pl.BlockSpec((1, la, w), seq)
    spans = [wd for (wd, _) in A_PATTERNS]
    return pl.pallas_call(
        functools.partial(_sample_a_kernel, la=la, slopes=_alibi_slopes(A_HEADS)),
        grid=(s,),
        in_specs=[small, small, small, big, big],
        out_specs=[big, big, small],
        out_shape=[jax.ShapeDtypeStruct(ck.shape, F32), jax.ShapeDtypeStruct(cv.shape, F32),
                   jax.ShapeDtypeStruct(q.shape, F32)],
        scratch_shapes=[pltpu.VMEM((HQ, spans[0]), F32), pltpu.VMEM((HQ, spans[1]), F32),
                        pltpu.VMEM((HQ, spans[2]), F32), pltpu.VMEM((len(A_PATTERNS), HQ, LANES), F32),
                        pltpu.VMEM((LANES, w), F32), pltpu.VMEM((LANES, w), F32)],
        compiler_params=_cparams("arbitrary"),
        name="sample_attn_a",
    )(q, kn, vn, ck, cv)


def _sample_b_kernel(sink_ref, q_ref, kn_ref, vn_ref, ck_ref, cv_ref, nk_ref, nv_ref, o_ref,
                     bc_ref, bn_ref, sk_ref, knb, vnb, *, lb, nseq, slopes):
    @pl.when(pl.program_id(0) == 0)
    def _():
        r = lax.broadcasted_iota(jnp.int32, (HQ, lb), 0)
        c = lax.broadcasted_iota(jnp.int32, (HQ, lb), 1)
        dist = (r % NEW) + lb - c
        slope = jnp.zeros((HQ, lb), F32)
        sink = jnp.zeros((HQ, LANES), F32)
        rs = lax.broadcasted_iota(jnp.int32, (HQ, LANES), 0)
        for h in range(B_HEADS):
            slope = jnp.where(r // NEW == h, slopes[h], slope)
            sink = jnp.where(rs // NEW == h, sink_ref[h], sink)
        bc_ref[...] = jnp.where(dist <= BAND, -slope * dist.astype(F32), NEG)
        cn = lax.broadcasted_iota(jnp.int32, (HQ, LANES), 1)
        dn = (rs % NEW) - cn
        sl = jnp.zeros((HQ, LANES), F32)
        for h in range(B_HEADS):
            sl = jnp.where(rs // NEW == h, slopes[h], sl)
        bn_ref[...] = jnp.where((dn >= 0) & (cn < NEW), -sl * dn.astype(F32), NEG)
        sk_ref[...] = sink
        knb[...] = jnp.zeros_like(knb)
        vnb[...] = jnp.zeros_like(vnb)

    nt = (((1,), (1,)), ((), ()))
    rr = lax.broadcasted_iota(jnp.int32, (HQ, LANES), 0)
    cc = lax.broadcasted_iota(jnp.int32, (HQ, LANES), 1)
    own = ((rr // NEW) % B_KV_HEADS) == (cc // HEAD_DIM)
    sink = sk_ref[:, 0:1]
    for t in range(nseq):
        kn = kn_ref[t]
        vn = vn_ref[t]
        nk_ref[t, 0:lb - NEW, :] = ck_ref[t, NEW:lb, :]
        nk_ref[t, lb - NEW:lb, :] = kn
        nv_ref[t, 0:lb - NEW, :] = cv_ref[t, NEW:lb, :]
        nv_ref[t, lb - NEW:lb, :] = vn
        knb[0:NEW, :] = kn
        vnb[0:NEW, :] = vn
        q = q_ref[t]
        rows = []
        for g in range(B_GROUP):
            for kv in range(B_KV_HEADS):
                rows.append(q[:, g * LANES:(g + 1) * LANES])
        qbd = jnp.where(own, jnp.concatenate(rows, axis=0), 0.0).astype(BF16)
        sc = lax.dot_general(qbd, ck_ref[t].astype(BF16), nt, preferred_element_type=F32) * SCALE + bc_ref[...]
        sn = lax.dot_general(qbd, knb[...].astype(BF16), nt, preferred_element_type=F32) * SCALE + bn_ref[...]
        m = jnp.maximum(jnp.maximum(jnp.max(sc, axis=-1, keepdims=True), jnp.max(sn, axis=-1, keepdims=True)), sink)
        pc = jnp.exp(sc - m)
        pn = jnp.exp(sn - m)
        den = jnp.sum(pc, axis=-1, keepdims=True) + jnp.sum(pn, axis=-1, keepdims=True) + jnp.exp(sink - m)
        o = (jnp.dot(pc.astype(BF16), cv_ref[t].astype(BF16), preferred_element_type=F32)
             + jnp.dot(pn.astype(BF16), vnb[...].astype(BF16), preferred_element_type=F32)) / den
        o = jnp.where(own, o, 0.0)
        for g in range(B_GROUP):
            base = g * B_KV_HEADS * NEW
            o_ref[t, :, g * LANES:(g + 1) * LANES] = o[base:base + NEW, :] + o[base + NEW:base + 2 * NEW, :]


def _sample_b(sinks_perm, slopes_perm, q, kn, vn, ck, cv, nseq=8):
    s, lb, kw = ck.shape
    assert lb == BAND and kw == LANES and s % nseq == 0
    seq = lambda i: (i, 0, 0)
    return pl.pallas_call(
        functools.partial(_sample_b_kernel, lb=lb, nseq=nseq, slopes=slopes_perm),
        grid=(s // nseq,),
        in_specs=[pl.BlockSpec(memory_space=pltpu.SMEM), pl.BlockSpec((nseq, NEW, A_WIDTH), seq),
                  pl.BlockSpec((nseq, NEW, kw), seq), pl.BlockSpec((nseq, NEW, kw), seq),
                  pl.BlockSpec((nseq, lb, kw), seq), pl.BlockSpec((nseq, lb, kw), seq)],
        out_specs=[pl.BlockSpec((nseq, lb, kw), seq), pl.BlockSpec((nseq, lb, kw), seq),
                   pl.BlockSpec((nseq, NEW, A_WIDTH), seq)],
        out_shape=[jax.ShapeDtypeStruct(ck.shape, F32), jax.ShapeDtypeStruct(cv.shape, F32),
                   jax.ShapeDtypeStruct(q.shape, F32)],
        scratch_shapes=[pltpu.VMEM((HQ, lb), F32), pltpu.VMEM((HQ, LANES), F32), pltpu.VMEM((HQ, LANES), F32),
                        pltpu.VMEM((LANES, kw), F32), pltpu.VMEM((LANES, kw), F32)],
        compiler_params=_cparams("arbitrary"),
        name="sample_attn_b",
    )(sinks_perm, q, kn, vn, ck, cv)


def _outproj_kernel(*refs, nbr):
    x_ref = refs[0]
    o_refs = refs[1:1 + nbr]
    l_refs = refs[1 + nbr:1 + 2 * nbr] if nbr > 1 else ()
    ob_ref, wa_ref, wb_ref, ht_ref = refs[-4:]
    if nbr > 1:
        ls = [r[...] for r in l_refs]
        mx = functools.reduce(jnp.maximum, ls)
        ws = [jnp.exp(l - mx) for l in ls]
        num = sum(w * r[...] for w, r in zip(ws, o_refs))
        oa = num / sum(ws)
    else:
        oa = o_refs[0][...]
    h = (x_ref[...] + jnp.dot(oa.astype(BF16), wa_ref[...], preferred_element_type=F32)
         + jnp.dot(ob_ref[...].astype(BF16), wb_ref[...], preferred_element_type=F32))
    ht_ref[...] = h.T


def _outproj(x2d, oas, lses, ob, wa, wb, tm):
    n, d = x2d.shape
    w = A_WIDTH
    nbr = len(oas)
    row = lambda i: (i, 0)
    const = lambda i: (0, 0)
    half = pl.BlockSpec((tm, w), row)
    args = [x2d] + list(oas) + (list(lses) if nbr > 1 else []) + [ob, wa, wb]
    in_specs = ([pl.BlockSpec((tm, d), row)] + [half] * (nbr + (nbr if nbr > 1 else 0)) + [half]
                + [pl.BlockSpec((w, d), const)] * 2)
    return pl.pallas_call(
        functools.partial(_outproj_kernel, nbr=nbr),
        grid=(n // tm,),
        in_specs=in_specs,
        out_specs=pl.BlockSpec((d, tm), lambda i: (0, i)),
        out_shape=jax.ShapeDtypeStruct((d, n), F32),
        compiler_params=_cparams("parallel"),
        name=f"out_proj_{nbr}",
    )(*args)


BIG = 3.0e38


def _stair_layout():
    flat = [b for b in range(16)]
    for a in range(1, 8):
        flat += [a * 16 + b for b in range(8)]
    flat += [a * 16 for a in range(8, 16)]
    return flat


def _top16_rows(s, nk):
    rows = lax.broadcasted_iota(jnp.int32, (nk, LANES), 0).astype(F32)
    r16 = lax.broadcasted_iota(jnp.int32, (PEER_TOPK, LANES), 0)
    cur = s
    rank = jnp.full((nk, LANES), float(PEER_TOPK), F32)
    vals = jnp.zeros((PEER_TOPK, LANES), F32)
    for k in range(PEER_TOPK):
        m = jnp.max(cur, axis=0, keepdims=True)
        first = jnp.min(jnp.where(cur == m, rows, float(nk)), axis=0, keepdims=True)
        hit = rows == first
        rank = jnp.where(hit, float(k), rank)
        cur = jnp.where(hit, -BIG, cur)
        vals = jnp.where(r16 == k, m, vals)
    return vals, rank


def _peer_select_kernel(ht_ref, g_ref, wq_ref, keys_ref, flat_ref,
                        hn_ref, r1_ref, w1_ref, cnt_ref, w0_ref, s_ref, *, nk, tn):
    h = ht_ref[...]
    ms = jnp.mean(h * h, axis=0, keepdims=True)
    hn = (h * lax.rsqrt(ms + NORM_EPS) * g_ref[...]).astype(BF16)
    hn_ref[...] = hn
    q = jnp.dot(wq_ref[...], hn, preferred_element_type=F32)
    half = q.shape[0] // (2 * PEER_HEADS)
    for k in range(2 * PEER_HEADS):
        qk = q[k * half:(k + 1) * half, :].astype(BF16)
        s_ref[k] = jnp.dot(keys_ref[k], qk, preferred_element_type=F32)
    nchunk = tn // LANES

    def head_chunk(t, carry):
        hd = t // nchunk
        l0 = pl.multiple_of((t % nchunk) * LANES, LANES)
        flat = flat_ref[...]
        s0 = s_ref[2 * hd, :, pl.ds(l0, LANES)]
        s1 = s_ref[2 * hd + 1, :, pl.ds(l0, LANES)]
        v0, rk0 = _top16_rows(s0, nk)
        v1, rk1 = _top16_rows(s1, nk)
        blocks = [v0[0:1, :] + v1]
        for a in range(1, 8):
            blocks.append(v0[a:a + 1, :] + v1[0:8, :])
        blocks.append(v0[8:16, :] + v1[0:1, :])
        cand = jnp.concatenate(blocks, axis=0)
        top = cand[0:1, :]
        cur = cand
        sel = jnp.zeros(cand.shape, F32)
        for _ in range(PEER_TOPK):
            m = jnp.max(cur, axis=0, keepdims=True)
            first = jnp.min(jnp.where(cur == m, flat, 4096.0), axis=0, keepdims=True)
            hit = flat == first
            sel = jnp.where(hit, 1.0, sel)
            cur = jnp.where(hit, -BIG, cur)
        z = jnp.sum(sel * jnp.exp(cand - top), axis=0, keepdims=True)
        r8 = lax.broadcasted_iota(jnp.int32, (8, LANES), 0)
        low = jnp.where(r8 == 0, jnp.sum(sel[0:16, :], axis=0, keepdims=True), 0.0)
        for a in range(1, 8):
            low = jnp.where(r8 == a, jnp.sum(sel[8 + 8 * a:16 + 8 * a, :], axis=0, keepdims=True), low)
        cnt16 = jnp.concatenate([low, sel[72:80, :]], axis=0)
        cnt = jnp.zeros((nk, LANES), F32)
        for a in range(PEER_TOPK):
            cnt = jnp.where(rk0 == float(a), cnt16[a:a + 1, :], cnt)
        cnt_ref[hd, :, pl.ds(l0, LANES)] = cnt
        w0_ref[hd, :, pl.ds(l0, LANES)] = jnp.exp(s0 - v0[0:1, :]) * (1.0 / z)
        r1_ref[hd, :, pl.ds(l0, LANES)] = rk1
        w1_ref[hd, :, pl.ds(l0, LANES)] = jnp.exp(s1 - v1[0:1, :])
        return carry

    lax.fori_loop(0, PEER_HEADS * nchunk, head_chunk, 0)


def _peer_select(ht, g_col, wq_t, keys, flat, tn):
    d, n = ht.shape
    nslab, nk, half = keys.shape
    tok = lambda i: (0, i)
    tok3 = lambda i: (0, 0, i)
    stat = jax.ShapeDtypeStruct((PEER_HEADS, nk, n), F32)
    return pl.pallas_call(
        functools.partial(_peer_select_kernel, nk=nk, tn=tn),
        grid=(n // tn,),
        in_specs=[pl.BlockSpec((d, tn), tok), pl.BlockSpec((d, 1), lambda i: (0, 0)),
                  pl.BlockSpec(wq_t.shape, lambda i: (0, 0)), pl.BlockSpec(keys.shape, lambda i: (0, 0, 0)),
                  pl.BlockSpec(flat.shape, lambda i: (0, 0))],
        out_specs=[pl.BlockSpec((d, tn), tok)] + [pl.BlockSpec((PEER_HEADS, nk, tn), tok3)] * 4,
        out_shape=[jax.ShapeDtypeStruct((d, n), BF16), stat, stat, stat, stat],
        scratch_shapes=[pltpu.VMEM((nslab, nk, tn), F32)],
        compiler_params=_cparams("parallel"),
        name="peer_select",
    )(ht, g_col, wq_t, keys, flat)


def _gelu(x):
    return 0.5 * x * (1.0 + lax.erf(x * (2.0 ** -0.5)))


def _peer_dense_kernel(hn_ref, u_ref, vt_ref, r1_ref, w1_ref, cnt_ref, w0_ref, ht_ref, y_ref,
                       acc_ref, a_ref, c_ref, *, nk, te, tn, lc):
    e = pl.program_id(1)

    @pl.when(e == 0)
    def _():
        acc_ref[...] = jnp.zeros_like(acc_ref)

    a_ref[...] = jnp.dot(u_ref[...].astype(BF16), hn_ref[...], preferred_element_type=F32)
    nslab = te // nk

    def slab(ii, carry):
        i = e * nslab + ii
        r0 = pl.multiple_of(ii * nk, nk)
        for c in range(tn // lc):
            ls = slice(c * lc, (c + 1) * lc)
            g = jnp.zeros((nk, lc), F32)
            for hd in range(PEER_HEADS):
                cnt = cnt_ref[hd, pl.ds(i, 1), ls]
                w0 = w0_ref[hd, pl.ds(i, 1), ls]
                g = g + jnp.where(r1_ref[hd, :, ls] < cnt, w1_ref[hd, :, ls], 0.0) * w0
            c_ref[pl.ds(r0, nk), ls] = (g * _gelu(a_ref[pl.ds(r0, nk), ls])).astype(BF16)
        return carry

    lax.fori_loop(0, nslab, slab, 0)
    acc_ref[...] += jnp.dot(vt_ref[...], c_ref[...], preferred_element_type=F32)

    @pl.when(e == pl.num_programs(1) - 1)
    def _():
        y_ref[...] = (ht_ref[...] + acc_ref[...]).T


def _peer_dense(hn, u, vt, r1, w1, cnt, w0, ht, *, tn, te, lc):
    d, n = hn.shape
    ne = u.shape[0]
    nk = r1.shape[1]
    tok = lambda i, e: (0, i)
    tok3 = lambda i, e: (0, 0, i)
    stat = pl.BlockSpec((PEER_HEADS, nk, tn), tok3)
    return pl.pallas_call(
        functools.partial(_peer_dense_kernel, nk=nk, te=te, tn=tn, lc=lc),
        grid=(n // tn, ne // te),
        in_specs=[pl.BlockSpec((d, tn), tok), pl.BlockSpec((te, d), lambda i, e: (e, 0)),
                  pl.BlockSpec((d, te), lambda i, e: (0, e)), stat, stat, stat, stat, pl.BlockSpec((d, tn), tok)],
        out_specs=pl.BlockSpec((tn, d), lambda i, e: (i, 0)),
        out_shape=jax.ShapeDtypeStruct((n, d), F32),
        scratch_shapes=[pltpu.VMEM((d, tn), F32), pltpu.VMEM((te, tn), F32), pltpu.VMEM((te, tn), BF16)],
        compiler_params=_cparams("parallel", "arbitrary"),
        name="peer_dense",
    )(hn, u, vt, r1, w1, cnt, w0, ht)


def _peer(ht, g_col, wq_t, keys, flat, u, vt, *, tn_sel, tn, te, lc):
    hn, r1, w1, cnt, w0 = _peer_select(ht, g_col, wq_t, keys, flat, tn_sel)
    return _peer_dense(hn, u, vt, r1, w1, cnt, w0, ht, tn=tn, te=te, lc=lc)


def kernel(x_prompt, x_sample, cache_a_k, cache_a_v, cache_b_k, cache_b_v, norm_attn, w_in, g_qa, g_ka, g_qb, g_kb, sinks, w_o, norm_ffn, peer_wq, peer_keys, peer_u, peer_v):
    b, l, d = x_prompt.shape
    s, ns, _ = x_sample.shape
    assert ns == NEW and w_in.shape[0] == 1
    la, lb = cache_a_k.shape[2], cache_b_k.shape[2]
    w = A_WIDTH
    perm = _qb_perm()

    wl = w_in[0]
    w_all = jnp.concatenate([wl[:, :3 * w], wl[:, 3 * w:4 * w][:, perm], wl[:, 4 * w:]], axis=1).astype(BF16)
    seg = (jnp.arange(w)[:, None] // HEAD_DIM == jnp.arange(w)[None, :] // HEAD_DIM).astype(BF16)
    t8 = lambda g: jnp.tile(g, A_HEADS)[None, :]
    gains = (t8(g_qa[0]), t8(g_ka[0]), t8(g_qb[0]), jnp.tile(g_kb[0], B_KV_HEADS)[None, :])
    sb = _alibi_slopes(B_HEADS)
    slopes_perm = [sb[(h % 2) * B_GROUP + h // 2] for h in range(B_HEADS)]
    sinks_perm = jnp.stack([sinks[0, (h % 2) * B_GROUP + h // 2] for h in range(B_HEADS)])
    wo_a = w_o[0, :w, :].astype(BF16)
    wo_b = w_o[0, w:, :][perm, :].astype(BF16)
    g_col = norm_ffn[0][:, None]
    wq_t = peer_wq[0].T.astype(BF16)
    nk = peer_keys.shape[3]
    keys = peer_keys[0].reshape(2 * PEER_HEADS, nk, peer_keys.shape[4]).astype(BF16)
    flat = jnp.broadcast_to(jnp.asarray(_stair_layout(), F32)[:, None], (80, LANES))
    u = peer_u[0]
    vt = peer_v[0].T.astype(BF16)

    xp = x_prompt.reshape(b * l, d)
    qa, ka, va, qb, kbx, vbx, kb, vb = _project(xp, norm_attn, w_all, seg, *gains, 512)
    branches, ob = _prompt_mixers(qa, ka, va, qb, kbx, vbx, sinks_perm, b, l)
    ht_p = _outproj(xp, [o for (o, _) in branches], [ls for (_, ls) in branches], ob, wo_a, wo_b, 512)
    y_p = _peer(ht_p, g_col, wq_t, keys, flat, u, vt, tn_sel=512, tn=512, te=1024, lc=256)
    na = min(la, l)
    nb = min(lb, l)
    pak = ka.reshape(b, l, A_HEADS, HEAD_DIM)[None, :, l - na:]
    pav = va.reshape(b, l, A_HEADS, HEAD_DIM)[None, :, l - na:]
    pbk = kb.reshape(b, l, B_KV_HEADS, HEAD_DIM)[None, :, l - nb:]
    pbv = vb.reshape(b, l, B_KV_HEADS, HEAD_DIM)[None, :, l - nb:]

    xs = x_sample.reshape(s * ns, d)
    qa, ka, va, qb, kbx, vbx, kb, vb = _project(xs, norm_attn, w_all, seg, *gains, 512)
    r3 = lambda t: t.reshape(s, ns, t.shape[-1])
    sak, sav, oa = _sample_a(r3(qa), r3(ka), r3(va), cache_a_k[0].reshape(s, la, w), cache_a_v[0].reshape(s, la, w))
    sbk, sbv, ob = _sample_b(sinks_perm, slopes_perm, r3(qb), r3(kb), r3(vb),
                             cache_b_k[0].reshape(s, lb, LANES), cache_b_v[0].reshape(s, lb, LANES))
    ht_s = _outproj(xs, [oa.reshape(s * ns, w)], [], ob.reshape(s * ns, w), wo_a, wo_b, 512)
    y_s = _peer(ht_s, g_col, wq_t, keys, flat, u, vt, tn_sel=512, tn=512, te=1024, lc=256)

    return (y_p.reshape(b, l, d), y_s.reshape(s, ns, d), pak, pav, pbk, pbv,
            sak.reshape(1, s, la, A_HEADS, HEAD_DIM), sav.reshape(1, s, la, A_HEADS, HEAD_DIM),
            sbk.reshape(1, s, lb, B_KV_HEADS, HEAD_DIM), sbv.reshape(1, s, lb, B_KV_HEADS, HEAD_DIM))
```

```python
import functools
import math

import jax
import jax.numpy as jnp
from jax import lax
from jax.experimental import pallas as pl
from jax.experimental.pallas import tpu as pltpu

HEAD_DIM = 64
A_HEADS = 8
B_HEADS = 8
B_KV_HEADS = 2
B_GROUP = B_HEADS // B_KV_HEADS
A_PATTERNS = ((128, 1), (512, 4), (2048, 16))
BAND = 128
A_WIDTH = A_HEADS * HEAD_DIM
PEER_HEADS = 8
PEER_TOPK = 16
NORM_EPS = 1e-6
NEG = -1e30
SCALE = HEAD_DIM ** -0.5
LANES = 128
VMEM_LIMIT_BYTES = 56 * 1024 * 1024

BF16 = jnp.bfloat16
F32 = jnp.float32


def _alibi_slopes(n):
    return [2.0 ** (-8.0 * (i + 1) / n) for i in range(n)]


def _cparams(*sem, interleave=False):
    del interleave
    return pltpu.CompilerParams(dimension_semantics=sem, vmem_limit_bytes=VMEM_LIMIT_BYTES)


def _head_rms(h, seg, gain):
    sq = h * h
    hi = sq.astype(BF16)
    lo = (sq - hi.astype(F32)).astype(BF16)
    ms = (jnp.dot(hi, seg, preferred_element_type=F32)
          + jnp.dot(lo, seg, preferred_element_type=F32)) * (1.0 / HEAD_DIM)
    return h * lax.rsqrt(ms + NORM_EPS) * gain


NSLAB = A_WIDTH // LANES


def _proj_kernel(x_ref, g_ref, w_ref, seg_ref, gqa_ref, gka_ref, gqb_ref, gkb_ref,
                 qa4_ref, ka4_ref, va4_ref, qb4_ref, ka_ref, va_ref, kb_ref, vb_ref):
    x = x_ref[...]
    ms = jnp.mean(x * x, axis=-1, keepdims=True)
    xn = (x * lax.rsqrt(ms + NORM_EPS) * g_ref[...]).astype(BF16)
    h = jnp.dot(xn, w_ref[...], preferred_element_type=F32)
    seg = seg_ref[...]
    w = A_WIDTH
    qa = _head_rms(h[:, 0:w], seg, gqa_ref[...])
    ka = _head_rms(h[:, w:2 * w], seg, gka_ref[...])
    va = h[:, 2 * w:3 * w]
    qb = _head_rms(h[:, 3 * w:4 * w], seg, gqb_ref[...])
    ka_ref[...] = ka
    va_ref[...] = va
    kb_ref[...] = _head_rms(h[:, 4 * w:4 * w + LANES], seg[0:LANES, 0:LANES], gkb_ref[...])
    vb_ref[...] = h[:, 4 * w + LANES:4 * w + 2 * LANES]
    for s in range(NSLAB):
        sl = slice(s * LANES, (s + 1) * LANES)
        qa4_ref[s] = qa[:, sl]
        ka4_ref[s] = ka[:, sl]
        va4_ref[s] = va[:, sl]
        qb4_ref[s] = qb[:, sl]


def _project(x2d, norm_g, w_bf16, seg, gqa, gka, gqb, gkb, tm):
    n, d = x2d.shape
    nc = w_bf16.shape[1]
    w = A_WIDTH
    row = lambda i: (i, 0)
    slab = lambda i: (0, i, 0)
    const = lambda i: (0, 0)
    outs = ([jax.ShapeDtypeStruct((NSLAB, n, LANES), F32)] * 4 + [jax.ShapeDtypeStruct((n, w), F32)] * 2
            + [jax.ShapeDtypeStruct((n, LANES), F32)] * 2)
    return pl.pallas_call(
        _proj_kernel,
        grid=(n // tm,),
        in_specs=[pl.BlockSpec((tm, d), row), pl.BlockSpec((1, d), const), pl.BlockSpec((d, nc), const),
                  pl.BlockSpec((w, w), const), pl.BlockSpec((1, w), const), pl.BlockSpec((1, w), const),
                  pl.BlockSpec((1, w), const), pl.BlockSpec((1, LANES), const)],
        out_specs=([pl.BlockSpec((NSLAB, tm, LANES), slab)] * 4 + [pl.BlockSpec((tm, w), row)] * 2
                   + [pl.BlockSpec((tm, LANES), row)] * 2),
        out_shape=outs,
        compiler_params=_cparams("parallel"),
        name="qkv_proj",
    )(x2d, norm_g, w_bf16, seg, gqa, gka, gqb, gkb)


def _band_kernel(*refs, tq, dil, slopes, with_sink, with_lse):
    if with_sink:
        sink_ref, refs = refs[0], refs[1:]
    q_ref, kc_ref, kp_ref, vc_ref, vp_ref = refs[:5]
    o_ref = refs[5]
    lse_ref = refs[6] if with_lse else None
    kbuf, vbuf, bias_ref = refs[-3:]
    i = pl.program_id(2)
    nsub = tq // BAND

    @pl.when((pl.program_id(0) == 0) & (pl.program_id(1) == 0) & (i == 0))
    def _():
        r = lax.broadcasted_iota(jnp.int32, (BAND, 2 * BAND), 0)
        c = lax.broadcasted_iota(jnp.int32, (BAND, 2 * BAND), 1)
        dist = r + BAND - c
        valid = (dist >= 0) & (dist <= BAND)
        distf = (dist * dil).astype(F32)
        for h in range(A_HEADS):
            bias_ref[h] = jnp.where(valid, -slopes[h] * distf, NEG)

    kbuf[0:BAND, :] = kp_ref[0].astype(BF16)
    kbuf[BAND:BAND + tq, :] = kc_ref[0].astype(BF16)
    vbuf[0:BAND, :] = vp_ref[0].astype(BF16)
    vbuf[BAND:BAND + tq, :] = vc_ref[0].astype(BF16)

    lane = lax.broadcasted_iota(jnp.int32, (BAND, LANES), 1)
    col = lax.broadcasted_iota(jnp.int32, (1, 2 * BAND), 1)
    prev_cols = (col < BAND).astype(F32)

    def body(j, carry):
        r0 = pl.multiple_of(j * BAND, BAND)
        pen = jnp.where((i * nsub + j) == 0, NEG, 0.0) * prev_cols
        for hp in range(A_HEADS // 2):
            sl = slice(hp * LANES, (hp + 1) * LANES)
            qs = q_ref[0, pl.ds(r0, BAND), sl]
            kw = kbuf[pl.ds(r0, 2 * BAND), sl]
            vw = vbuf[pl.ds(r0, 2 * BAND), sl]
            outs, lses = [], []
            for e in range(2):
                h = 2 * hp + e
                qm = jnp.where((lane >= HEAD_DIM) == bool(e), qs, 0.0).astype(BF16)
                s = lax.dot_general(qm, kw, (((1,), (1,)), ((), ())), preferred_element_type=F32)
                s = s * SCALE + bias_ref[h] + pen
                m = jnp.max(s, axis=-1, keepdims=True)
                if with_sink:
                    m = jnp.maximum(m, sink_ref[h])
                p = jnp.exp(s - m)
                den = jnp.sum(p, axis=-1, keepdims=True)
                if with_sink:
                    den = den + jnp.exp(sink_ref[h] - m)
                o = jnp.dot(p.astype(BF16), vw, preferred_element_type=F32)
                outs.append(o / den)
                lses.append(m + jnp.log(den))
            hi = lane >= HEAD_DIM
            o_ref[0, pl.ds(r0, BAND), sl] = jnp.where(hi, outs[1], outs[0])
            if with_lse:
                lse_ref[0, pl.ds(r0, BAND), sl] = jnp.where(hi, lses[1], lses[0])
        return carry

    lax.fori_loop(0, nsub, body, 0)


def _band_attention(q, k, v, *, dil, tq, slopes, sinks=None, with_lse=True):
    b, r, cw = q.shape
    w = A_WIDTH
    c = cw // w
    assert r % tq == 0 and tq % BAND == 0
    nsub = tq // BAND
    cur = lambda bi, ci, ii: (bi, ii, ci)
    prev = lambda bi, ci, ii: (bi, jnp.maximum(ii * nsub - 1, 0), ci)
    in_specs = [pl.BlockSpec((1, tq, w), cur), pl.BlockSpec((1, tq, w), cur), pl.BlockSpec((1, BAND, w), prev),
                pl.BlockSpec((1, tq, w), cur), pl.BlockSpec((1, BAND, w), prev)]
    args = [q, k, k, v, v]
    if sinks is not None:
        in_specs = [pl.BlockSpec(memory_space=pltpu.SMEM)] + in_specs
        args = [sinks] + args
    n_out = 2 if with_lse else 1
    out = pl.pallas_call(
        functools.partial(_band_kernel, tq=tq, dil=dil, slopes=slopes, with_sink=sinks is not None,
                          with_lse=with_lse),
        grid=(b, c, r // tq),
        in_specs=in_specs,
        out_specs=[pl.BlockSpec((1, tq, w), cur)] * n_out,
        out_shape=[jax.ShapeDtypeStruct(q.shape, F32)] * n_out,
        scratch_shapes=[pltpu.VMEM((BAND + tq, w), BF16), pltpu.VMEM((BAND + tq, w), BF16),
                        pltpu.VMEM((A_HEADS, BAND, 2 * BAND), F32)],
        compiler_params=_cparams("arbitrary", "arbitrary", "arbitrary"),
        name=f"band_attn_d{dil}",
    )(*args)
    return out if with_lse else out[0]


def _prompt_mixers(qa, ka, va, qb, kbx, vbx, sinks_perm, b, l):
    w = A_WIDTH
    slopes_a = _alibi_slopes(A_HEADS)
    sb = _alibi_slopes(B_HEADS)
    slopes_b = [sb[(h % 2) * B_GROUP + h // 2] for h in range(B_HEADS)]
    branches = []
    for (_, dil) in A_PATTERNS:
        r = l // dil
        tq = min(r, 1024)
        view = lambda t: t.reshape(b, r, dil * w)
        o, lse = _band_attention(view(qa), view(ka), view(va), dil=dil, tq=tq, slopes=slopes_a)
        branches.append((o.reshape(b * l, w), lse.reshape(b * l, w)))
    view = lambda t: t.reshape(b, l, w)
    ob = _band_attention(view(qb), view(kbx), view(vbx), dil=1, tq=min(l, 1024), slopes=slopes_b,
                         sinks=sinks_perm, with_lse=False)
    return branches, ob.reshape(b * l, w)


MIXER_UNROLL = 4


def _pick(idx, values):
    out = jnp.float32(values[-1])
    for i in range(len(values) - 2, -1, -1):
        out = jnp.where(idx == i, jnp.float32(values[i]), out)
    return out


def _mixer_kernel(*refs, dils, slopes, with_sink, seq):
    if with_sink:
        sink_ref, refs = refs[0], refs[1:]
    q_ref, k_ref, v_ref, o_ref, kpad, vpad, bias_ref = refs[:7]
    ob_ref, lse_ref = refs[7:9] if len(dils) > 1 else (None, None)
    hp = pl.program_id(1)
    pad = BAND * max(dils)

    @pl.when((pl.program_id(0) == 0) & (hp == 0))
    def _():
        kpad[0:pad, :] = jnp.zeros((pad, LANES), F32)
        vpad[0:pad, :] = jnp.zeros((pad, LANES), F32)

    kpad[pad:pad + seq, :] = k_ref[0, 0]
    vpad[pad:pad + seq, :] = v_ref[0, 0]

    r = lax.broadcasted_iota(jnp.int32, (BAND, 2 * BAND), 0)
    c = lax.broadcasted_iota(jnp.int32, (BAND, 2 * BAND), 1)
    dist = r + BAND - c
    valid = (dist >= 0) & (dist <= BAND)
    for bi, dil in enumerate(dils):
        distf = (dist * dil).astype(F32)
        for e in range(2):
            slope = _pick(hp, [slopes[2 * g + e] for g in range(NSLAB)])
            bias_ref[bi, e] = jnp.where(valid, -slope * distf, NEG)

    lane = lax.broadcasted_iota(jnp.int32, (BAND, LANES), 1)
    hi = lane >= HEAD_DIM
    col = lax.broadcasted_iota(jnp.int32, (1, 2 * BAND), 1)
    prev_cols = (col < BAND).astype(F32)
    nt = (((1,), (1,)), ((), ()))

    for bi, dil in enumerate(dils):
        nblk = seq // (dil * BAND)

        def block(t, bi=bi, dil=dil, nblk=nblk):
            res = t // nblk
            j = t % nblk
            base = res + dil * BAND * j
            if dil == 1:
                base = pl.multiple_of(base, BAND)
                rows = pl.ds(base, BAND)
                win = pl.ds(pl.multiple_of(pad + base - BAND, BAND), 2 * BAND)
            else:
                rows = pl.ds(base, BAND, stride=dil)
                win = pl.ds(pad + base - dil * BAND, 2 * BAND, stride=dil)
            qs = q_ref[0, 0, rows, :]
            kw = kpad[win, :].astype(BF16)
            vw = vpad[win, :].astype(BF16)
            pen = jnp.where(j == 0, NEG, 0.0) * prev_cols
            outs, lses = [], []
            for e in range(2):
                qm = jnp.where(hi == bool(e), qs, 0.0).astype(BF16)
                s = lax.dot_general(qm, kw, nt, preferred_element_type=F32)
                s = s * SCALE + bias_ref[bi, e] + pen
                m = jnp.max(s, axis=-1, keepdims=True)
                if with_sink:
                    sink = sink_ref[2 * hp + e]
                    m = jnp.maximum(m, sink)
                p = jnp.exp(s - m)
                den = jnp.sum(p, axis=-1, keepdims=True)
                if with_sink:
                    den = den + jnp.exp(sink - m)
                o = jnp.dot(p.astype(BF16), vw, preferred_element_type=F32)
                outs.append(o / den)
                lses.append(m + jnp.log(den))
            if len(dils) == 1:
                o_ref[0, 0, rows, :] = jnp.where(hi, outs[1], outs[0])
            else:
                ob_ref[bi, rows, :] = jnp.where(hi, outs[1], outs[0])
                lse_ref[bi, rows, :] = jnp.where(hi, lses[1], lses[0])

        def blocks(t, carry, block=block):
            for uu in range(MIXER_UNROLL):
                block(t * MIXER_UNROLL + uu)
            return carry

        assert (dil * nblk) % MIXER_UNROLL == 0
        lax.fori_loop(0, dil * nblk // MIXER_UNROLL, blocks, 0)

    if len(dils) > 1:
        def merge(t, carry):
            rows = pl.ds(pl.multiple_of(t * BAND, BAND), BAND)
            ls = [lse_ref[bi, rows, :] for bi in range(len(dils))]
            mx = functools.reduce(jnp.maximum, ls)
            ws = [jnp.exp(l - mx) for l in ls]
            num = sum(w * ob_ref[bi, rows, :] for bi, w in enumerate(ws))
            o_ref[0, 0, rows, :] = num / sum(ws)
            return carry

        lax.fori_loop(0, seq // BAND, merge, 0)


def _mixer(q4, k4, v4, *, dils, slopes, sinks=None):
    nslab, b, l, _ = q4.shape
    assert l % (BAND * max(dils)) == 0
    shared = k4.shape[0] == 1
    qmap = lambda bi, hp: (hp, bi, 0, 0)
    kmap = (lambda bi, hp: (0, bi, 0, 0)) if shared else qmap
    blk = (1, 1, l, LANES)
    in_specs = [pl.BlockSpec(blk, qmap), pl.BlockSpec(blk, kmap), pl.BlockSpec(blk, kmap)]
    args = [q4, k4, v4]
    if sinks is not None:
        in_specs = [pl.BlockSpec(memory_space=pltpu.SMEM)] + in_specs
        args = [sinks] + args
    pad = BAND * max(dils)
    scratch = [pltpu.VMEM((pad + l, LANES), F32), pltpu.VMEM((pad + l, LANES), F32),
               pltpu.VMEM((len(dils), 2, BAND, 2 * BAND), F32)]
    if len(dils) > 1:
        scratch += [pltpu.VMEM((len(dils), l, LANES), F32), pltpu.VMEM((len(dils), l, LANES), F32)]
    return pl.pallas_call(
        functools.partial(_mixer_kernel, dils=tuple(dils), slopes=slopes, with_sink=sinks is not None, seq=l),
        grid=(b, nslab),
        in_specs=in_specs,
        out_specs=pl.BlockSpec(blk, qmap),
        out_shape=jax.ShapeDtypeStruct(q4.shape, F32),
        scratch_shapes=scratch,
        compiler_params=_cparams("arbitrary", "arbitrary"),
        name="mixer_b" if shared else "mixer_a",
    )(*args)


def _qb_perm():
    idx = []
    for g in range(B_GROUP):
        for kv in range(B_KV_HEADS):
            base = (kv * B_GROUP + g) * HEAD_DIM
            idx.extend(range(base, base + HEAD_DIM))
    return jnp.asarray(idx, jnp.int32)


NEW = 8
HQ = A_HEADS * NEW


def _block_diag_rows(q8, width):
    rep = jnp.concatenate([q8] * A_HEADS, axis=0)
    r = lax.broadcasted_iota(jnp.int32, (HQ, width), 0)
    c = lax.broadcasted_iota(jnp.int32, (HQ, width), 1)
    return jnp.where(r // NEW == c // HEAD_DIM, rep, 0.0)


def _sample_a_kernel(q_ref, kn_ref, vn_ref, ck_ref, cv_ref, nk_ref, nv_ref, o_ref,
                     b1_ref, b4_ref, b16_ref, bn_ref, knb, vnb, *, la, slopes):
    spans = [w for (w, _) in A_PATTERNS]
    bias_refs = [b1_ref, b4_ref, b16_ref]

    @pl.when(pl.program_id(0) == 0)
    def _():
        for bi, (wdw, dil) in enumerate(A_PATTERNS):
            ncol = spans[bi]
            r = lax.broadcasted_iota(jnp.int32, (HQ, ncol), 0)
            c = lax.broadcasted_iota(jnp.int32, (HQ, ncol), 1)
            dist = (r % NEW) + ncol - c
            valid = (dist <= wdw) & (dist % dil == 0)
            slope = jnp.zeros((HQ, ncol), F32)
            for h in range(A_HEADS):
                slope = jnp.where(r // NEW == h, slopes[h], slope)
            bias_refs[bi][...] = jnp.where(valid, -slope * dist.astype(F32), NEG)
            rn = lax.broadcasted_iota(jnp.int32, (HQ, LANES), 0)
            cn = lax.broadcasted_iota(jnp.int32, (HQ, LANES), 1)
            dn = (rn % NEW) - cn
            vn_ok = (dn >= 0) & (dn % dil == 0) & (cn < NEW)
            sl = jnp.zeros((HQ, LANES), F32)
            for h in range(A_HEADS):
                sl = jnp.where(rn // NEW == h, slopes[h], sl)
            bn_ref[bi] = jnp.where(vn_ok, -sl * dn.astype(F32), NEG)
        knb[...] = jnp.zeros_like(knb)
        vnb[...] = jnp.zeros_like(vnb)

    kn = kn_ref[0]
    vn = vn_ref[0]
    nk_ref[0, 0:la - NEW, :] = ck_ref[0, NEW:la, :]
    nk_ref[0, la - NEW:la, :] = kn
    nv_ref[0, 0:la - NEW, :] = cv_ref[0, NEW:la, :]
    nv_ref[0, la - NEW:la, :] = vn
    knb[0:NEW, :] = kn
    vnb[0:NEW, :] = vn

    qbd = _block_diag_rows(q_ref[0], A_WIDTH).astype(BF16)
    nt = (((1,), (1,)), ((), ()))
    s_new = lax.dot_general(qbd, knb[...].astype(BF16), nt, preferred_element_type=F32) * SCALE
    ms, dens, nums = [], [], []
    for bi in range(len(A_PATTERNS)):
        ncol = spans[bi]
        kc = ck_ref[0, la - ncol:la, :].astype(BF16)
        vc = cv_ref[0, la - ncol:la, :].astype(BF16)
        sc = lax.dot_general(qbd, kc, nt, preferred_element_type=F32) * SCALE + bias_refs[bi][...]
        sn = s_new + bn_ref[bi]
        m = jnp.maximum(jnp.max(sc, axis=-1, keepdims=True), jnp.max(sn, axis=-1, keepdims=True))
        pc = jnp.exp(sc - m)
        pn = jnp.exp(sn - m)
        dens.append(jnp.sum(pc, axis=-1, keepdims=True) + jnp.sum(pn, axis=-1, keepdims=True))
        nums.append(jnp.dot(pc.astype(BF16), vc, preferred_element_type=F32)
                    + jnp.dot(pn.astype(BF16), vnb[...].astype(BF16), preferred_element_type=F32))
        ms.append(m)
    mx = jnp.maximum(jnp.maximum(ms[0], ms[1]), ms[2])
    den = 0.0
    num = 0.0
    for bi in range(len(A_PATTERNS)):
        wt = jnp.exp(ms[bi] - mx)
        den = den + dens[bi] * wt
        num = num + nums[bi] * wt
    out = _block_diag_rows_keep(num / den, A_WIDTH)
    o_ref[0] = sum(out[h * NEW:(h + 1) * NEW, :] for h in range(A_HEADS))


def _block_diag_rows_keep(x, width):
    r = lax.broadcasted_iota(jnp.int32, (HQ, width), 0)
    c = lax.broadcasted_iota(jnp.int32, (HQ, width), 1)
    return jnp.where(r // NEW == c // HEAD_DIM, x, 0.0)


def _sample_a(q, kn, vn, ck, cv):
    s, la, w = ck.shape
    assert la >= max(wd for (wd, _) in A_PATTERNS)
    seq = lambda i: (i, 0, 0)
    small = pl.BlockSpec((1, NEW, w), seq)
    big = pl.BlockSpec((1, la, w), seq)
    spans = [wd for (wd, _) in A_PATTERNS]
    return pl.pallas_call(
        functools.partial(_sample_a_kernel, la=la, slopes=_alibi_slopes(A_HEADS)),
        grid=(s,),
        in_specs=[small, small, small, big, big],
        out_specs=[big, big, small],
        out_shape=[jax.ShapeDtypeStruct(ck.shape, F32), jax.ShapeDtypeStruct(cv.shape, F32),
                   jax.ShapeDtypeStruct(q.shape, F32)],
        scratch_shapes=[pltpu.VMEM((HQ, spans[0]), F32), pltpu.VMEM((HQ, spans[1]), F32),
                        pltpu.VMEM((HQ, spans[2]), F32), pltpu.VMEM((len(A_PATTERNS), HQ, LANES), F32),
                        pltpu.VMEM((LANES, w), F32), pltpu.VMEM((LANES, w), F32)],
        compiler_params=_cparams("arbitrary"),
        name="sample_attn_a",
    )(q, kn, vn, ck, cv)


SAMPLE_CHUNK = 512


def _sample_a5_kernel(q_ref, kn_ref, vn_ref, ck_ref, cv_ref, nk_hbm, nv_hbm, o_ref,
                      b16_ref, b4_ref, b1_ref, bn_ref, sem, *, la, slopes):
    b = pl.program_id(0)
    copies = [
        pltpu.make_async_copy(ck_ref.at[0, 0, pl.ds(NEW, la - NEW)], nk_hbm.at[0, b, pl.ds(0, la - NEW)], sem.at[0]),
        pltpu.make_async_copy(cv_ref.at[0, 0, pl.ds(NEW, la - NEW)], nv_hbm.at[0, b, pl.ds(0, la - NEW)], sem.at[1]),
        pltpu.make_async_copy(kn_ref.at[0], nk_hbm.at[0, b, pl.ds(la - NEW, NEW)], sem.at[2]),
        pltpu.make_async_copy(vn_ref.at[0], nv_hbm.at[0, b, pl.ds(la - NEW, NEW)], sem.at[3]),
    ]
    for cp in copies:
        cp.start()

    spans = [wd for (wd, _) in A_PATTERNS]
    bias_refs = [b1_ref, b4_ref, b16_ref]

    @pl.when(b == 0)
    def _():
        for bi, (wdw, dil) in enumerate(A_PATTERNS):
            ncol = spans[bi] * A_HEADS
            r = lax.broadcasted_iota(jnp.int32, (HQ, ncol), 0)
            c = lax.broadcasted_iota(jnp.int32, (HQ, ncol), 1)
            dist = (r % NEW) + spans[bi] - c // A_HEADS
            valid = (dist <= wdw) & (dist % dil == 0) & (c % A_HEADS == r // NEW)
            slope = jnp.zeros((HQ, ncol), F32)
            for h in range(A_HEADS):
                slope = jnp.where(r // NEW == h, slopes[h], slope)
            bias_refs[bi][...] = jnp.where(valid, -slope * dist.astype(F32), NEG)
            rn = lax.broadcasted_iota(jnp.int32, (HQ, HQ), 0)
            cn = lax.broadcasted_iota(jnp.int32, (HQ, HQ), 1)
            dn = (rn % NEW) - cn // A_HEADS
            ok = (dn >= 0) & (dn % dil == 0) & (cn % A_HEADS == rn // NEW)
            sl = jnp.zeros((HQ, HQ), F32)
            for h in range(A_HEADS):
                sl = jnp.where(rn // NEW == h, slopes[h], sl)
            bn_ref[bi] = jnp.where(ok, -sl * dn.astype(F32), NEG)

    nt = (((1,), (1,)), ((), ()))
    q = q_ref[0].astype(BF16)

    def scores(k3):
        k2 = k3.reshape(k3.shape[0] * A_HEADS, HEAD_DIM).astype(BF16)
        return lax.dot_general(q, k2, nt, preferred_element_type=F32) * SCALE

    def flat_v(v3):
        return v3.reshape(v3.shape[0] * A_HEADS, HEAD_DIM).astype(BF16)

    def fold(state, s, v2):
        m_new = jnp.max(s, axis=-1, keepdims=True)
        if state is not None:
            m_old, den, num = state
            m_new = jnp.maximum(m_new, m_old)
        p = jnp.exp(s - m_new)
        d_new = jnp.sum(p, axis=-1, keepdims=True)
        n_new = jnp.dot(p.astype(BF16), v2, preferred_element_type=F32)
        if state is not None:
            alpha = jnp.exp(m_old - m_new)
            d_new = d_new + den * alpha
            n_new = n_new + num * alpha
        return m_new, d_new, n_new

    states = [None, None, None]
    nchunk = la // SAMPLE_CHUNK
    ch_cols = SAMPLE_CHUNK * A_HEADS
    for ci in range(nchunk):
        rows = slice(ci * SAMPLE_CHUNK, (ci + 1) * SAMPLE_CHUNK)
        s = scores(ck_ref[0, 0, rows])
        v2 = flat_v(cv_ref[0, 0, rows])
        states[2] = fold(states[2], s + b16_ref[:, ci * ch_cols:(ci + 1) * ch_cols], v2)
        if ci == nchunk - 1:
            for bi in (0, 1):
                ncol = spans[bi] * A_HEADS
                states[bi] = fold(states[bi], s[:, ch_cols - ncol:] + bias_refs[bi][...], v2[ch_cols - ncol:, :])
    s_new = scores(kn_ref[0])
    v_new = flat_v(vn_ref[0])
    for bi in range(len(A_PATTERNS)):
        states[bi] = fold(states[bi], s_new + bn_ref[bi], v_new)
    mx = functools.reduce(jnp.maximum, [st[0] for st in states])
    den = 0.0
    num = 0.0
    for (m, dd, nn) in states:
        wt = jnp.exp(m - mx)
        den = den + dd * wt
        num = num + nn * wt
    o_ref[0] = num / den

    for cp in copies:
        cp.wait()


def _sample_a5(q64, kn, vn, ck5, cv5):
    _, s, la, nh, hd = ck5.shape
    spans = [wd for (wd, _) in A_PATTERNS]
    assert la == max(spans) and la % SAMPLE_CHUNK == 0 and spans[1] == SAMPLE_CHUNK and nh == A_HEADS
    seq3 = lambda i: (i, 0, 0)
    seq4 = lambda i: (i, 0, 0, 0)
    big = pl.BlockSpec((1, 1, la, nh, hd), lambda i: (0, i, 0, 0, 0))
    anyspec = pl.BlockSpec(memory_space=pl.ANY)
    return pl.pallas_call(
        functools.partial(_sample_a5_kernel, la=la, slopes=_alibi_slopes(A_HEADS)),
        grid=(s,),
        in_specs=[pl.BlockSpec((1, HQ, hd), seq3), pl.BlockSpec((1, NEW, nh, hd), seq4),
                  pl.BlockSpec((1, NEW, nh, hd), seq4), big, big],
        out_specs=[anyspec, anyspec, pl.BlockSpec((1, HQ, hd), seq3)],
        out_shape=[jax.ShapeDtypeStruct(ck5.shape, F32), jax.ShapeDtypeStruct(cv5.shape, F32),
                   jax.ShapeDtypeStruct(q64.shape, F32)],
        scratch_shapes=[pltpu.VMEM((HQ, spans[2] * nh), F32), pltpu.VMEM((HQ, spans[1] * nh), F32),
                        pltpu.VMEM((HQ, spans[0] * nh), F32), pltpu.VMEM((len(A_PATTERNS), HQ, HQ), F32),
                        pltpu.SemaphoreType.DMA((4,))],
        compiler_params=_cparams("arbitrary"),
        name="sample_attn_a5",
    )(q64, kn, vn, ck5, cv5)


def _sample_b_kernel(sink_ref, q_ref, kn_ref, vn_ref, ck_ref, cv_ref, nk_ref, nv_ref, o_ref,
                     bc_ref, bn_ref, sk_ref, knb, vnb, *, lb, nseq, slopes):
    @pl.when(pl.program_id(0) == 0)
    def _():
        r = lax.broadcasted_iota(jnp.int32, (HQ, lb), 0)
        c = lax.broadcasted_iota(jnp.int32, (HQ, lb), 1)
        dist = (r % NEW) + lb - c
        slope = jnp.zeros((HQ, lb), F32)
        sink = jnp.zeros((HQ, LANES), F32)
        rs = lax.broadcasted_iota(jnp.int32, (HQ, LANES), 0)
        for h in range(B_HEADS):
            slope = jnp.where(r // NEW == h, slopes[h], slope)
            sink = jnp.where(rs // NEW == h, sink_ref[h], sink)
        bc_ref[...] = jnp.where(dist <= BAND, -slope * dist.astype(F32), NEG)
        cn = lax.broadcasted_iota(jnp.int32, (HQ, LANES), 1)
        dn = (rs % NEW) - cn
        sl = jnp.zeros((HQ, LANES), F32)
        for h in range(B_HEADS):
            sl = jnp.where(rs // NEW == h, slopes[h], sl)
        bn_ref[...] = jnp.where((dn >= 0) & (cn < NEW), -sl * dn.astype(F32), NEG)
        sk_ref[...] = sink
        knb[...] = jnp.zeros_like(knb)
        vnb[...] = jnp.zeros_like(vnb)

    nt = (((1,), (1,)), ((), ()))
    rr = lax.broadcasted_iota(jnp.int32, (HQ, LANES), 0)
    cc = lax.broadcasted_iota(jnp.int32, (HQ, LANES), 1)
    own = ((rr // NEW) % B_KV_HEADS) == (cc // HEAD_DIM)
    sink = sk_ref[:, 0:1]
    for t in range(nseq):
        kn = kn_ref[t]
        vn = vn_ref[t]
        nk_ref[t, 0:lb - NEW, :] = ck_ref[t, NEW:lb, :]
        nk_ref[t, lb - NEW:lb, :] = kn
        nv_ref[t, 0:lb - NEW, :] = cv_ref[t, NEW:lb, :]
        nv_ref[t, lb - NEW:lb, :] = vn
        knb[0:NEW, :] = kn
        vnb[0:NEW, :] = vn
        q = q_ref[t]
        rows = []
        for g in range(B_GROUP):
            for kv in range(B_KV_HEADS):
                rows.append(q[:, g * LANES:(g + 1) * LANES])
        qbd = jnp.where(own, jnp.concatenate(rows, axis=0), 0.0).astype(BF16)
        sc = lax.dot_general(qbd, ck_ref[t].astype(BF16), nt, preferred_element_type=F32) * SCALE + bc_ref[...]
        sn = lax.dot_general(qbd, knb[...].astype(BF16), nt, preferred_element_type=F32) * SCALE + bn_ref[...]
        m = jnp.maximum(jnp.maximum(jnp.max(sc, axis=-1, keepdims=True), jnp.max(sn, axis=-1, keepdims=True)), sink)
        pc = jnp.exp(sc - m)
        pn = jnp.exp(sn - m)
        den = jnp.sum(pc, axis=-1, keepdims=True) + jnp.sum(pn, axis=-1, keepdims=True) + jnp.exp(sink - m)
        o = (jnp.dot(pc.astype(BF16), cv_ref[t].astype(BF16), preferred_element_type=F32)
             + jnp.dot(pn.astype(BF16), vnb[...].astype(BF16), preferred_element_type=F32)) / den
        o = jnp.where(own, o, 0.0)
        for g in range(B_GROUP):
            base = g * B_KV_HEADS * NEW
            o_ref[t, :, g * LANES:(g + 1) * LANES] = o[base:base + NEW, :] + o[base + NEW:base + 2 * NEW, :]


def _sample_b(sinks_perm, slopes_perm, q, kn, vn, ck, cv, nseq=8):
    s, lb, kw = ck.shape
    assert lb == BAND and kw == LANES and s % nseq == 0
    seq = lambda i: (i, 0, 0)
    return pl.pallas_call(
        functools.partial(_sample_b_kernel, lb=lb, nseq=nseq, slopes=slopes_perm),
        grid=(s // nseq,),
        in_specs=[pl.BlockSpec(memory_space=pltpu.SMEM), pl.BlockSpec((nseq, NEW, A_WIDTH), seq),
                  pl.BlockSpec((nseq, NEW, kw), seq), pl.BlockSpec((nseq, NEW, kw), seq),
                  pl.BlockSpec((nseq, lb, kw), seq), pl.BlockSpec((nseq, lb, kw), seq)],
        out_specs=[pl.BlockSpec((nseq, lb, kw), seq), pl.BlockSpec((nseq, lb, kw), seq),
                   pl.BlockSpec((nseq, NEW, A_WIDTH), seq)],
        out_shape=[jax.ShapeDtypeStruct(ck.shape, F32), jax.ShapeDtypeStruct(cv.shape, F32),
                   jax.ShapeDtypeStruct(q.shape, F32)],
        scratch_shapes=[pltpu.VMEM((HQ, lb), F32), pltpu.VMEM((HQ, LANES), F32), pltpu.VMEM((HQ, LANES), F32),
                        pltpu.VMEM((LANES, kw), F32), pltpu.VMEM((LANES, kw), F32)],
        compiler_params=_cparams("arbitrary"),
        name="sample_attn_b",
    )(sinks_perm, q, kn, vn, ck, cv)


def _outproj_kernel(x_ref, oa_ref, ob_ref, wa_ref, wb_ref, ht_ref):
    oa = jnp.concatenate([oa_ref[s] for s in range(NSLAB)], axis=1).astype(BF16)
    ob = jnp.concatenate([ob_ref[s] for s in range(NSLAB)], axis=1).astype(BF16)
    h = (x_ref[...] + jnp.dot(oa, wa_ref[...], preferred_element_type=F32)
         + jnp.dot(ob, wb_ref[...], preferred_element_type=F32))
    ht_ref[...] = h.T


def _outproj(x2d, oa4, ob4, wa, wb, tm):
    n, d = x2d.shape
    w = A_WIDTH
    row = lambda i: (i, 0)
    const = lambda i: (0, 0)
    slab = pl.BlockSpec((NSLAB, tm, LANES), lambda i: (0, i, 0))
    return pl.pallas_call(
        _outproj_kernel,
        grid=(n // tm,),
        in_specs=[pl.BlockSpec((tm, d), row), slab, slab, pl.BlockSpec((w, d), const), pl.BlockSpec((w, d), const)],
        out_specs=pl.BlockSpec((d, tm), lambda i: (0, i)),
        out_shape=jax.ShapeDtypeStruct((d, n), F32),
        compiler_params=_cparams("parallel"),
        name="out_proj",
    )(x2d, oa4, ob4, wa, wb)


BIG = 3.0e38


SUBLANES = 8


def _merge_exchange(n):
    pairs = []
    t = max(1, math.ceil(math.log2(n)))
    p = 1 << (t - 1)
    while p > 0:
        q, r, d = 1 << (t - 1), 0, p
        while d > 0:
            pairs.extend((i, i + d) for i in range(n - d) if (i & p) == r)
            d, q, r = q - p, q >> 1, p
        p >>= 1
    return pairs


def _vmax(a, b):
    if a is None:
        return b
    if b is None:
        return a
    return jnp.maximum(a, b)


def _vmin(a, b):
    if a is None or b is None:
        return None
    return jnp.minimum(a, b)


def _exchange(x, i, j):
    x[i], x[j] = _vmax(x[i], x[j]), _vmin(x[i], x[j])


def _top16(tiles):
    x = list(tiles) + [None] * (PEER_TOPK - len(tiles))
    for (i, j) in _merge_exchange(len(tiles)):
        _exchange(x, i, j)
    for shift in (4, 2, 1):
        y = [None if v is None else pltpu.roll(v, shift, 0) for v in x]
        x = [_vmax(x[k], y[PEER_TOPK - 1 - k]) for k in range(PEER_TOPK)]
        for d in (8, 4, 2, 1):
            for i in range(PEER_TOPK):
                if not i & d:
                    _exchange(x, i, i + d)
    return x


def _rows_sum(x):
    for shift in (4, 2, 1):
        x = x + pltpu.roll(x, shift, 0)
    return x


def _peer_select_kernel(ht_ref, g_ref, wq_ref, keys_ref,
                        hn_ref, r1_ref, w1_ref, cnt_ref, w0_ref, s_ref, *, nk, tn):
    h = ht_ref[...]
    ms = jnp.mean(h * h, axis=0, keepdims=True)
    hn = (h * lax.rsqrt(ms + NORM_EPS) * g_ref[...]).astype(BF16)
    hn_ref[...] = hn
    q = jnp.dot(wq_ref[...], hn, preferred_element_type=F32)
    half = q.shape[0] // (2 * PEER_HEADS)
    for k in range(2 * PEER_HEADS):
        qk = q[k * half:(k + 1) * half, :].astype(BF16)
        s_ref[k] = jnp.dot(keys_ref[k], qk, preferred_element_type=F32)
    nchunk = tn // LANES
    ntile = nk // SUBLANES
    sub = lax.broadcasted_iota(jnp.int32, (SUBLANES, LANES), 0)

    def pack(vals):
        out = vals[-1]
        for r in range(len(vals) - 2, -1, -1):
            out = jnp.where(sub == r, vals[r], out)
        return out

    def head_chunk(t, carry):
        hd = t // nchunk
        lanes = pl.ds(pl.multiple_of((t % nchunk) * LANES, LANES), LANES)
        rows = [slice(k * SUBLANES, (k + 1) * SUBLANES) for k in range(ntile)]
        s0 = [s_ref[2 * hd, r, lanes] for r in rows]
        s1 = [s_ref[2 * hd + 1, r, lanes] for r in rows]
        v0 = _top16(s0)
        v1 = _top16(s1)
        v1lo, v1hi, v0hi = pack(v1[0:8]), pack(v1[8:16]), pack(v0[8:16])
        cands = [v0[0] + v1lo, v0[0] + v1hi] + [v0[a] + v1lo for a in range(1, 8)] + [v0hi + v1[0]]
        best = _top16(cands)
        top, tau = best[0], best[PEER_TOPK - 1]
        z = _rows_sum(sum(jnp.where(c >= tau, jnp.exp(c - top), 0.0) for c in cands))
        inv_z = 1.0 / z
        for m in range(ntile // 2):
            cnt, rk1 = [], []
            for k in (2 * m, 2 * m + 1):
                c = jnp.zeros((SUBLANES, LANES), F32)
                r = jnp.zeros((SUBLANES, LANES), F32)
                for b in range(PEER_TOPK):
                    c = c + jnp.where(s0[k] + v1[b] >= tau, 1.0, 0.0)
                    r = r + jnp.where(v1[b] > s1[k], 1.0, 0.0)
                cnt.append(c)
                rk1.append(r)
            pair = slice(2 * m * SUBLANES, (2 * m + 2) * SUBLANES)
            both = lambda f: jnp.concatenate([f(2 * m), f(2 * m + 1)], axis=0)
            cnt_ref[hd, pair, lanes] = jnp.concatenate(cnt, axis=0)
            w0_ref[hd, pair, lanes] = both(lambda k: jnp.exp(s0[k] - v0[0]) * inv_z)
            r1_ref[hd, pair, lanes] = jnp.concatenate(rk1, axis=0).astype(BF16)
            w1_ref[hd, pair, lanes] = both(lambda k: jnp.exp(s1[k] - v1[0])).astype(BF16)
        return carry

    lax.fori_loop(0, PEER_HEADS * nchunk, head_chunk, 0)


def _peer_select(ht, g_col, wq_t, keys, tn):
    d, n = ht.shape
    nslab, nk, half = keys.shape
    tok = lambda i: (0, i)
    tok3 = lambda i: (0, 0, i)
    stat = lambda dt: jax.ShapeDtypeStruct((PEER_HEADS, nk, n), dt)
    return pl.pallas_call(
        functools.partial(_peer_select_kernel, nk=nk, tn=tn),
        grid=(n // tn,),
        in_specs=[pl.BlockSpec((d, tn), tok), pl.BlockSpec((d, 1), lambda i: (0, 0)),
                  pl.BlockSpec(wq_t.shape, lambda i: (0, 0)), pl.BlockSpec(keys.shape, lambda i: (0, 0, 0))],
        out_specs=[pl.BlockSpec((d, tn), tok)] + [pl.BlockSpec((PEER_HEADS, nk, tn), tok3)] * 4,
        out_shape=[jax.ShapeDtypeStruct((d, n), BF16), stat(BF16), stat(BF16), stat(F32), stat(F32)],
        scratch_shapes=[pltpu.VMEM((nslab, nk, tn), F32)],
        compiler_params=_cparams("parallel"),
        name="peer_select",
    )(ht, g_col, wq_t, keys)


def _gelu(x):
    return 0.5 * x * (1.0 + lax.erf(x * (2.0 ** -0.5)))


def _peer_dense_kernel(hn_ref, u_ref, vt_ref, r1_ref, w1_ref, cnt_ref, w0_ref, ht_ref, y_ref,
                       acc_ref, a0_ref, a1_ref, c0_ref, c1_ref, *, nk, te, tn, lc, rb, ne, nsteps):
    s = pl.program_id(0)

    @pl.when(s == 0)
    def _():
        for r in (a0_ref, a1_ref, c0_ref, c1_ref, acc_ref):
            r[...] = jnp.zeros_like(r)

    e = jnp.clip(s - 1, 0, nsteps - 1) % ne
    p2 = s - 2
    first = (jnp.maximum(p2, 0) % ne) == 0
    nslab = te // nk

    def stages(a_new, a_prev, c_new, c_prev):
        a_new[...] = jnp.dot(u_ref[...], hn_ref[...], preferred_element_type=F32)
        for ii in range(nslab):
            i = e * nslab + ii
            cnt_rows = [cnt_ref[hd, pl.ds(i, 1), :].astype(BF16) for hd in range(PEER_HEADS)]
            w0_rows = [w0_ref[hd, pl.ds(i, 1), :].astype(BF16) for hd in range(PEER_HEADS)]
            for c in range(tn // lc):
                ls = slice(c * lc, (c + 1) * lc)
                for jb in range(nk // rb):
                    js = slice(jb * rb, (jb + 1) * rb)
                    rs = slice(ii * nk + jb * rb, ii * nk + (jb + 1) * rb)
                    g = None
                    for hd in range(PEER_HEADS):
                        term = jnp.where(r1_ref[hd, js, ls] < cnt_rows[hd][:, ls], w1_ref[hd, js, ls],
                                         jnp.zeros((), BF16)) * w0_rows[hd][:, ls]
                        g = term if g is None else g + term
                    c_new[rs, ls] = g * _gelu(a_prev[rs, ls]).astype(BF16)
        contrib = jnp.dot(vt_ref[...], c_prev[...], preferred_element_type=F32)
        acc_ref[...] = jnp.where(first, contrib, acc_ref[...] + contrib)

    @pl.when(s % 2 == 0)
    def _():
        stages(a0_ref, a1_ref, c1_ref, c0_ref)

    @pl.when(s % 2 == 1)
    def _():
        stages(a1_ref, a0_ref, c0_ref, c1_ref)

    @pl.when((p2 >= 0) & (p2 % ne == ne - 1))
    def _():
        y_ref[...] = (ht_ref[...] + acc_ref[...]).T


def _peer_dense(hn, u, vt, r1, w1, cnt, w0, ht, *, tn, te, lc, rb):
    d, n = hn.shape
    ne = u.shape[0] // te
    nk = r1.shape[1]
    nsteps = (n // tn) * ne
    last = nsteps - 1
    pair = lambda s, lag: jnp.clip(s - lag, 0, last)
    stat = pl.BlockSpec((PEER_HEADS, nk, tn), lambda s: (0, 0, pair(s, 1) // ne))
    return pl.pallas_call(
        functools.partial(_peer_dense_kernel, nk=nk, te=te, tn=tn, lc=lc, rb=min(rb, nk), ne=ne, nsteps=nsteps),
        grid=(nsteps + 2,),
        in_specs=[pl.BlockSpec((d, tn), lambda s: (0, pair(s, 0) // ne)),
                  pl.BlockSpec((te, d), lambda s: (pair(s, 0) % ne, 0)),
                  pl.BlockSpec((d, te), lambda s: (0, pair(s, 2) % ne)),
                  stat, stat, stat, stat,
                  pl.BlockSpec((d, tn), lambda s: (0, pair(s, 2) // ne))],
        out_specs=pl.BlockSpec((tn, d), lambda s: (pair(s, 2) // ne, 0)),
        out_shape=jax.ShapeDtypeStruct((n, d), F32),
        scratch_shapes=[pltpu.VMEM((d, tn), F32), pltpu.VMEM((te, tn), F32), pltpu.VMEM((te, tn), F32),
                        pltpu.VMEM((te, tn), BF16), pltpu.VMEM((te, tn), BF16)],
        compiler_params=_cparams("arbitrary", interleave=True),
        name="peer_dense",
    )(hn, u, vt, r1, w1, cnt, w0, ht)


def _peer(ht, g_col, wq_t, keys, u, vt, *, tn_sel, tn, te, lc, rb):
    hn, r1, w1, cnt, w0 = _peer_select(ht, g_col, wq_t, keys, tn_sel)
    return _peer_dense(hn, u, vt, r1, w1, cnt, w0, ht, tn=tn, te=te, lc=lc, rb=rb)


def kernel(x_prompt, x_sample, cache_a_k, cache_a_v, cache_b_k, cache_b_v, norm_attn, w_in, g_qa, g_ka, g_qb, g_kb, sinks, w_o, norm_ffn, peer_wq, peer_keys, peer_u, peer_v):
    b, l, d = x_prompt.shape
    s, ns, _ = x_sample.shape
    assert ns == NEW and w_in.shape[0] == 1
    la, lb = cache_a_k.shape[2], cache_b_k.shape[2]
    w = A_WIDTH
    perm = _qb_perm()

    wl = w_in[0]
    w_all = jnp.concatenate([wl[:, :3 * w], wl[:, 3 * w:4 * w][:, perm], wl[:, 4 * w:]], axis=1).astype(BF16)
    seg = (jnp.arange(w)[:, None] // HEAD_DIM == jnp.arange(w)[None, :] // HEAD_DIM).astype(BF16)
    t8 = lambda g: jnp.tile(g, A_HEADS)[None, :]
    gains = (t8(g_qa[0]), t8(g_ka[0]), t8(g_qb[0]), jnp.tile(g_kb[0], B_KV_HEADS)[None, :])
    sb = _alibi_slopes(B_HEADS)
    slopes_perm = [sb[(h % 2) * B_GROUP + h // 2] for h in range(B_HEADS)]
    sinks_perm = jnp.stack([sinks[0, (h % 2) * B_GROUP + h // 2] for h in range(B_HEADS)])
    wo_a = w_o[0, :w, :].astype(BF16)
    wo_b = w_o[0, w:, :][perm, :].astype(BF16)
    g_col = norm_ffn[0][:, None]
    wq_t = peer_wq[0].T.astype(BF16)
    nk = peer_keys.shape[3]
    keys = peer_keys[0].reshape(2 * PEER_HEADS, nk, peer_keys.shape[4]).astype(BF16)
    u = peer_u[0].astype(BF16)
    vt = peer_v[0].T.astype(BF16)

    xp = x_prompt.reshape(b * l, d)
    qa4, ka4, va4, qb4, ka, va, kb, vb = _project(xp, norm_attn, w_all, seg, *gains, 512)
    seq4 = lambda t: t.reshape(t.shape[0], b, l, LANES)
    oa4 = _mixer(seq4(qa4), seq4(ka4), seq4(va4), dils=[dl for (_, dl) in A_PATTERNS],
                 slopes=_alibi_slopes(A_HEADS))
    ob4 = _mixer(seq4(qb4), kb.reshape(1, b, l, LANES), vb.reshape(1, b, l, LANES), dils=[1],
                 slopes=slopes_perm, sinks=sinks_perm)
    ht_p = _outproj(xp, oa4.reshape(NSLAB, b * l, LANES), ob4.reshape(NSLAB, b * l, LANES), wo_a, wo_b, 512)
    y_p = _peer(ht_p, g_col, wq_t, keys, u, vt, tn_sel=512, tn=512, te=1024, lc=256, rb=64)
    na = min(la, l)
    nb = min(lb, l)
    pak = ka.reshape(b, l, A_HEADS, HEAD_DIM)[None, :, l - na:]
    pav = va.reshape(b, l, A_HEADS, HEAD_DIM)[None, :, l - na:]
    pbk = kb.reshape(b, l, B_KV_HEADS, HEAD_DIM)[None, :, l - nb:]
    pbv = vb.reshape(b, l, B_KV_HEADS, HEAD_DIM)[None, :, l - nb:]

    xs = x_sample.reshape(s * ns, d)
    qa4, _, _, qb4, ka, va, kb, vb = _project(xs, norm_attn, w_all, seg, *gains, 512)
    r3 = lambda t: t.reshape(s, ns, t.shape[-1])
    wide = lambda t4: jnp.transpose(t4, (1, 0, 2)).reshape(s, ns, w)
    slabs = lambda t: jnp.transpose(t.reshape(s * ns, NSLAB, LANES), (1, 0, 2))
    heads = lambda t: t.reshape(s, ns, A_HEADS, HEAD_DIM)
    q64 = jnp.transpose(heads(wide(qa4)), (0, 2, 1, 3)).reshape(s, HQ, HEAD_DIM)
    sak, sav, o64 = _sample_a5(q64, heads(ka), heads(va), cache_a_k, cache_a_v)
    oa = jnp.transpose(o64.reshape(s, A_HEADS, ns, HEAD_DIM), (0, 2, 1, 3)).reshape(s, ns, w)
    sbk, sbv, ob = _sample_b(sinks_perm, slopes_perm, wide(qb4), r3(kb), r3(vb),
                             cache_b_k[0].reshape(s, lb, LANES), cache_b_v[0].reshape(s, lb, LANES))
    ht_s = _outproj(xs, slabs(oa), slabs(ob), wo_a, wo_b, 512)
    y_s = _peer(ht_s, g_col, wq_t, keys, u, vt, tn_sel=512, tn=512, te=1024, lc=256, rb=64)

    return (y_p.reshape(b, l, d), y_s.reshape(s, ns, d), pak, pav, pbk, pbv,
            sak, sav,
            sbk.reshape(1, s, lb, B_KV_HEADS, HEAD_DIM), sbv.reshape(1, s, lb, B_KV_HEADS, HEAD_DIM))
```

```python
import functools
import math

import jax
import jax.numpy as jnp
from jax import lax
from jax.experimental import pallas as pl
from jax.experimental.pallas import tpu as pltpu

HEAD_DIM = 64
A_HEADS = 8
B_HEADS = 8
B_KV_HEADS = 2
B_GROUP = B_HEADS // B_KV_HEADS
A_PATTERNS = ((128, 1), (512, 4), (2048, 16))
BAND = 128
A_WIDTH = A_HEADS * HEAD_DIM
PEER_HEADS = 8
PEER_TOPK = 16
NORM_EPS = 1e-6
NEG = -1e30
SCALE = HEAD_DIM ** -0.5
LANES = 128
VMEM_LIMIT_BYTES = 56 * 1024 * 1024

BF16 = jnp.bfloat16
F32 = jnp.float32


def _alibi_slopes(n):
    return [2.0 ** (-8.0 * (i + 1) / n) for i in range(n)]


def _cparams(*sem, interleave=False):
    del interleave
    return pltpu.CompilerParams(dimension_semantics=sem, vmem_limit_bytes=VMEM_LIMIT_BYTES)


def _head_rms(h, seg, gain):
    sq = h * h
    hi = sq.astype(BF16)
    lo = (sq - hi.astype(F32)).astype(BF16)
    ms = (jnp.dot(hi, seg, preferred_element_type=F32)
          + jnp.dot(lo, seg, preferred_element_type=F32)) * (1.0 / HEAD_DIM)
    return h * lax.rsqrt(ms + NORM_EPS) * gain


NSLAB = A_WIDTH // LANES


def _proj_kernel(x_ref, g_ref, w_ref, seg_ref, gqa_ref, gka_ref, gqb_ref, gkb_ref,
                 qa4_ref, ka4_ref, va4_ref, qb4_ref, ka_ref, va_ref, kb_ref, vb_ref):
    x = x_ref[...]
    ms = jnp.mean(x * x, axis=-1, keepdims=True)
    xn = (x * lax.rsqrt(ms + NORM_EPS) * g_ref[...]).astype(BF16)
    h = jnp.dot(xn, w_ref[...], preferred_element_type=F32)
    seg = seg_ref[...]
    w = A_WIDTH
    qa = _head_rms(h[:, 0:w], seg, gqa_ref[...])
    ka = _head_rms(h[:, w:2 * w], seg, gka_ref[...])
    va = h[:, 2 * w:3 * w]
    qb = _head_rms(h[:, 3 * w:4 * w], seg, gqb_ref[...])
    ka_ref[...] = ka
    va_ref[...] = va
    kb_ref[...] = _head_rms(h[:, 4 * w:4 * w + LANES], seg[0:LANES, 0:LANES], gkb_ref[...])
    vb_ref[...] = h[:, 4 * w + LANES:4 * w + 2 * LANES]
    for s in range(NSLAB):
        sl = slice(s * LANES, (s + 1) * LANES)
        qa4_ref[s] = qa[:, sl]
        ka4_ref[s] = ka[:, sl]
        va4_ref[s] = va[:, sl]
        qb4_ref[s] = qb[:, sl]


def _project(x2d, norm_g, w_bf16, seg, gqa, gka, gqb, gkb, tm):
    n, d = x2d.shape
    nc = w_bf16.shape[1]
    w = A_WIDTH
    row = lambda i: (i, 0)
    slab = lambda i: (0, i, 0)
    const = lambda i: (0, 0)
    outs = ([jax.ShapeDtypeStruct((NSLAB, n, LANES), F32)] * 4 + [jax.ShapeDtypeStruct((n, w), F32)] * 2
            + [jax.ShapeDtypeStruct((n, LANES), F32)] * 2)
    return pl.pallas_call(
        _proj_kernel,
        grid=(n // tm,),
        in_specs=[pl.BlockSpec((tm, d), row), pl.BlockSpec((1, d), const), pl.BlockSpec((d, nc), const),
                  pl.BlockSpec((w, w), const), pl.BlockSpec((1, w), const), pl.BlockSpec((1, w), const),
                  pl.BlockSpec((1, w), const), pl.BlockSpec((1, LANES), const)],
        out_specs=([pl.BlockSpec((NSLAB, tm, LANES), slab)] * 4 + [pl.BlockSpec((tm, w), row)] * 2
                   + [pl.BlockSpec((tm, LANES), row)] * 2),
        out_shape=outs,
        compiler_params=_cparams("parallel"),
        name="qkv_proj",
    )(x2d, norm_g, w_bf16, seg, gqa, gka, gqb, gkb)


def _band_kernel(*refs, tq, dil, slopes, with_sink, with_lse):
    if with_sink:
        sink_ref, refs = refs[0], refs[1:]
    q_ref, kc_ref, kp_ref, vc_ref, vp_ref = refs[:5]
    o_ref = refs[5]
    lse_ref = refs[6] if with_lse else None
    kbuf, vbuf, bias_ref = refs[-3:]
    i = pl.program_id(2)
    nsub = tq // BAND

    @pl.when((pl.program_id(0) == 0) & (pl.program_id(1) == 0) & (i == 0))
    def _():
        r = lax.broadcasted_iota(jnp.int32, (BAND, 2 * BAND), 0)
        c = lax.broadcasted_iota(jnp.int32, (BAND, 2 * BAND), 1)
        dist = r + BAND - c
        valid = (dist >= 0) & (dist <= BAND)
        distf = (dist * dil).astype(F32)
        for h in range(A_HEADS):
            bias_ref[h] = jnp.where(valid, -slopes[h] * distf, NEG)

    kbuf[0:BAND, :] = kp_ref[0].astype(BF16)
    kbuf[BAND:BAND + tq, :] = kc_ref[0].astype(BF16)
    vbuf[0:BAND, :] = vp_ref[0].astype(BF16)
    vbuf[BAND:BAND + tq, :] = vc_ref[0].astype(BF16)

    lane = lax.broadcasted_iota(jnp.int32, (BAND, LANES), 1)
    col = lax.broadcasted_iota(jnp.int32, (1, 2 * BAND), 1)
    prev_cols = (col < BAND).astype(F32)

    def body(j, carry):
        r0 = pl.multiple_of(j * BAND, BAND)
        pen = jnp.where((i * nsub + j) == 0, NEG, 0.0) * prev_cols
        for hp in range(A_HEADS // 2):
            sl = slice(hp * LANES, (hp + 1) * LANES)
            qs = q_ref[0, pl.ds(r0, BAND), sl]
            kw = kbuf[pl.ds(r0, 2 * BAND), sl]
            vw = vbuf[pl.ds(r0, 2 * BAND), sl]
            outs, lses = [], []
            for e in range(2):
                h = 2 * hp + e
                qm = jnp.where((lane >= HEAD_DIM) == bool(e), qs, 0.0).astype(BF16)
                s = lax.dot_general(qm, kw, (((1,), (1,)), ((), ())), preferred_element_type=F32)
                s = s * SCALE + bias_ref[h] + pen
                m = jnp.max(s, axis=-1, keepdims=True)
                if with_sink:
                    m = jnp.maximum(m, sink_ref[h])
                p = jnp.exp(s - m)
                den = jnp.sum(p, axis=-1, keepdims=True)
                if with_sink:
                    den = den + jnp.exp(sink_ref[h] - m)
                o = jnp.dot(p.astype(BF16), vw, preferred_element_type=F32)
                outs.append(o / den)
                lses.append(m + jnp.log(den))
            hi = lane >= HEAD_DIM
            o_ref[0, pl.ds(r0, BAND), sl] = jnp.where(hi, outs[1], outs[0])
            if with_lse:
                lse_ref[0, pl.ds(r0, BAND), sl] = jnp.where(hi, lses[1], lses[0])
        return carry

    lax.fori_loop(0, nsub, body, 0)


def _band_attention(q, k, v, *, dil, tq, slopes, sinks=None, with_lse=True):
    b, r, cw = q.shape
    w = A_WIDTH
    c = cw // w
    assert r % tq == 0 and tq % BAND == 0
    nsub = tq // BAND
    cur = lambda bi, ci, ii: (bi, ii, ci)
    prev = lambda bi, ci, ii: (bi, jnp.maximum(ii * nsub - 1, 0), ci)
    in_specs = [pl.BlockSpec((1, tq, w), cur), pl.BlockSpec((1, tq, w), cur), pl.BlockSpec((1, BAND, w), prev),
                pl.BlockSpec((1, tq, w), cur), pl.BlockSpec((1, BAND, w), prev)]
    args = [q, k, k, v, v]
    if sinks is not None:
        in_specs = [pl.BlockSpec(memory_space=pltpu.SMEM)] + in_specs
        args = [sinks] + args
    n_out = 2 if with_lse else 1
    out = pl.pallas_call(
        functools.partial(_band_kernel, tq=tq, dil=dil, slopes=slopes, with_sink=sinks is not None,
                          with_lse=with_lse),
        grid=(b, c, r // tq),
        in_specs=in_specs,
        out_specs=[pl.BlockSpec((1, tq, w), cur)] * n_out,
        out_shape=[jax.ShapeDtypeStruct(q.shape, F32)] * n_out,
        scratch_shapes=[pltpu.VMEM((BAND + tq, w), BF16), pltpu.VMEM((BAND + tq, w), BF16),
                        pltpu.VMEM((A_HEADS, BAND, 2 * BAND), F32)],
        compiler_params=_cparams("arbitrary", "arbitrary", "arbitrary"),
        name=f"band_attn_d{dil}",
    )(*args)
    return out if with_lse else out[0]


def _prompt_mixers(qa, ka, va, qb, kbx, vbx, sinks_perm, b, l):
    w = A_WIDTH
    slopes_a = _alibi_slopes(A_HEADS)
    sb = _alibi_slopes(B_HEADS)
    slopes_b = [sb[(h % 2) * B_GROUP + h // 2] for h in range(B_HEADS)]
    branches = []
    for (_, dil) in A_PATTERNS:
        r = l // dil
        tq = min(r, 1024)
        view = lambda t: t.reshape(b, r, dil * w)
        o, lse = _band_attention(view(qa), view(ka), view(va), dil=dil, tq=tq, slopes=slopes_a)
        branches.append((o.reshape(b * l, w), lse.reshape(b * l, w)))
    view = lambda t: t.reshape(b, l, w)
    ob = _band_attention(view(qb), view(kbx), view(vbx), dil=1, tq=min(l, 1024), slopes=slopes_b,
                         sinks=sinks_perm, with_lse=False)
    return branches, ob.reshape(b * l, w)


MIXER_UNROLL = 4


def _pick(idx, values):
    out = jnp.float32(values[-1])
    for i in range(len(values) - 2, -1, -1):
        out = jnp.where(idx == i, jnp.float32(values[i]), out)
    return out


def _mixer_kernel(*refs, dils, slopes, with_sink, seq):
    if with_sink:
        sink_ref, refs = refs[0], refs[1:]
    q_ref, k_ref, v_ref, o_ref, kpad, vpad, bias_ref = refs[:7]
    ob_ref, lse_ref = refs[7:9] if len(dils) > 1 else (None, None)
    hp = pl.program_id(1)
    pad = BAND * max(dils)

    @pl.when((pl.program_id(0) == 0) & (hp == 0))
    def _():
        kpad[0:pad, :] = jnp.zeros((pad, LANES), F32)
        vpad[0:pad, :] = jnp.zeros((pad, LANES), F32)

    kpad[pad:pad + seq, :] = k_ref[0, 0]
    vpad[pad:pad + seq, :] = v_ref[0, 0]

    r = lax.broadcasted_iota(jnp.int32, (BAND, 2 * BAND), 0)
    c = lax.broadcasted_iota(jnp.int32, (BAND, 2 * BAND), 1)
    dist = r + BAND - c
    valid = (dist >= 0) & (dist <= BAND)
    for bi, dil in enumerate(dils):
        distf = (dist * dil).astype(F32)
        for e in range(2):
            slope = _pick(hp, [slopes[2 * g + e] for g in range(NSLAB)])
            bias_ref[bi, e] = jnp.where(valid, -slope * distf, NEG)

    lane = lax.broadcasted_iota(jnp.int32, (BAND, LANES), 1)
    hi = lane >= HEAD_DIM
    col = lax.broadcasted_iota(jnp.int32, (1, 2 * BAND), 1)
    prev_cols = (col < BAND).astype(F32)
    nt = (((1,), (1,)), ((), ()))

    for bi, dil in enumerate(dils):
        nblk = seq // (dil * BAND)

        def block(t, bi=bi, dil=dil, nblk=nblk):
            res = t // nblk
            j = t % nblk
            base = res + dil * BAND * j
            if dil == 1:
                base = pl.multiple_of(base, BAND)
                rows = pl.ds(base, BAND)
                win = pl.ds(pl.multiple_of(pad + base - BAND, BAND), 2 * BAND)
            else:
                rows = pl.ds(base, BAND, stride=dil)
                win = pl.ds(pad + base - dil * BAND, 2 * BAND, stride=dil)
            qs = q_ref[0, 0, rows, :]
            kw = kpad[win, :].astype(BF16)
            vw = vpad[win, :].astype(BF16)
            pen = jnp.where(j == 0, NEG, 0.0) * prev_cols
            outs, lses = [], []
            for e in range(2):
                qm = jnp.where(hi == bool(e), qs, 0.0).astype(BF16)
                s = lax.dot_general(qm, kw, nt, preferred_element_type=F32)
                s = s * SCALE + bias_ref[bi, e] + pen
                m = jnp.max(s, axis=-1, keepdims=True)
                if with_sink:
                    sink = sink_ref[2 * hp + e]
                    m = jnp.maximum(m, sink)
                p = jnp.exp(s - m)
                den = jnp.sum(p, axis=-1, keepdims=True)
                if with_sink:
                    den = den + jnp.exp(sink - m)
                o = jnp.dot(p.astype(BF16), vw, preferred_element_type=F32)
                outs.append(o / den)
                lses.append(m + jnp.log(den))
            if len(dils) == 1:
                o_ref[0, 0, rows, :] = jnp.where(hi, outs[1], outs[0])
            else:
                ob_ref[bi, rows, :] = jnp.where(hi, outs[1], outs[0])
                lse_ref[bi, rows, :] = jnp.where(hi, lses[1], lses[0])

        def blocks(t, carry, block=block):
            for uu in range(MIXER_UNROLL):
                block(t * MIXER_UNROLL + uu)
            return carry

        assert (dil * nblk) % MIXER_UNROLL == 0
        lax.fori_loop(0, dil * nblk // MIXER_UNROLL, blocks, 0)

    if len(dils) > 1:
        def merge(t, carry):
            rows = pl.ds(pl.multiple_of(t * BAND, BAND), BAND)
            ls = [lse_ref[bi, rows, :] for bi in range(len(dils))]
            mx = functools.reduce(jnp.maximum, ls)
            ws = [jnp.exp(l - mx) for l in ls]
            num = sum(w * ob_ref[bi, rows, :] for bi, w in enumerate(ws))
            o_ref[0, 0, rows, :] = num / sum(ws)
            return carry

        lax.fori_loop(0, seq // BAND, merge, 0)


def _mixer(q4, k4, v4, *, dils, slopes, sinks=None):
    nslab, b, l, _ = q4.shape
    assert l % (BAND * max(dils)) == 0
    shared = k4.shape[0] == 1
    qmap = lambda bi, hp: (hp, bi, 0, 0)
    kmap = (lambda bi, hp: (0, bi, 0, 0)) if shared else qmap
    blk = (1, 1, l, LANES)
    in_specs = [pl.BlockSpec(blk, qmap), pl.BlockSpec(blk, kmap), pl.BlockSpec(blk, kmap)]
    args = [q4, k4, v4]
    if sinks is not None:
        in_specs = [pl.BlockSpec(memory_space=pltpu.SMEM)] + in_specs
        args = [sinks] + args
    pad = BAND * max(dils)
    scratch = [pltpu.VMEM((pad + l, LANES), F32), pltpu.VMEM((pad + l, LANES), F32),
               pltpu.VMEM((len(dils), 2, BAND, 2 * BAND), F32)]
    if len(dils) > 1:
        scratch += [pltpu.VMEM((len(dils), l, LANES), F32), pltpu.VMEM((len(dils), l, LANES), F32)]
    return pl.pallas_call(
        functools.partial(_mixer_kernel, dils=tuple(dils), slopes=slopes, with_sink=sinks is not None, seq=l),
        grid=(b, nslab),
        in_specs=in_specs,
        out_specs=pl.BlockSpec(blk, qmap),
        out_shape=jax.ShapeDtypeStruct(q4.shape, F32),
        scratch_shapes=scratch,
        compiler_params=_cparams("arbitrary", "arbitrary"),
        name="mixer_b" if shared else "mixer_a",
    )(*args)


def _qb_perm():
    idx = []
    for g in range(B_GROUP):
        for kv in range(B_KV_HEADS):
            base = (kv * B_GROUP + g) * HEAD_DIM
            idx.extend(range(base, base + HEAD_DIM))
    return jnp.asarray(idx, jnp.int32)


NEW = 8
HQ = A_HEADS * NEW


def _block_diag_rows(q8, width):
    rep = jnp.concatenate([q8] * A_HEADS, axis=0)
    r = lax.broadcasted_iota(jnp.int32, (HQ, width), 0)
    c = lax.broadcasted_iota(jnp.int32, (HQ, width), 1)
    return jnp.where(r // NEW == c // HEAD_DIM, rep, 0.0)


def _sample_a_kernel(q_ref, kn_ref, vn_ref, ck_ref, cv_ref, nk_ref, nv_ref, o_ref,
                     b1_ref, b4_ref, b16_ref, bn_ref, knb, vnb, *, la, slopes):
    spans = [w for (w, _) in A_PATTERNS]
    bias_refs = [b1_ref, b4_ref, b16_ref]

    @pl.when(pl.program_id(0) == 0)
    def _():
        for bi, (wdw, dil) in enumerate(A_PATTERNS):
            ncol = spans[bi]
            r = lax.broadcasted_iota(jnp.int32, (HQ, ncol), 0)
            c = lax.broadcasted_iota(jnp.int32, (HQ, ncol), 1)
            dist = (r % NEW) + ncol - c
            valid = (dist <= wdw) & (dist % dil == 0)
            slope = jnp.zeros((HQ, ncol), F32)
            for h in range(A_HEADS):
                slope = jnp.where(r // NEW == h, slopes[h], slope)
            bias_refs[bi][...] = jnp.where(valid, -slope * dist.astype(F32), NEG)
            rn = lax.broadcasted_iota(jnp.int32, (HQ, LANES), 0)
            cn = lax.broadcasted_iota(jnp.int32, (HQ, LANES), 1)
            dn = (rn % NEW) - cn
            vn_ok = (dn >= 0) & (dn % dil == 0) & (cn < NEW)
            sl = jnp.zeros((HQ, LANES), F32)
            for h in range(A_HEADS):
                sl = jnp.where(rn // NEW == h, slopes[h], sl)
            bn_ref[bi] = jnp.where(vn_ok, -sl * dn.astype(F32), NEG)
        knb[...] = jnp.zeros_like(knb)
        vnb[...] = jnp.zeros_like(vnb)

    kn = kn_ref[0]
    vn = vn_ref[0]
    nk_ref[0, 0:la - NEW, :] = ck_ref[0, NEW:la, :]
    nk_ref[0, la - NEW:la, :] = kn
    nv_ref[0, 0:la - NEW, :] = cv_ref[0, NEW:la, :]
    nv_ref[0, la - NEW:la, :] = vn
    knb[0:NEW, :] = kn
    vnb[0:NEW, :] = vn

    qbd = _block_diag_rows(q_ref[0], A_WIDTH).astype(BF16)
    nt = (((1,), (1,)), ((), ()))
    s_new = lax.dot_general(qbd, knb[...].astype(BF16), nt, preferred_element_type=F32) * SCALE
    ms, dens, nums = [], [], []
    for bi in range(len(A_PATTERNS)):
        ncol = spans[bi]
        kc = ck_ref[0, la - ncol:la, :].astype(BF16)
        vc = cv_ref[0, la - ncol:la, :].astype(BF16)
        sc = lax.dot_general(qbd, kc, nt, preferred_element_type=F32) * SCALE + bias_refs[bi][...]
        sn = s_new + bn_ref[bi]
        m = jnp.maximum(jnp.max(sc, axis=-1, keepdims=True), jnp.max(sn, axis=-1, keepdims=True))
        pc = jnp.exp(sc - m)
        pn = jnp.exp(sn - m)
        dens.append(jnp.sum(pc, axis=-1, keepdims=True) + jnp.sum(pn, axis=-1, keepdims=True))
        nums.append(jnp.dot(pc.astype(BF16), vc, preferred_element_type=F32)
                    + jnp.dot(pn.astype(BF16), vnb[...].astype(BF16), preferred_element_type=F32))
        ms.append(m)
    mx = jnp.maximum(jnp.maximum(ms[0], ms[1]), ms[2])
    den = 0.0
    num = 0.0
    for bi in range(len(A_PATTERNS)):
        wt = jnp.exp(ms[bi] - mx)
        den = den + dens[bi] * wt
        num = num + nums[bi] * wt
    out = _block_diag_rows_keep(num / den, A_WIDTH)
    o_ref[0] = sum(out[h * NEW:(h + 1) * NEW, :] for h in range(A_HEADS))


def _block_diag_rows_keep(x, width):
    r = lax.broadcasted_iota(jnp.int32, (HQ, width), 0)
    c = lax.broadcasted_iota(jnp.int32, (HQ, width), 1)
    return jnp.where(r // NEW == c // HEAD_DIM, x, 0.0)


def _sample_a(q, kn, vn, ck, cv):
    s, la, w = ck.shape
    assert la >= max(wd for (wd, _) in A_PATTERNS)
    seq = lambda i: (i, 0, 0)
    small = pl.BlockSpec((1, NEW, w), seq)
    big = pl.BlockSpec((1, la, w), seq)
    spans = [wd for (wd, _) in A_PATTERNS]
    return pl.pallas_call(
        functools.partial(_sample_a_kernel, la=la, slopes=_alibi_slopes(A_HEADS)),
        grid=(s,),
        in_specs=[small, small, small, big, big],
        out_specs=[big, big, small],
        out_shape=[jax.ShapeDtypeStruct(ck.shape, F32), jax.ShapeDtypeStruct(cv.shape, F32),
                   jax.ShapeDtypeStruct(q.shape, F32)],
        scratch_shapes=[pltpu.VMEM((HQ, spans[0]), F32), pltpu.VMEM((HQ, spans[1]), F32),
                        pltpu.VMEM((HQ, spans[2]), F32), pltpu.VMEM((len(A_PATTERNS), HQ, LANES), F32),
                        pltpu.VMEM((LANES, w), F32), pltpu.VMEM((LANES, w), F32)],
        compiler_params=_cparams("arbitrary"),
        name="sample_attn_a",
    )(q, kn, vn, ck, cv)


SAMPLE_CHUNK = 512


def _sample_a5_kernel(q_ref, kn_ref, vn_ref, ck_ref, cv_ref, nk_hbm, nv_hbm, o_ref,
                      b16_ref, b4_ref, b1_ref, bn_ref, sem, *, la, slopes):
    b = pl.program_id(0)
    copies = [
        pltpu.make_async_copy(ck_ref.at[0, 0, pl.ds(NEW, la - NEW)], nk_hbm.at[0, b, pl.ds(0, la - NEW)], sem.at[0]),
        pltpu.make_async_copy(cv_ref.at[0, 0, pl.ds(NEW, la - NEW)], nv_hbm.at[0, b, pl.ds(0, la - NEW)], sem.at[1]),
        pltpu.make_async_copy(kn_ref.at[0], nk_hbm.at[0, b, pl.ds(la - NEW, NEW)], sem.at[2]),
        pltpu.make_async_copy(vn_ref.at[0], nv_hbm.at[0, b, pl.ds(la - NEW, NEW)], sem.at[3]),
    ]
    for cp in copies:
        cp.start()

    spans = [wd for (wd, _) in A_PATTERNS]
    bias_refs = [b1_ref, b4_ref, b16_ref]

    @pl.when(b == 0)
    def _():
        for bi, (wdw, dil) in enumerate(A_PATTERNS):
            ncol = spans[bi] * A_HEADS
            r = lax.broadcasted_iota(jnp.int32, (HQ, ncol), 0)
            c = lax.broadcasted_iota(jnp.int32, (HQ, ncol), 1)
            dist = (r % NEW) + spans[bi] - c // A_HEADS
            valid = (dist <= wdw) & (dist % dil == 0) & (c % A_HEADS == r // NEW)
            slope = jnp.zeros((HQ, ncol), F32)
            for h in range(A_HEADS):
                slope = jnp.where(r // NEW == h, slopes[h], slope)
            bias_refs[bi][...] = jnp.where(valid, -slope * dist.astype(F32), NEG)
            rn = lax.broadcasted_iota(jnp.int32, (HQ, HQ), 0)
            cn = lax.broadcasted_iota(jnp.int32, (HQ, HQ), 1)
            dn = (rn % NEW) - cn // A_HEADS
            ok = (dn >= 0) & (dn % dil == 0) & (cn % A_HEADS == rn // NEW)
            sl = jnp.zeros((HQ, HQ), F32)
            for h in range(A_HEADS):
                sl = jnp.where(rn // NEW == h, slopes[h], sl)
            bn_ref[bi] = jnp.where(ok, -sl * dn.astype(F32), NEG)

    nt = (((1,), (1,)), ((), ()))
    q = q_ref[0].astype(BF16)

    def scores(k3):
        k2 = k3.reshape(k3.shape[0] * A_HEADS, HEAD_DIM).astype(BF16)
        return lax.dot_general(q, k2, nt, preferred_element_type=F32) * SCALE

    def flat_v(v3):
        return v3.reshape(v3.shape[0] * A_HEADS, HEAD_DIM).astype(BF16)

    def fold(state, s, v2):
        m_new = jnp.max(s, axis=-1, keepdims=True)
        if state is not None:
            m_old, den, num = state
            m_new = jnp.maximum(m_new, m_old)
        p = jnp.exp(s - m_new)
        d_new = jnp.sum(p, axis=-1, keepdims=True)
        n_new = jnp.dot(p.astype(BF16), v2, preferred_element_type=F32)
        if state is not None:
            alpha = jnp.exp(m_old - m_new)
            d_new = d_new + den * alpha
            n_new = n_new + num * alpha
        return m_new, d_new, n_new

    states = [None, None, None]
    nchunk = la // SAMPLE_CHUNK
    ch_cols = SAMPLE_CHUNK * A_HEADS
    for ci in range(nchunk):
        rows = slice(ci * SAMPLE_CHUNK, (ci + 1) * SAMPLE_CHUNK)
        s = scores(ck_ref[0, 0, rows])
        v2 = flat_v(cv_ref[0, 0, rows])
        states[2] = fold(states[2], s + b16_ref[:, ci * ch_cols:(ci + 1) * ch_cols], v2)
        if ci == nchunk - 1:
            for bi in (0, 1):
                ncol = spans[bi] * A_HEADS
                states[bi] = fold(states[bi], s[:, ch_cols - ncol:] + bias_refs[bi][...], v2[ch_cols - ncol:, :])
    s_new = scores(kn_ref[0])
    v_new = flat_v(vn_ref[0])
    for bi in range(len(A_PATTERNS)):
        states[bi] = fold(states[bi], s_new + bn_ref[bi], v_new)
    mx = functools.reduce(jnp.maximum, [st[0] for st in states])
    den = 0.0
    num = 0.0
    for (m, dd, nn) in states:
        wt = jnp.exp(m - mx)
        den = den + dd * wt
        num = num + nn * wt
    o_ref[0] = num / den

    for cp in copies:
        cp.wait()


def _sample_a5(q64, kn, vn, ck5, cv5):
    _, s, la, nh, hd = ck5.shape
    spans = [wd for (wd, _) in A_PATTERNS]
    assert la == max(spans) and la % SAMPLE_CHUNK == 0 and spans[1] == SAMPLE_CHUNK and nh == A_HEADS
    seq3 = lambda i: (i, 0, 0)
    seq4 = lambda i: (i, 0, 0, 0)
    big = pl.BlockSpec((1, 1, la, nh, hd), lambda i: (0, i, 0, 0, 0))
    anyspec = pl.BlockSpec(memory_space=pl.ANY)
    return pl.pallas_call(
        functools.partial(_sample_a5_kernel, la=la, slopes=_alibi_slopes(A_HEADS)),
        grid=(s,),
        in_specs=[pl.BlockSpec((1, HQ, hd), seq3), pl.BlockSpec((1, NEW, nh, hd), seq4),
                  pl.BlockSpec((1, NEW, nh, hd), seq4), big, big],
        out_specs=[anyspec, anyspec, pl.BlockSpec((1, HQ, hd), seq3)],
        out_shape=[jax.ShapeDtypeStruct(ck5.shape, F32), jax.ShapeDtypeStruct(cv5.shape, F32),
                   jax.ShapeDtypeStruct(q64.shape, F32)],
        scratch_shapes=[pltpu.VMEM((HQ, spans[2] * nh), F32), pltpu.VMEM((HQ, spans[1] * nh), F32),
                        pltpu.VMEM((HQ, spans[0] * nh), F32), pltpu.VMEM((len(A_PATTERNS), HQ, HQ), F32),
                        pltpu.SemaphoreType.DMA((4,))],
        compiler_params=_cparams("arbitrary"),
        name="sample_attn_a5",
    )(q64, kn, vn, ck5, cv5)


ROLL_ROWS = 64


def _sample_at_kernel(q_ref, kn_ref, vn_ref, ck_ref, cv_ref, nk_ref, nv_ref, o_ref, bias_ref, bn_ref, knb, vnb,
                      *, la, slopes):
    w = A_WIDTH
    nbr = len(A_PATTERNS)

    @pl.when(pl.program_id(0) == 0)
    def _():
        r = lax.broadcasted_iota(jnp.int32, (HQ, la), 0)
        c = lax.broadcasted_iota(jnp.int32, (HQ, la), 1)
        dist = (r % NEW) + la - c
        slope = jnp.zeros((HQ, la), F32)
        rn = lax.broadcasted_iota(jnp.int32, (HQ, LANES), 0)
        cn = lax.broadcasted_iota(jnp.int32, (HQ, LANES), 1)
        dn = (rn % NEW) - cn
        sl = jnp.zeros((HQ, LANES), F32)
        for h in range(A_HEADS):
            slope = jnp.where(r // NEW == h, slopes[h], slope)
            sl = jnp.where(rn // NEW == h, slopes[h], sl)
        for bi, (wdw, dil) in enumerate(A_PATTERNS):
            bias_ref[bi] = jnp.where((dist <= wdw) & (dist % dil == 0), -slope * dist.astype(F32), NEG)
            bn_ref[bi] = jnp.where((dn >= 0) & (dn % dil == 0) & (cn < NEW), -sl * dn.astype(F32), NEG)
        knb[...] = jnp.zeros_like(knb)
        vnb[...] = jnp.zeros_like(vnb)

    knb[:, 0:NEW] = kn_ref[0]
    vnb[:, 0:NEW] = vn_ref[0]

    qbd = _block_diag_rows(q_ref[0], w).astype(BF16)
    nt = (((1,), (1,)), ((), ()))
    s_c = jnp.dot(qbd, ck_ref[0].astype(BF16), preferred_element_type=F32) * SCALE
    s_n = jnp.dot(qbd, knb[...].astype(BF16), preferred_element_type=F32) * SCALE
    ms, dens, pcs, pns = [], [], [], []
    for bi in range(nbr):
        sc = s_c + bias_ref[bi]
        sn = s_n + bn_ref[bi]
        m = jnp.maximum(jnp.max(sc, axis=-1, keepdims=True), jnp.max(sn, axis=-1, keepdims=True))
        pc = jnp.exp(sc - m)
        pn = jnp.exp(sn - m)
        dens.append(jnp.sum(pc, axis=-1, keepdims=True) + jnp.sum(pn, axis=-1, keepdims=True))
        ms.append(m)
        pcs.append(pc.astype(BF16))
        pns.append(pn.astype(BF16))
    nums = (lax.dot_general(jnp.concatenate(pcs, axis=0), cv_ref[0].astype(BF16), nt, preferred_element_type=F32)
            + lax.dot_general(jnp.concatenate(pns, axis=0), vnb[...].astype(BF16), nt, preferred_element_type=F32))
    mx = functools.reduce(jnp.maximum, ms)
    den = 0.0
    num = 0.0
    for bi in range(nbr):
        wt = jnp.exp(ms[bi] - mx)
        den = den + dens[bi] * wt
        num = num + nums[bi * HQ:(bi + 1) * HQ, :] * wt
    out = _block_diag_rows_keep(num / den, w)
    o_ref[0] = sum(out[h * NEW:(h + 1) * NEW, :] for h in range(A_HEADS))

    lane = lax.broadcasted_iota(jnp.int32, (ROLL_ROWS, LANES), 1)
    tail = lane >= LANES - NEW
    for src, newb, dst in ((ck_ref, knb, nk_ref), (cv_ref, vnb, nv_ref)):
        for r0 in range(0, w, ROLL_ROWS):
            rows = slice(r0, r0 + ROLL_ROWS)
            rolled = pltpu.roll(src[0, rows, :], la - NEW, 1)
            dst[0, rows, 0:la - LANES] = rolled[:, 0:la - LANES]
            fresh = pltpu.roll(newb[rows, :], LANES - NEW, 1)
            dst[0, rows, la - LANES:la] = jnp.where(tail, fresh, rolled[:, la - LANES:la])


def _sample_at(q, kn_t, vn_t, ck_t, cv_t):
    s, w, la = ck_t.shape
    assert la >= max(wd for (wd, _) in A_PATTERNS) and la % LANES == 0
    seq = lambda i: (i, 0, 0)
    big = pl.BlockSpec((1, w, la), seq)
    new = pl.BlockSpec((1, w, NEW), seq)
    qspec = pl.BlockSpec((1, NEW, w), seq)
    nbr = len(A_PATTERNS)
    return pl.pallas_call(
        functools.partial(_sample_at_kernel, la=la, slopes=_alibi_slopes(A_HEADS)),
        grid=(s,),
        in_specs=[qspec, new, new, big, big],
        out_specs=[big, big, qspec],
        out_shape=[jax.ShapeDtypeStruct(ck_t.shape, F32), jax.ShapeDtypeStruct(cv_t.shape, F32),
                   jax.ShapeDtypeStruct(q.shape, F32)],
        scratch_shapes=[pltpu.VMEM((nbr, HQ, la), F32), pltpu.VMEM((nbr, HQ, LANES), F32),
                        pltpu.VMEM((w, LANES), F32), pltpu.VMEM((w, LANES), F32)],
        compiler_params=_cparams("arbitrary"),
        name="sample_attn_at",
    )(q, kn_t, vn_t, ck_t, cv_t)


def _sample_b_kernel(sink_ref, q_ref, kn_ref, vn_ref, ck_ref, cv_ref, nk_ref, nv_ref, o_ref,
                     bc_ref, bn_ref, sk_ref, knb, vnb, *, lb, nseq, slopes):
    @pl.when(pl.program_id(0) == 0)
    def _():
        r = lax.broadcasted_iota(jnp.int32, (HQ, lb), 0)
        c = lax.broadcasted_iota(jnp.int32, (HQ, lb), 1)
        dist = (r % NEW) + lb - c
        slope = jnp.zeros((HQ, lb), F32)
        sink = jnp.zeros((HQ, LANES), F32)
        rs = lax.broadcasted_iota(jnp.int32, (HQ, LANES), 0)
        for h in range(B_HEADS):
            slope = jnp.where(r // NEW == h, slopes[h], slope)
            sink = jnp.where(rs // NEW == h, sink_ref[h], sink)
        bc_ref[...] = jnp.where(dist <= BAND, -slope * dist.astype(F32), NEG)
        cn = lax.broadcasted_iota(jnp.int32, (HQ, LANES), 1)
        dn = (rs % NEW) - cn
        sl = jnp.zeros((HQ, LANES), F32)
        for h in range(B_HEADS):
            sl = jnp.where(rs // NEW == h, slopes[h], sl)
        bn_ref[...] = jnp.where((dn >= 0) & (cn < NEW), -sl * dn.astype(F32), NEG)
        sk_ref[...] = sink
        knb[...] = jnp.zeros_like(knb)
        vnb[...] = jnp.zeros_like(vnb)

    nt = (((1,), (1,)), ((), ()))
    rr = lax.broadcasted_iota(jnp.int32, (HQ, LANES), 0)
    cc = lax.broadcasted_iota(jnp.int32, (HQ, LANES), 1)
    own = ((rr // NEW) % B_KV_HEADS) == (cc // HEAD_DIM)
    sink = sk_ref[:, 0:1]
    for t in range(nseq):
        kn = kn_ref[t]
        vn = vn_ref[t]
        nk_ref[t, 0:lb - NEW, :] = ck_ref[t, NEW:lb, :]
        nk_ref[t, lb - NEW:lb, :] = kn
        nv_ref[t, 0:lb - NEW, :] = cv_ref[t, NEW:lb, :]
        nv_ref[t, lb - NEW:lb, :] = vn
        knb[0:NEW, :] = kn
        vnb[0:NEW, :] = vn
        q = q_ref[t]
        rows = []
        for g in range(B_GROUP):
            for kv in range(B_KV_HEADS):
                rows.append(q[:, g * LANES:(g + 1) * LANES])
        qbd = jnp.where(own, jnp.concatenate(rows, axis=0), 0.0).astype(BF16)
        sc = lax.dot_general(qbd, ck_ref[t].astype(BF16), nt, preferred_element_type=F32) * SCALE + bc_ref[...]
        sn = lax.dot_general(qbd, knb[...].astype(BF16), nt, preferred_element_type=F32) * SCALE + bn_ref[...]
        m = jnp.maximum(jnp.maximum(jnp.max(sc, axis=-1, keepdims=True), jnp.max(sn, axis=-1, keepdims=True)), sink)
        pc = jnp.exp(sc - m)
        pn = jnp.exp(sn - m)
        den = jnp.sum(pc, axis=-1, keepdims=True) + jnp.sum(pn, axis=-1, keepdims=True) + jnp.exp(sink - m)
        o = (jnp.dot(pc.astype(BF16), cv_ref[t].astype(BF16), preferred_element_type=F32)
             + jnp.dot(pn.astype(BF16), vnb[...].astype(BF16), preferred_element_type=F32)) / den
        o = jnp.where(own, o, 0.0)
        for g in range(B_GROUP):
            base = g * B_KV_HEADS * NEW
            o_ref[t, :, g * LANES:(g + 1) * LANES] = o[base:base + NEW, :] + o[base + NEW:base + 2 * NEW, :]


def _sample_b(sinks_perm, slopes_perm, q, kn, vn, ck, cv, nseq=8):
    s, lb, kw = ck.shape
    assert lb == BAND and kw == LANES and s % nseq == 0
    seq = lambda i: (i, 0, 0)
    return pl.pallas_call(
        functools.partial(_sample_b_kernel, lb=lb, nseq=nseq, slopes=slopes_perm),
        grid=(s // nseq,),
        in_specs=[pl.BlockSpec(memory_space=pltpu.SMEM), pl.BlockSpec((nseq, NEW, A_WIDTH), seq),
                  pl.BlockSpec((nseq, NEW, kw), seq), pl.BlockSpec((nseq, NEW, kw), seq),
                  pl.BlockSpec((nseq, lb, kw), seq), pl.BlockSpec((nseq, lb, kw), seq)],
        out_specs=[pl.BlockSpec((nseq, lb, kw), seq), pl.BlockSpec((nseq, lb, kw), seq),
                   pl.BlockSpec((nseq, NEW, A_WIDTH), seq)],
        out_shape=[jax.ShapeDtypeStruct(ck.shape, F32), jax.ShapeDtypeStruct(cv.shape, F32),
                   jax.ShapeDtypeStruct(q.shape, F32)],
        scratch_shapes=[pltpu.VMEM((HQ, lb), F32), pltpu.VMEM((HQ, LANES), F32), pltpu.VMEM((HQ, LANES), F32),
                        pltpu.VMEM((LANES, kw), F32), pltpu.VMEM((LANES, kw), F32)],
        compiler_params=_cparams("arbitrary"),
        name="sample_attn_b",
    )(sinks_perm, q, kn, vn, ck, cv)


def _outproj_kernel(x_ref, oa_ref, ob_ref, wa_ref, wb_ref, ht_ref):
    oa = jnp.concatenate([oa_ref[s] for s in range(NSLAB)], axis=1).astype(BF16)
    ob = jnp.concatenate([ob_ref[s] for s in range(NSLAB)], axis=1).astype(BF16)
    h = (x_ref[...] + jnp.dot(oa, wa_ref[...], preferred_element_type=F32)
         + jnp.dot(ob, wb_ref[...], preferred_element_type=F32))
    ht_ref[...] = h.T


def _outproj(x2d, oa4, ob4, wa, wb, tm):
    n, d = x2d.shape
    w = A_WIDTH
    row = lambda i: (i, 0)
    const = lambda i: (0, 0)
    slab = pl.BlockSpec((NSLAB, tm, LANES), lambda i: (0, i, 0))
    return pl.pallas_call(
        _outproj_kernel,
        grid=(n // tm,),
        in_specs=[pl.BlockSpec((tm, d), row), slab, slab, pl.BlockSpec((w, d), const), pl.BlockSpec((w, d), const)],
        out_specs=pl.BlockSpec((d, tm), lambda i: (0, i)),
        out_shape=jax.ShapeDtypeStruct((d, n), F32),
        compiler_params=_cparams("parallel"),
        name="out_proj",
    )(x2d, oa4, ob4, wa, wb)


BIG = 3.0e38


SUBLANES = 8


def _merge_exchange(n):
    pairs = []
    t = max(1, math.ceil(math.log2(n)))
    p = 1 << (t - 1)
    while p > 0:
        q, r, d = 1 << (t - 1), 0, p
        while d > 0:
            pairs.extend((i, i + d) for i in range(n - d) if (i & p) == r)
            d, q, r = q - p, q >> 1, p
        p >>= 1
    return pairs


def _vmax(a, b):
    if a is None:
        return b
    if b is None:
        return a
    return jnp.maximum(a, b)


def _vmin(a, b):
    if a is None or b is None:
        return None
    return jnp.minimum(a, b)


def _exchange(x, i, j):
    x[i], x[j] = _vmax(x[i], x[j]), _vmin(x[i], x[j])


def _top16(tiles):
    x = list(tiles) + [None] * (PEER_TOPK - len(tiles))
    for (i, j) in _merge_exchange(len(tiles)):
        _exchange(x, i, j)
    for shift in (4, 2, 1):
        y = [None if v is None else pltpu.roll(v, shift, 0) for v in x]
        x = [_vmax(x[k], y[PEER_TOPK - 1 - k]) for k in range(PEER_TOPK)]
        for d in (8, 4, 2, 1):
            for i in range(PEER_TOPK):
                if not i & d:
                    _exchange(x, i, i + d)
    return x


def _rows_sum(x):
    for shift in (4, 2, 1):
        x = x + pltpu.roll(x, shift, 0)
    return x


def _peer_select_kernel(ht_ref, g_ref, wq_ref, keys_ref,
                        hn_ref, r1_ref, w1_ref, cnt_ref, w0_ref, s_ref, *, nk, tn):
    h = ht_ref[...]
    ms = jnp.mean(h * h, axis=0, keepdims=True)
    hn = (h * lax.rsqrt(ms + NORM_EPS) * g_ref[...]).astype(BF16)
    hn_ref[...] = hn
    q = jnp.dot(wq_ref[...], hn, preferred_element_type=F32)
    half = q.shape[0] // (2 * PEER_HEADS)
    for k in range(2 * PEER_HEADS):
        qk = q[k * half:(k + 1) * half, :].astype(BF16)
        s_ref[k] = jnp.dot(keys_ref[k], qk, preferred_element_type=F32)
    nchunk = tn // LANES
    ntile = nk // SUBLANES
    sub = lax.broadcasted_iota(jnp.int32, (SUBLANES, LANES), 0)

    def pack(vals):
        out = vals[-1]
        for r in range(len(vals) - 2, -1, -1):
            out = jnp.where(sub == r, vals[r], out)
        return out

    def head_chunk(t, carry):
        hd = t // nchunk
        lanes = pl.ds(pl.multiple_of((t % nchunk) * LANES, LANES), LANES)
        rows = [slice(k * SUBLANES, (k + 1) * SUBLANES) for k in range(ntile)]
        s0 = [s_ref[2 * hd, r, lanes] for r in rows]
        s1 = [s_ref[2 * hd + 1, r, lanes] for r in rows]
        v0 = _top16(s0)
        v1 = _top16(s1)
        v1lo, v1hi, v0hi = pack(v1[0:8]), pack(v1[8:16]), pack(v0[8:16])
        cands = [v0[0] + v1lo, v0[0] + v1hi] + [v0[a] + v1lo for a in range(1, 8)] + [v0hi + v1[0]]
        best = _top16(cands)
        top, tau = best[0], best[PEER_TOPK - 1]
        z = _rows_sum(sum(jnp.where(c >= tau, jnp.exp(c - top), 0.0) for c in cands))
        inv_z = 1.0 / z
        for m in range(ntile // 2):
            cnt, rk1 = [], []
            for k in (2 * m, 2 * m + 1):
                c = jnp.zeros((SUBLANES, LANES), F32)
                r = jnp.zeros((SUBLANES, LANES), F32)
                for b in range(PEER_TOPK):
                    c = c + jnp.where(s0[k] + v1[b] >= tau, 1.0, 0.0)
                    r = r + jnp.where(v1[b] > s1[k], 1.0, 0.0)
                cnt.append(c)
                rk1.append(r)
            pair = slice(2 * m * SUBLANES, (2 * m + 2) * SUBLANES)
            both = lambda f: jnp.concatenate([f(2 * m), f(2 * m + 1)], axis=0)
            cnt_ref[hd, pair, lanes] = jnp.concatenate(cnt, axis=0)
            w0_ref[hd, pair, lanes] = both(lambda k: jnp.exp(s0[k] - v0[0]) * inv_z)
            r1_ref[hd, pair, lanes] = jnp.concatenate(rk1, axis=0).astype(BF16)
            w1_ref[hd, pair, lanes] = both(lambda k: jnp.exp(s1[k] - v1[0])).astype(BF16)
        return carry

    lax.fori_loop(0, PEER_HEADS * nchunk, head_chunk, 0)


def _peer_select(ht, g_col, wq_t, keys, tn):
    d, n = ht.shape
    nslab, nk, half = keys.shape
    tok = lambda i: (0, i)
    tok3 = lambda i: (0, 0, i)
    stat = lambda dt: jax.ShapeDtypeStruct((PEER_HEADS, nk, n), dt)
    return pl.pallas_call(
        functools.partial(_peer_select_kernel, nk=nk, tn=tn),
        grid=(n // tn,),
        in_specs=[pl.BlockSpec((d, tn), tok), pl.BlockSpec((d, 1), lambda i: (0, 0)),
                  pl.BlockSpec(wq_t.shape, lambda i: (0, 0)), pl.BlockSpec(keys.shape, lambda i: (0, 0, 0))],
        out_specs=[pl.BlockSpec((d, tn), tok)] + [pl.BlockSpec((PEER_HEADS, nk, tn), tok3)] * 4,
        out_shape=[jax.ShapeDtypeStruct((d, n), BF16), stat(BF16), stat(BF16), stat(F32), stat(F32)],
        scratch_shapes=[pltpu.VMEM((nslab, nk, tn), F32)],
        compiler_params=_cparams("parallel"),
        name="peer_select",
    )(ht, g_col, wq_t, keys)


def _gelu(x):
    return 0.5 * x * (1.0 + lax.erf(x * (2.0 ** -0.5)))


def _peer_dense_kernel(hn_ref, u_ref, vt_ref, r1_ref, w1_ref, cnt_ref, w0_ref, ht_ref, y_ref,
                       acc_ref, a0_ref, a1_ref, c0_ref, c1_ref, *, nk, te, tn, lc, rb, ne, nsteps):
    s = pl.program_id(0)

    @pl.when(s == 0)
    def _():
        for r in (a0_ref, a1_ref, c0_ref, c1_ref, acc_ref):
            r[...] = jnp.zeros_like(r)

    e = jnp.clip(s - 1, 0, nsteps - 1) % ne
    p2 = s - 2
    first = (jnp.maximum(p2, 0) % ne) == 0
    nslab = te // nk

    def stages(a_new, a_prev, c_new, c_prev):
        a_new[...] = jnp.dot(u_ref[...], hn_ref[...], preferred_element_type=F32)
        for ii in range(nslab):
            i = e * nslab + ii
            cnt_rows = [cnt_ref[hd, pl.ds(i, 1), :].astype(BF16) for hd in range(PEER_HEADS)]
            w0_rows = [w0_ref[hd, pl.ds(i, 1), :].astype(BF16) for hd in range(PEER_HEADS)]
            for c in range(tn // lc):
                ls = slice(c * lc, (c + 1) * lc)
                for jb in range(nk // rb):
                    js = slice(jb * rb, (jb + 1) * rb)
                    rs = slice(ii * nk + jb * rb, ii * nk + (jb + 1) * rb)
                    g = None
                    for hd in range(PEER_HEADS):
                        term = jnp.where(r1_ref[hd, js, ls] < cnt_rows[hd][:, ls], w1_ref[hd, js, ls],
                                         jnp.zeros((), BF16)) * w0_rows[hd][:, ls]
                        g = term if g is None else g + term
                    c_new[rs, ls] = g * _gelu(a_prev[rs, ls]).astype(BF16)
        contrib = jnp.dot(vt_ref[...], c_prev[...], preferred_element_type=F32)
        acc_ref[...] = jnp.where(first, contrib, acc_ref[...] + contrib)

    @pl.when(s % 2 == 0)
    def _():
        stages(a0_ref, a1_ref, c1_ref, c0_ref)

    @pl.when(s % 2 == 1)
    def _():
        stages(a1_ref, a0_ref, c0_ref, c1_ref)

    @pl.when((p2 >= 0) & (p2 % ne == ne - 1))
    def _():
        y_ref[...] = (ht_ref[...] + acc_ref[...]).T


def _peer_dense(hn, u, vt, r1, w1, cnt, w0, ht, *, tn, te, lc, rb):
    d, n = hn.shape
    ne = u.shape[0] // te
    nk = r1.shape[1]
    nsteps = (n // tn) * ne
    last = nsteps - 1
    pair = lambda s, lag: jnp.clip(s - lag, 0, last)
    stat = pl.BlockSpec((PEER_HEADS, nk, tn), lambda s: (0, 0, pair(s, 1) // ne))
    return pl.pallas_call(
        functools.partial(_peer_dense_kernel, nk=nk, te=te, tn=tn, lc=lc, rb=min(rb, nk), ne=ne, nsteps=nsteps),
        grid=(nsteps + 2,),
        in_specs=[pl.BlockSpec((d, tn), lambda s: (0, pair(s, 0) // ne)),
                  pl.BlockSpec((te, d), lambda s: (pair(s, 0) % ne, 0)),
                  pl.BlockSpec((d, te), lambda s: (0, pair(s, 2) % ne)),
                  stat, stat, stat, stat,
                  pl.BlockSpec((d, tn), lambda s: (0, pair(s, 2) // ne))],
        out_specs=pl.BlockSpec((tn, d), lambda s: (pair(s, 2) // ne, 0)),
        out_shape=jax.ShapeDtypeStruct((n, d), F32),
        scratch_shapes=[pltpu.VMEM((d, tn), F32), pltpu.VMEM((te, tn), F32), pltpu.VMEM((te, tn), F32),
                        pltpu.VMEM((te, tn), BF16), pltpu.VMEM((te, tn), BF16)],
        compiler_params=_cparams("arbitrary", interleave=True),
        name="peer_dense",
    )(hn, u, vt, r1, w1, cnt, w0, ht)


def _peer(ht, g_col, wq_t, keys, u, vt, *, tn_sel, tn, te, lc, rb):
    hn, r1, w1, cnt, w0 = _peer_select(ht, g_col, wq_t, keys, tn_sel)
    return _peer_dense(hn, u, vt, r1, w1, cnt, w0, ht, tn=tn, te=te, lc=lc, rb=rb)


def kernel(x_prompt, x_sample, cache_a_k, cache_a_v, cache_b_k, cache_b_v, norm_attn, w_in, g_qa, g_ka, g_qb, g_kb, sinks, w_o, norm_ffn, peer_wq, peer_keys, peer_u, peer_v):
    b, l, d = x_prompt.shape
    s, ns, _ = x_sample.shape
    assert ns == NEW and w_in.shape[0] == 1
    la, lb = cache_a_k.shape[2], cache_b_k.shape[2]
    w = A_WIDTH
    perm = _qb_perm()

    wl = w_in[0]
    w_all = jnp.concatenate([wl[:, :3 * w], wl[:, 3 * w:4 * w][:, perm], wl[:, 4 * w:]], axis=1).astype(BF16)
    seg = (jnp.arange(w)[:, None] // HEAD_DIM == jnp.arange(w)[None, :] // HEAD_DIM).astype(BF16)
    t8 = lambda g: jnp.tile(g, A_HEADS)[None, :]
    gains = (t8(g_qa[0]), t8(g_ka[0]), t8(g_qb[0]), jnp.tile(g_kb[0], B_KV_HEADS)[None, :])
    sb = _alibi_slopes(B_HEADS)
    slopes_perm = [sb[(h % 2) * B_GROUP + h // 2] for h in range(B_HEADS)]
    sinks_perm = jnp.stack([sinks[0, (h % 2) * B_GROUP + h // 2] for h in range(B_HEADS)])
    wo_a = w_o[0, :w, :].astype(BF16)
    wo_b = w_o[0, w:, :][perm, :].astype(BF16)
    g_col = norm_ffn[0][:, None]
    wq_t = peer_wq[0].T.astype(BF16)
    nk = peer_keys.shape[3]
    keys = peer_keys[0].reshape(2 * PEER_HEADS, nk, peer_keys.shape[4]).astype(BF16)
    u = peer_u[0].astype(BF16)
    vt = peer_v[0].T.astype(BF16)

    xp = x_prompt.reshape(b * l, d)
    qa4, ka4, va4, qb4, ka, va, kb, vb = _project(xp, norm_attn, w_all, seg, *gains, 512)
    seq4 = lambda t: t.reshape(t.shape[0], b, l, LANES)
    oa4 = _mixer(seq4(qa4), seq4(ka4), seq4(va4), dils=[dl for (_, dl) in A_PATTERNS],
                 slopes=_alibi_slopes(A_HEADS))
    ob4 = _mixer(seq4(qb4), kb.reshape(1, b, l, LANES), vb.reshape(1, b, l, LANES), dils=[1],
                 slopes=slopes_perm, sinks=sinks_perm)
    ht_p = _outproj(xp, oa4.reshape(NSLAB, b * l, LANES), ob4.reshape(NSLAB, b * l, LANES), wo_a, wo_b, 512)
    y_p = _peer(ht_p, g_col, wq_t, keys, u, vt, tn_sel=512, tn=512, te=1024, lc=256, rb=64)
    na = min(la, l)
    nb = min(lb, l)
    pak = ka.reshape(b, l, A_HEADS, HEAD_DIM)[None, :, l - na:]
    pav = va.reshape(b, l, A_HEADS, HEAD_DIM)[None, :, l - na:]
    pbk = kb.reshape(b, l, B_KV_HEADS, HEAD_DIM)[None, :, l - nb:]
    pbv = vb.reshape(b, l, B_KV_HEADS, HEAD_DIM)[None, :, l - nb:]

    xs = x_sample.reshape(s * ns, d)
    qa4, _, _, qb4, ka, va, kb, vb = _project(xs, norm_attn, w_all, seg, *gains, 512)
    r3 = lambda t: t.reshape(s, ns, t.shape[-1])
    wide = lambda t4: jnp.transpose(t4, (1, 0, 2)).reshape(s, ns, w)
    slabs = lambda t: jnp.transpose(t.reshape(s * ns, NSLAB, LANES), (1, 0, 2))
    to_t = lambda c: jnp.transpose(c[0], (0, 2, 3, 1)).reshape(s, w, la)
    from_t = lambda t: jnp.transpose(t.reshape(s, A_HEADS, HEAD_DIM, la), (0, 3, 1, 2))[None]
    new_t = lambda t: jnp.transpose(r3(t), (0, 2, 1))
    sak_t, sav_t, oa = _sample_at(wide(qa4), new_t(ka), new_t(va), to_t(cache_a_k), to_t(cache_a_v))
    sak, sav = from_t(sak_t), from_t(sav_t)
    sbk, sbv, ob = _sample_b(sinks_perm, slopes_perm, wide(qb4), r3(kb), r3(vb),
                             cache_b_k[0].reshape(s, lb, LANES), cache_b_v[0].reshape(s, lb, LANES))
    ht_s = _outproj(xs, slabs(oa), slabs(ob), wo_a, wo_b, 512)
    y_s = _peer(ht_s, g_col, wq_t, keys, u, vt, tn_sel=512, tn=512, te=1024, lc=256, rb=64)

    return (y_p.reshape(b, l, d), y_s.reshape(s, ns, d), pak, pav, pbk, pbv,
            sak, sav,
            sbk.reshape(1, s, lb, B_KV_HEADS, HEAD_DIM), sbv.reshape(1, s, lb, B_KV_HEADS, HEAD_DIM))
```

```python
import functools
import math

import jax
import jax.numpy as jnp
from jax import lax
from jax.experimental import pallas as pl
from jax.experimental.pallas import tpu as pltpu

HEAD_DIM = 64
A_HEADS = 8
B_HEADS = 8
B_KV_HEADS = 2
B_GROUP = B_HEADS // B_KV_HEADS
A_PATTERNS = ((128, 1), (512, 4), (2048, 16))
BAND = 128
A_WIDTH = A_HEADS * HEAD_DIM
PEER_HEADS = 8
PEER_TOPK = 16
NORM_EPS = 1e-6
NEG = -1e30
SCALE = HEAD_DIM ** -0.5
LANES = 128
VMEM_LIMIT_BYTES = 56 * 1024 * 1024

BF16 = jnp.bfloat16
F32 = jnp.float32


def _alibi_slopes(n):
    return [2.0 ** (-8.0 * (i + 1) / n) for i in range(n)]


def _cparams(*sem, interleave=False):
    del interleave
    return pltpu.CompilerParams(dimension_semantics=sem, vmem_limit_bytes=VMEM_LIMIT_BYTES)


def _head_rms(h, seg, gain):
    sq = h * h
    hi = sq.astype(BF16)
    lo = (sq - hi.astype(F32)).astype(BF16)
    ms = (jnp.dot(hi, seg, preferred_element_type=F32)
          + jnp.dot(lo, seg, preferred_element_type=F32)) * (1.0 / HEAD_DIM)
    return h * lax.rsqrt(ms + NORM_EPS) * gain


NSLAB = A_WIDTH // LANES


def _proj_kernel(x_ref, g_ref, w_ref, seg_ref, gqa_ref, gka_ref, gqb_ref, gkb_ref,
                 qa4_ref, ka4_ref, va4_ref, qb4_ref, kat_ref, vat_ref, kb_ref, vb_ref, kbt_ref, vbt_ref):
    x = x_ref[...]
    ms = jnp.mean(x * x, axis=-1, keepdims=True)
    xn = (x * lax.rsqrt(ms + NORM_EPS) * g_ref[...]).astype(BF16)
    h = jnp.dot(xn, w_ref[...], preferred_element_type=F32)
    seg = seg_ref[...]
    w = A_WIDTH
    qa = _head_rms(h[:, 0:w], seg, gqa_ref[...])
    ka = _head_rms(h[:, w:2 * w], seg, gka_ref[...])
    va = h[:, 2 * w:3 * w]
    qb = _head_rms(h[:, 3 * w:4 * w], seg, gqb_ref[...])
    kb = _head_rms(h[:, 4 * w:4 * w + LANES], seg[0:LANES, 0:LANES], gkb_ref[...])
    vb = h[:, 4 * w + LANES:4 * w + 2 * LANES]
    kb_ref[...] = kb
    vb_ref[...] = vb
    kat_ref[...] = ka.T
    vat_ref[...] = va.T
    kbt_ref[...] = kb.T
    vbt_ref[...] = vb.T
    for s in range(NSLAB):
        sl = slice(s * LANES, (s + 1) * LANES)
        qa4_ref[s] = qa[:, sl]
        ka4_ref[s] = ka[:, sl]
        va4_ref[s] = va[:, sl]
        qb4_ref[s] = qb[:, sl]


def _project(x2d, norm_g, w_bf16, seg, gqa, gka, gqb, gkb, tm, nseq):
    n, d = x2d.shape
    nc = w_bf16.shape[1]
    w = A_WIDTH
    l = n // nseq
    tps = l // tm
    row = lambda i: (i, 0)
    col = lambda i: (i // tps, 0, i % tps)
    slab = lambda i: (0, i, 0)
    const = lambda i: (0, 0)
    outs = ([jax.ShapeDtypeStruct((NSLAB, n, LANES), F32)] * 4 + [jax.ShapeDtypeStruct((nseq, w, l), F32)] * 2
            + [jax.ShapeDtypeStruct((n, LANES), F32)] * 2 + [jax.ShapeDtypeStruct((nseq, LANES, l), F32)] * 2)
    return pl.pallas_call(
        _proj_kernel,
        grid=(n // tm,),
        in_specs=[pl.BlockSpec((tm, d), row), pl.BlockSpec((1, d), const), pl.BlockSpec((d, nc), const),
                  pl.BlockSpec((w, w), const), pl.BlockSpec((1, w), const), pl.BlockSpec((1, w), const),
                  pl.BlockSpec((1, w), const), pl.BlockSpec((1, LANES), const)],
        out_specs=([pl.BlockSpec((NSLAB, tm, LANES), slab)] * 4 + [pl.BlockSpec((None, w, tm), col)] * 2
                   + [pl.BlockSpec((tm, LANES), row)] * 2 + [pl.BlockSpec((None, LANES, tm), col)] * 2),
        out_shape=outs,
        compiler_params=_cparams("parallel"),
        name="qkv_proj",
    )(x2d, norm_g, w_bf16, seg, gqa, gka, gqb, gkb)


def _band_kernel(*refs, tq, dil, slopes, with_sink, with_lse):
    if with_sink:
        sink_ref, refs = refs[0], refs[1:]
    q_ref, kc_ref, kp_ref, vc_ref, vp_ref = refs[:5]
    o_ref = refs[5]
    lse_ref = refs[6] if with_lse else None
    kbuf, vbuf, bias_ref = refs[-3:]
    i = pl.program_id(2)
    nsub = tq // BAND

    @pl.when((pl.program_id(0) == 0) & (pl.program_id(1) == 0) & (i == 0))
    def _():
        r = lax.broadcasted_iota(jnp.int32, (BAND, 2 * BAND), 0)
        c = lax.broadcasted_iota(jnp.int32, (BAND, 2 * BAND), 1)
        dist = r + BAND - c
        valid = (dist >= 0) & (dist <= BAND)
        distf = (dist * dil).astype(F32)
        for h in range(A_HEADS):
            bias_ref[h] = jnp.where(valid, -slopes[h] * distf, NEG)

    kbuf[0:BAND, :] = kp_ref[0].astype(BF16)
    kbuf[BAND:BAND + tq, :] = kc_ref[0].astype(BF16)
    vbuf[0:BAND, :] = vp_ref[0].astype(BF16)
    vbuf[BAND:BAND + tq, :] = vc_ref[0].astype(BF16)

    lane = lax.broadcasted_iota(jnp.int32, (BAND, LANES), 1)
    col = lax.broadcasted_iota(jnp.int32, (1, 2 * BAND), 1)
    prev_cols = (col < BAND).astype(F32)

    def body(j, carry):
        r0 = pl.multiple_of(j * BAND, BAND)
        pen = jnp.where((i * nsub + j) == 0, NEG, 0.0) * prev_cols
        for hp in range(A_HEADS // 2):
            sl = slice(hp * LANES, (hp + 1) * LANES)
            qs = q_ref[0, pl.ds(r0, BAND), sl]
            kw = kbuf[pl.ds(r0, 2 * BAND), sl]
            vw = vbuf[pl.ds(r0, 2 * BAND), sl]
            outs, lses = [], []
            for e in range(2):
                h = 2 * hp + e
                qm = jnp.where((lane >= HEAD_DIM) == bool(e), qs, 0.0).astype(BF16)
                s = lax.dot_general(qm, kw, (((1,), (1,)), ((), ())), preferred_element_type=F32)
                s = s * SCALE + bias_ref[h] + pen
                m = jnp.max(s, axis=-1, keepdims=True)
                if with_sink:
                    m = jnp.maximum(m, sink_ref[h])
                p = jnp.exp(s - m)
                den = jnp.sum(p, axis=-1, keepdims=True)
                if with_sink:
                    den = den + jnp.exp(sink_ref[h] - m)
                o = jnp.dot(p.astype(BF16), vw, preferred_element_type=F32)
                outs.append(o / den)
                lses.append(m + jnp.log(den))
            hi = lane >= HEAD_DIM
            o_ref[0, pl.ds(r0, BAND), sl] = jnp.where(hi, outs[1], outs[0])
            if with_lse:
                lse_ref[0, pl.ds(r0, BAND), sl] = jnp.where(hi, lses[1], lses[0])
        return carry

    lax.fori_loop(0, nsub, body, 0)


def _band_attention(q, k, v, *, dil, tq, slopes, sinks=None, with_lse=True):
    b, r, cw = q.shape
    w = A_WIDTH
    c = cw // w
    assert r % tq == 0 and tq % BAND == 0
    nsub = tq // BAND
    cur = lambda bi, ci, ii: (bi, ii, ci)
    prev = lambda bi, ci, ii: (bi, jnp.maximum(ii * nsub - 1, 0), ci)
    in_specs = [pl.BlockSpec((1, tq, w), cur), pl.BlockSpec((1, tq, w), cur), pl.BlockSpec((1, BAND, w), prev),
                pl.BlockSpec((1, tq, w), cur), pl.BlockSpec((1, BAND, w), prev)]
    args = [q, k, k, v, v]
    if sinks is not None:
        in_specs = [pl.BlockSpec(memory_space=pltpu.SMEM)] + in_specs
        args = [sinks] + args
    n_out = 2 if with_lse else 1
    out = pl.pallas_call(
        functools.partial(_band_kernel, tq=tq, dil=dil, slopes=slopes, with_sink=sinks is not None,
                          with_lse=with_lse),
        grid=(b, c, r // tq),
        in_specs=in_specs,
        out_specs=[pl.BlockSpec((1, tq, w), cur)] * n_out,
        out_shape=[jax.ShapeDtypeStruct(q.shape, F32)] * n_out,
        scratch_shapes=[pltpu.VMEM((BAND + tq, w), BF16), pltpu.VMEM((BAND + tq, w), BF16),
                        pltpu.VMEM((A_HEADS, BAND, 2 * BAND), F32)],
        compiler_params=_cparams("arbitrary", "arbitrary", "arbitrary"),
        name=f"band_attn_d{dil}",
    )(*args)
    return out if with_lse else out[0]


def _prompt_mixers(qa, ka, va, qb, kbx, vbx, sinks_perm, b, l):
    w = A_WIDTH
    slopes_a = _alibi_slopes(A_HEADS)
    sb = _alibi_slopes(B_HEADS)
    slopes_b = [sb[(h % 2) * B_GROUP + h // 2] for h in range(B_HEADS)]
    branches = []
    for (_, dil) in A_PATTERNS:
        r = l // dil
        tq = min(r, 1024)
        view = lambda t: t.reshape(b, r, dil * w)
        o, lse = _band_attention(view(qa), view(ka), view(va), dil=dil, tq=tq, slopes=slopes_a)
        branches.append((o.reshape(b * l, w), lse.reshape(b * l, w)))
    view = lambda t: t.reshape(b, l, w)
    ob = _band_attention(view(qb), view(kbx), view(vbx), dil=1, tq=min(l, 1024), slopes=slopes_b,
                         sinks=sinks_perm, with_lse=False)
    return branches, ob.reshape(b * l, w)


MIXER_UNROLL = 4


def _pick(idx, values):
    out = jnp.float32(values[-1])
    for i in range(len(values) - 2, -1, -1):
        out = jnp.where(idx == i, jnp.float32(values[i]), out)
    return out


def _mixer_kernel(*refs, dils, slopes, with_sink, seq):
    if with_sink:
        sink_ref, refs = refs[0], refs[1:]
    q_ref, k_ref, v_ref, o_ref, kpad, vpad, bias_ref = refs[:7]
    ob_ref, lse_ref = refs[7:9] if len(dils) > 1 else (None, None)
    hp = pl.program_id(1)
    pad = BAND * max(dils)

    @pl.when((pl.program_id(0) == 0) & (hp == 0))
    def _():
        kpad[0:pad, :] = jnp.zeros((pad, LANES), F32)
        vpad[0:pad, :] = jnp.zeros((pad, LANES), F32)

    kpad[pad:pad + seq, :] = k_ref[0, 0]
    vpad[pad:pad + seq, :] = v_ref[0, 0]

    r = lax.broadcasted_iota(jnp.int32, (BAND, 2 * BAND), 0)
    c = lax.broadcasted_iota(jnp.int32, (BAND, 2 * BAND), 1)
    dist = r + BAND - c
    valid = (dist >= 0) & (dist <= BAND)
    for bi, dil in enumerate(dils):
        distf = (dist * dil).astype(F32)
        for e in range(2):
            slope = _pick(hp, [slopes[2 * g + e] for g in range(NSLAB)])
            bias_ref[bi, e] = jnp.where(valid, -slope * distf, NEG)

    lane = lax.broadcasted_iota(jnp.int32, (BAND, LANES), 1)
    hi = lane >= HEAD_DIM
    col = lax.broadcasted_iota(jnp.int32, (1, 2 * BAND), 1)
    prev_cols = (col < BAND).astype(F32)
    nt = (((1,), (1,)), ((), ()))

    for bi, dil in enumerate(dils):
        nblk = seq // (dil * BAND)

        def block(t, bi=bi, dil=dil, nblk=nblk):
            res = t // nblk
            j = t % nblk
            base = res + dil * BAND * j
            if dil == 1:
                base = pl.multiple_of(base, BAND)
                rows = pl.ds(base, BAND)
                win = pl.ds(pl.multiple_of(pad + base - BAND, BAND), 2 * BAND)
            else:
                rows = pl.ds(base, BAND, stride=dil)
                win = pl.ds(pad + base - dil * BAND, 2 * BAND, stride=dil)
            qs = q_ref[0, 0, rows, :]
            kw = kpad[win, :].astype(BF16)
            vw = vpad[win, :].astype(BF16)
            pen = jnp.where(j == 0, NEG, 0.0) * prev_cols
            outs, lses = [], []
            for e in range(2):
                qm = jnp.where(hi == bool(e), qs, 0.0).astype(BF16)
                s = lax.dot_general(qm, kw, nt, preferred_element_type=F32)
                s = s * SCALE + bias_ref[bi, e] + pen
                m = jnp.max(s, axis=-1, keepdims=True)
                if with_sink:
                    sink = sink_ref[2 * hp + e]
                    m = jnp.maximum(m, sink)
                p = jnp.exp(s - m)
                den = jnp.sum(p, axis=-1, keepdims=True)
                if with_sink:
                    den = den + jnp.exp(sink - m)
                o = jnp.dot(p.astype(BF16), vw, preferred_element_type=F32)
                outs.append(o / den)
                lses.append(m + jnp.log(den))
            if len(dils) == 1:
                o_ref[0, 0, rows, :] = jnp.where(hi, outs[1], outs[0])
            else:
                ob_ref[bi, rows, :] = jnp.where(hi, outs[1], outs[0])
                lse_ref[bi, rows, :] = jnp.where(hi, lses[1], lses[0])

        def blocks(t, carry, block=block):
            for uu in range(MIXER_UNROLL):
                block(t * MIXER_UNROLL + uu)
            return carry

        assert (dil * nblk) % MIXER_UNROLL == 0
        lax.fori_loop(0, dil * nblk // MIXER_UNROLL, blocks, 0)

    if len(dils) > 1:
        def merge(t, carry):
            rows = pl.ds(pl.multiple_of(t * BAND, BAND), BAND)
            ls = [lse_ref[bi, rows, :] for bi in range(len(dils))]
            mx = functools.reduce(jnp.maximum, ls)
            ws = [jnp.exp(l - mx) for l in ls]
            num = sum(w * ob_ref[bi, rows, :] for bi, w in enumerate(ws))
            o_ref[0, 0, rows, :] = num / sum(ws)
            return carry

        lax.fori_loop(0, seq // BAND, merge, 0)


def _mixer(q4, k4, v4, *, dils, slopes, sinks=None):
    nslab, b, l, _ = q4.shape
    assert l % (BAND * max(dils)) == 0
    shared = k4.shape[0] == 1
    qmap = lambda bi, hp: (hp, bi, 0, 0)
    kmap = (lambda bi, hp: (0, bi, 0, 0)) if shared else qmap
    blk = (1, 1, l, LANES)
    in_specs = [pl.BlockSpec(blk, qmap), pl.BlockSpec(blk, kmap), pl.BlockSpec(blk, kmap)]
    args = [q4, k4, v4]
    if sinks is not None:
        in_specs = [pl.BlockSpec(memory_space=pltpu.SMEM)] + in_specs
        args = [sinks] + args
    pad = BAND * max(dils)
    scratch = [pltpu.VMEM((pad + l, LANES), F32), pltpu.VMEM((pad + l, LANES), F32),
               pltpu.VMEM((len(dils), 2, BAND, 2 * BAND), F32)]
    if len(dils) > 1:
        scratch += [pltpu.VMEM((len(dils), l, LANES), F32), pltpu.VMEM((len(dils), l, LANES), F32)]
    return pl.pallas_call(
        functools.partial(_mixer_kernel, dils=tuple(dils), slopes=slopes, with_sink=sinks is not None, seq=l),
        grid=(b, nslab),
        in_specs=in_specs,
        out_specs=pl.BlockSpec(blk, qmap),
        out_shape=jax.ShapeDtypeStruct(q4.shape, F32),
        scratch_shapes=scratch,
        compiler_params=_cparams("arbitrary", "arbitrary"),
        name="mixer_b" if shared else "mixer_a",
    )(*args)


def _qb_perm():
    idx = []
    for g in range(B_GROUP):
        for kv in range(B_KV_HEADS):
            base = (kv * B_GROUP + g) * HEAD_DIM
            idx.extend(range(base, base + HEAD_DIM))
    return jnp.asarray(idx, jnp.int32)


NEW = 8
HQ = A_HEADS * NEW


def _block_diag_rows(q8, width):
    rep = jnp.concatenate([q8] * A_HEADS, axis=0)
    r = lax.broadcasted_iota(jnp.int32, (HQ, width), 0)
    c = lax.broadcasted_iota(jnp.int32, (HQ, width), 1)
    return jnp.where(r // NEW == c // HEAD_DIM, rep, 0.0)


def _sample_a_kernel(q_ref, kn_ref, vn_ref, ck_ref, cv_ref, nk_ref, nv_ref, o_ref,
                     b1_ref, b4_ref, b16_ref, bn_ref, knb, vnb, *, la, slopes):
    spans = [w for (w, _) in A_PATTERNS]
    bias_refs = [b1_ref, b4_ref, b16_ref]

    @pl.when(pl.program_id(0) == 0)
    def _():
        for bi, (wdw, dil) in enumerate(A_PATTERNS):
            ncol = spans[bi]
            r = lax.broadcasted_iota(jnp.int32, (HQ, ncol), 0)
            c = lax.broadcasted_iota(jnp.int32, (HQ, ncol), 1)
            dist = (r % NEW) + ncol - c
            valid = (dist <= wdw) & (dist % dil == 0)
            slope = jnp.zeros((HQ, ncol), F32)
            for h in range(A_HEADS):
                slope = jnp.where(r // NEW == h, slopes[h], slope)
            bias_refs[bi][...] = jnp.where(valid, -slope * dist.astype(F32), NEG)
            rn = lax.broadcasted_iota(jnp.int32, (HQ, LANES), 0)
            cn = lax.broadcasted_iota(jnp.int32, (HQ, LANES), 1)
            dn = (rn % NEW) - cn
            vn_ok = (dn >= 0) & (dn % dil == 0) & (cn < NEW)
            sl = jnp.zeros((HQ, LANES), F32)
            for h in range(A_HEADS):
                sl = jnp.where(rn // NEW == h, slopes[h], sl)
            bn_ref[bi] = jnp.where(vn_ok, -sl * dn.astype(F32), NEG)
        knb[...] = jnp.zeros_like(knb)
        vnb[...] = jnp.zeros_like(vnb)

    kn = kn_ref[0]
    vn = vn_ref[0]
    nk_ref[0, 0:la - NEW, :] = ck_ref[0, NEW:la, :]
    nk_ref[0, la - NEW:la, :] = kn
    nv_ref[0, 0:la - NEW, :] = cv_ref[0, NEW:la, :]
    nv_ref[0, la - NEW:la, :] = vn
    knb[0:NEW, :] = kn
    vnb[0:NEW, :] = vn

    qbd = _block_diag_rows(q_ref[0], A_WIDTH).astype(BF16)
    nt = (((1,), (1,)), ((), ()))
    s_new = lax.dot_general(qbd, knb[...].astype(BF16), nt, preferred_element_type=F32) * SCALE
    ms, dens, nums = [], [], []
    for bi in range(len(A_PATTERNS)):
        ncol = spans[bi]
        kc = ck_ref[0, la - ncol:la, :].astype(BF16)
        vc = cv_ref[0, la - ncol:la, :].astype(BF16)
        sc = lax.dot_general(qbd, kc, nt, preferred_element_type=F32) * SCALE + bias_refs[bi][...]
        sn = s_new + bn_ref[bi]
        m = jnp.maximum(jnp.max(sc, axis=-1, keepdims=True), jnp.max(sn, axis=-1, keepdims=True))
        pc = jnp.exp(sc - m)
        pn = jnp.exp(sn - m)
        dens.append(jnp.sum(pc, axis=-1, keepdims=True) + jnp.sum(pn, axis=-1, keepdims=True))
        nums.append(jnp.dot(pc.astype(BF16), vc, preferred_element_type=F32)
                    + jnp.dot(pn.astype(BF16), vnb[...].astype(BF16), preferred_element_type=F32))
        ms.append(m)
    mx = jnp.maximum(jnp.maximum(ms[0], ms[1]), ms[2])
    den = 0.0
    num = 0.0
    for bi in range(len(A_PATTERNS)):
        wt = jnp.exp(ms[bi] - mx)
        den = den + dens[bi] * wt
        num = num + nums[bi] * wt
    out = _block_diag_rows_keep(num / den, A_WIDTH)
    o_ref[0] = sum(out[h * NEW:(h + 1) * NEW, :] for h in range(A_HEADS))


def _block_diag_rows_keep(x, width):
    r = lax.broadcasted_iota(jnp.int32, (HQ, width), 0)
    c = lax.broadcasted_iota(jnp.int32, (HQ, width), 1)
    return jnp.where(r // NEW == c // HEAD_DIM, x, 0.0)


def _sample_a(q, kn, vn, ck, cv):
    s, la, w = ck.shape
    assert la >= max(wd for (wd, _) in A_PATTERNS)
    seq = lambda i: (i, 0, 0)
    small = pl.BlockSpec((1, NEW, w), seq)
    big = pl.BlockSpec((1, la, w), seq)
    spans = [wd for (wd, _) in A_PATTERNS]
    return pl.pallas_call(
        functools.partial(_sample_a_kernel, la=la, slopes=_alibi_slopes(A_HEADS)),
        grid=(s,),
        in_specs=[small, small, small, big, big],
        out_specs=[big, big, small],
        out_shape=[jax.ShapeDtypeStruct(ck.shape, F32), jax.ShapeDtypeStruct(cv.shape, F32),
                   jax.ShapeDtypeStruct(q.shape, F32)],
        scratch_shapes=[pltpu.VMEM((HQ, spans[0]), F32), pltpu.VMEM((HQ, spans[1]), F32),
                        pltpu.VMEM((HQ, spans[2]), F32), pltpu.VMEM((len(A_PATTERNS), HQ, LANES), F32),
                        pltpu.VMEM((LANES, w), F32), pltpu.VMEM((LANES, w), F32)],
        compiler_params=_cparams("arbitrary"),
        name="sample_attn_a",
    )(q, kn, vn, ck, cv)


SAMPLE_CHUNK = 512


def _sample_a5_kernel(q_ref, kn_ref, vn_ref, ck_ref, cv_ref, nk_hbm, nv_hbm, o_ref,
                      b16_ref, b4_ref, b1_ref, bn_ref, sem, *, la, slopes):
    b = pl.program_id(0)
    copies = [
        pltpu.make_async_copy(ck_ref.at[0, 0, pl.ds(NEW, la - NEW)], nk_hbm.at[0, b, pl.ds(0, la - NEW)], sem.at[0]),
        pltpu.make_async_copy(cv_ref.at[0, 0, pl.ds(NEW, la - NEW)], nv_hbm.at[0, b, pl.ds(0, la - NEW)], sem.at[1]),
        pltpu.make_async_copy(kn_ref.at[0], nk_hbm.at[0, b, pl.ds(la - NEW, NEW)], sem.at[2]),
        pltpu.make_async_copy(vn_ref.at[0], nv_hbm.at[0, b, pl.ds(la - NEW, NEW)], sem.at[3]),
    ]
    for cp in copies:
        cp.start()

    spans = [wd for (wd, _) in A_PATTERNS]
    bias_refs = [b1_ref, b4_ref, b16_ref]

    @pl.when(b == 0)
    def _():
        for bi, (wdw, dil) in enumerate(A_PATTERNS):
            ncol = spans[bi] * A_HEADS
            r = lax.broadcasted_iota(jnp.int32, (HQ, ncol), 0)
            c = lax.broadcasted_iota(jnp.int32, (HQ, ncol), 1)
            dist = (r % NEW) + spans[bi] - c // A_HEADS
            valid = (dist <= wdw) & (dist % dil == 0) & (c % A_HEADS == r // NEW)
            slope = jnp.zeros((HQ, ncol), F32)
            for h in range(A_HEADS):
                slope = jnp.where(r // NEW == h, slopes[h], slope)
            bias_refs[bi][...] = jnp.where(valid, -slope * dist.astype(F32), NEG)
            rn = lax.broadcasted_iota(jnp.int32, (HQ, HQ), 0)
            cn = lax.broadcasted_iota(jnp.int32, (HQ, HQ), 1)
            dn = (rn % NEW) - cn // A_HEADS
            ok = (dn >= 0) & (dn % dil == 0) & (cn % A_HEADS == rn // NEW)
            sl = jnp.zeros((HQ, HQ), F32)
            for h in range(A_HEADS):
                sl = jnp.where(rn // NEW == h, slopes[h], sl)
            bn_ref[bi] = jnp.where(ok, -sl * dn.astype(F32), NEG)

    nt = (((1,), (1,)), ((), ()))
    q = q_ref[0].astype(BF16)

    def scores(k3):
        k2 = k3.reshape(k3.shape[0] * A_HEADS, HEAD_DIM).astype(BF16)
        return lax.dot_general(q, k2, nt, preferred_element_type=F32) * SCALE

    def flat_v(v3):
        return v3.reshape(v3.shape[0] * A_HEADS, HEAD_DIM).astype(BF16)

    def fold(state, s, v2):
        m_new = jnp.max(s, axis=-1, keepdims=True)
        if state is not None:
            m_old, den, num = state
            m_new = jnp.maximum(m_new, m_old)
        p = jnp.exp(s - m_new)
        d_new = jnp.sum(p, axis=-1, keepdims=True)
        n_new = jnp.dot(p.astype(BF16), v2, preferred_element_type=F32)
        if state is not None:
            alpha = jnp.exp(m_old - m_new)
            d_new = d_new + den * alpha
            n_new = n_new + num * alpha
        return m_new, d_new, n_new

    states = [None, None, None]
    nchunk = la // SAMPLE_CHUNK
    ch_cols = SAMPLE_CHUNK * A_HEADS
    for ci in range(nchunk):
        rows = slice(ci * SAMPLE_CHUNK, (ci + 1) * SAMPLE_CHUNK)
        s = scores(ck_ref[0, 0, rows])
        v2 = flat_v(cv_ref[0, 0, rows])
        states[2] = fold(states[2], s + b16_ref[:, ci * ch_cols:(ci + 1) * ch_cols], v2)
        if ci == nchunk - 1:
            for bi in (0, 1):
                ncol = spans[bi] * A_HEADS
                states[bi] = fold(states[bi], s[:, ch_cols - ncol:] + bias_refs[bi][...], v2[ch_cols - ncol:, :])
    s_new = scores(kn_ref[0])
    v_new = flat_v(vn_ref[0])
    for bi in range(len(A_PATTERNS)):
        states[bi] = fold(states[bi], s_new + bn_ref[bi], v_new)
    mx = functools.reduce(jnp.maximum, [st[0] for st in states])
    den = 0.0
    num = 0.0
    for (m, dd, nn) in states:
        wt = jnp.exp(m - mx)
        den = den + dd * wt
        num = num + nn * wt
    o_ref[0] = num / den

    for cp in copies:
        cp.wait()


def _sample_a5(q64, kn, vn, ck5, cv5):
    _, s, la, nh, hd = ck5.shape
    spans = [wd for (wd, _) in A_PATTERNS]
    assert la == max(spans) and la % SAMPLE_CHUNK == 0 and spans[1] == SAMPLE_CHUNK and nh == A_HEADS
    seq3 = lambda i: (i, 0, 0)
    seq4 = lambda i: (i, 0, 0, 0)
    big = pl.BlockSpec((1, 1, la, nh, hd), lambda i: (0, i, 0, 0, 0))
    anyspec = pl.BlockSpec(memory_space=pl.ANY)
    return pl.pallas_call(
        functools.partial(_sample_a5_kernel, la=la, slopes=_alibi_slopes(A_HEADS)),
        grid=(s,),
        in_specs=[pl.BlockSpec((1, HQ, hd), seq3), pl.BlockSpec((1, NEW, nh, hd), seq4),
                  pl.BlockSpec((1, NEW, nh, hd), seq4), big, big],
        out_specs=[anyspec, anyspec, pl.BlockSpec((1, HQ, hd), seq3)],
        out_shape=[jax.ShapeDtypeStruct(ck5.shape, F32), jax.ShapeDtypeStruct(cv5.shape, F32),
                   jax.ShapeDtypeStruct(q64.shape, F32)],
        scratch_shapes=[pltpu.VMEM((HQ, spans[2] * nh), F32), pltpu.VMEM((HQ, spans[1] * nh), F32),
                        pltpu.VMEM((HQ, spans[0] * nh), F32), pltpu.VMEM((len(A_PATTERNS), HQ, HQ), F32),
                        pltpu.SemaphoreType.DMA((4,))],
        compiler_params=_cparams("arbitrary"),
        name="sample_attn_a5",
    )(q64, kn, vn, ck5, cv5)


ROLL_ROWS = 64


def _sample_at_kernel(q_ref, kn_ref, vn_ref, ck_ref, cv_ref, nk_ref, nv_ref, o_ref, bias_ref, bn_ref, knb, vnb,
                      *, la, slopes):
    w = A_WIDTH
    nbr = len(A_PATTERNS)

    @pl.when(pl.program_id(0) == 0)
    def _():
        r = lax.broadcasted_iota(jnp.int32, (HQ, la), 0)
        c = lax.broadcasted_iota(jnp.int32, (HQ, la), 1)
        dist = (r % NEW) + la - c
        slope = jnp.zeros((HQ, la), F32)
        rn = lax.broadcasted_iota(jnp.int32, (HQ, LANES), 0)
        cn = lax.broadcasted_iota(jnp.int32, (HQ, LANES), 1)
        dn = (rn % NEW) - cn
        sl = jnp.zeros((HQ, LANES), F32)
        for h in range(A_HEADS):
            slope = jnp.where(r // NEW == h, slopes[h], slope)
            sl = jnp.where(rn // NEW == h, slopes[h], sl)
        for bi, (wdw, dil) in enumerate(A_PATTERNS):
            bias_ref[bi] = jnp.where((dist <= wdw) & (dist % dil == 0), -slope * dist.astype(F32), NEG)
            bn_ref[bi] = jnp.where((dn >= 0) & (dn % dil == 0) & (cn < NEW), -sl * dn.astype(F32), NEG)
        knb[...] = jnp.zeros_like(knb)
        vnb[...] = jnp.zeros_like(vnb)

    knb[:, 0:NEW] = kn_ref[0]
    vnb[:, 0:NEW] = vn_ref[0]

    qbd = _block_diag_rows(q_ref[0], w).astype(BF16)
    nt = (((1,), (1,)), ((), ()))
    s_c = jnp.dot(qbd, ck_ref[0].astype(BF16), preferred_element_type=F32) * SCALE
    s_n = jnp.dot(qbd, knb[...].astype(BF16), preferred_element_type=F32) * SCALE
    ms, dens, pcs, pns = [], [], [], []
    for bi in range(nbr):
        sc = s_c + bias_ref[bi]
        sn = s_n + bn_ref[bi]
        m = jnp.maximum(jnp.max(sc, axis=-1, keepdims=True), jnp.max(sn, axis=-1, keepdims=True))
        pc = jnp.exp(sc - m)
        pn = jnp.exp(sn - m)
        dens.append(jnp.sum(pc, axis=-1, keepdims=True) + jnp.sum(pn, axis=-1, keepdims=True))
        ms.append(m)
        pcs.append(pc.astype(BF16))
        pns.append(pn.astype(BF16))
    nums = (lax.dot_general(jnp.concatenate(pcs, axis=0), cv_ref[0].astype(BF16), nt, preferred_element_type=F32)
            + lax.dot_general(jnp.concatenate(pns, axis=0), vnb[...].astype(BF16), nt, preferred_element_type=F32))
    mx = functools.reduce(jnp.maximum, ms)
    den = 0.0
    num = 0.0
    for bi in range(nbr):
        wt = jnp.exp(ms[bi] - mx)
        den = den + dens[bi] * wt
        num = num + nums[bi * HQ:(bi + 1) * HQ, :] * wt
    out = _block_diag_rows_keep(num / den, w)
    o_ref[0] = sum(out[h * NEW:(h + 1) * NEW, :] for h in range(A_HEADS))

    lane = lax.broadcasted_iota(jnp.int32, (ROLL_ROWS, LANES), 1)
    tail = lane >= LANES - NEW
    for src, newb, dst in ((ck_ref, knb, nk_ref), (cv_ref, vnb, nv_ref)):
        for r0 in range(0, w, ROLL_ROWS):
            rows = slice(r0, r0 + ROLL_ROWS)
            rolled = pltpu.roll(src[0, rows, :], la - NEW, 1)
            dst[0, rows, 0:la - LANES] = rolled[:, 0:la - LANES]
            fresh = pltpu.roll(newb[rows, :], LANES - NEW, 1)
            dst[0, rows, la - LANES:la] = jnp.where(tail, fresh, rolled[:, la - LANES:la])


def _sample_at(q, kn_t, vn_t, ck_t, cv_t):
    s, w, la = ck_t.shape
    assert la >= max(wd for (wd, _) in A_PATTERNS) and la % LANES == 0
    seq = lambda i: (i, 0, 0)
    big = pl.BlockSpec((1, w, la), seq)
    new = pl.BlockSpec((1, w, NEW), seq)
    qspec = pl.BlockSpec((1, NEW, w), seq)
    nbr = len(A_PATTERNS)
    return pl.pallas_call(
        functools.partial(_sample_at_kernel, la=la, slopes=_alibi_slopes(A_HEADS)),
        grid=(s,),
        in_specs=[qspec, new, new, big, big],
        out_specs=[big, big, qspec],
        out_shape=[jax.ShapeDtypeStruct(ck_t.shape, F32), jax.ShapeDtypeStruct(cv_t.shape, F32),
                   jax.ShapeDtypeStruct(q.shape, F32)],
        scratch_shapes=[pltpu.VMEM((nbr, HQ, la), F32), pltpu.VMEM((nbr, HQ, LANES), F32),
                        pltpu.VMEM((w, LANES), F32), pltpu.VMEM((w, LANES), F32)],
        compiler_params=_cparams("arbitrary"),
        name="sample_attn_at",
    )(q, kn_t, vn_t, ck_t, cv_t)


def _sample_b_kernel(sink_ref, q_ref, kn_ref, vn_ref, ck_ref, cv_ref, nk_ref, nv_ref, o_ref,
                     bc_ref, bn_ref, sk_ref, knb, vnb, *, lb, nseq, slopes):
    @pl.when(pl.program_id(0) == 0)
    def _():
        r = lax.broadcasted_iota(jnp.int32, (HQ, lb), 0)
        c = lax.broadcasted_iota(jnp.int32, (HQ, lb), 1)
        dist = (r % NEW) + lb - c
        slope = jnp.zeros((HQ, lb), F32)
        sink = jnp.zeros((HQ, LANES), F32)
        rs = lax.broadcasted_iota(jnp.int32, (HQ, LANES), 0)
        for h in range(B_HEADS):
            slope = jnp.where(r // NEW == h, slopes[h], slope)
            sink = jnp.where(rs // NEW == h, sink_ref[h], sink)
        bc_ref[...] = jnp.where(dist <= BAND, -slope * dist.astype(F32), NEG)
        cn = lax.broadcasted_iota(jnp.int32, (HQ, LANES), 1)
        dn = (rs % NEW) - cn
        sl = jnp.zeros((HQ, LANES), F32)
        for h in range(B_HEADS):
            sl = jnp.where(rs // NEW == h, slopes[h], sl)
        bn_ref[...] = jnp.where((dn >= 0) & (cn < NEW), -sl * dn.astype(F32), NEG)
        sk_ref[...] = sink
        knb[...] = jnp.zeros_like(knb)
        vnb[...] = jnp.zeros_like(vnb)

    nt = (((1,), (1,)), ((), ()))
    rr = lax.broadcasted_iota(jnp.int32, (HQ, LANES), 0)
    cc = lax.broadcasted_iota(jnp.int32, (HQ, LANES), 1)
    own = ((rr // NEW) % B_KV_HEADS) == (cc // HEAD_DIM)
    sink = sk_ref[:, 0:1]
    tail = cc >= LANES - NEW
    for t in range(nseq):
        knb[:, 0:NEW] = kn_ref[t]
        vnb[:, 0:NEW] = vn_ref[t]
        for src, newb, dst in ((ck_ref, knb, nk_ref), (cv_ref, vnb, nv_ref)):
            for r0 in range(0, LANES, HQ):
                rows = slice(r0, r0 + HQ)
                dst[t, rows, :] = jnp.where(tail, pltpu.roll(newb[rows, :], LANES - NEW, 1),
                                            pltpu.roll(src[t, rows, :], lb - NEW, 1))
        q = q_ref[t]
        rows = []
        for g in range(B_GROUP):
            for kv in range(B_KV_HEADS):
                rows.append(q[:, g * LANES:(g + 1) * LANES])
        qbd = jnp.where(own, jnp.concatenate(rows, axis=0), 0.0).astype(BF16)
        sc = jnp.dot(qbd, ck_ref[t].astype(BF16), preferred_element_type=F32) * SCALE + bc_ref[...]
        sn = jnp.dot(qbd, knb[...].astype(BF16), preferred_element_type=F32) * SCALE + bn_ref[...]
        m = jnp.maximum(jnp.maximum(jnp.max(sc, axis=-1, keepdims=True), jnp.max(sn, axis=-1, keepdims=True)), sink)
        pc = jnp.exp(sc - m)
        pn = jnp.exp(sn - m)
        den = jnp.sum(pc, axis=-1, keepdims=True) + jnp.sum(pn, axis=-1, keepdims=True) + jnp.exp(sink - m)
        o = (lax.dot_general(pc.astype(BF16), cv_ref[t].astype(BF16), nt, preferred_element_type=F32)
             + lax.dot_general(pn.astype(BF16), vnb[...].astype(BF16), nt, preferred_element_type=F32)) / den
        o = jnp.where(own, o, 0.0)
        for g in range(B_GROUP):
            base = g * B_KV_HEADS * NEW
            o_ref[t, :, g * LANES:(g + 1) * LANES] = o[base:base + NEW, :] + o[base + NEW:base + 2 * NEW, :]


def _sample_b(sinks_perm, slopes_perm, q, kn, vn, ck, cv, nseq=8):
    s, kw, lb = ck.shape
    assert lb == BAND and kw == LANES and lb == LANES and s % nseq == 0
    seq = lambda i: (i, 0, 0)
    return pl.pallas_call(
        functools.partial(_sample_b_kernel, lb=lb, nseq=nseq, slopes=slopes_perm),
        grid=(s // nseq,),
        in_specs=[pl.BlockSpec(memory_space=pltpu.SMEM), pl.BlockSpec((nseq, NEW, A_WIDTH), seq),
                  pl.BlockSpec((nseq, kw, NEW), seq), pl.BlockSpec((nseq, kw, NEW), seq),
                  pl.BlockSpec((nseq, kw, lb), seq), pl.BlockSpec((nseq, kw, lb), seq)],
        out_specs=[pl.BlockSpec((nseq, kw, lb), seq), pl.BlockSpec((nseq, kw, lb), seq),
                   pl.BlockSpec((nseq, NEW, A_WIDTH), seq)],
        out_shape=[jax.ShapeDtypeStruct(ck.shape, F32), jax.ShapeDtypeStruct(cv.shape, F32),
                   jax.ShapeDtypeStruct(q.shape, F32)],
        scratch_shapes=[pltpu.VMEM((HQ, lb), F32), pltpu.VMEM((HQ, LANES), F32), pltpu.VMEM((HQ, LANES), F32),
                        pltpu.VMEM((LANES, kw), F32), pltpu.VMEM((LANES, kw), F32)],
        compiler_params=_cparams("arbitrary"),
        name="sample_attn_b",
    )(sinks_perm, q, kn, vn, ck, cv)


def _outproj_kernel(x_ref, oa_ref, ob_ref, wa_ref, wb_ref, ht_ref):
    oa = jnp.concatenate([oa_ref[s] for s in range(NSLAB)], axis=1).astype(BF16)
    ob = jnp.concatenate([ob_ref[s] for s in range(NSLAB)], axis=1).astype(BF16)
    h = (x_ref[...] + jnp.dot(oa, wa_ref[...], preferred_element_type=F32)
         + jnp.dot(ob, wb_ref[...], preferred_element_type=F32))
    ht_ref[...] = h.T


def _outproj(x2d, oa4, ob4, wa, wb, tm):
    n, d = x2d.shape
    w = A_WIDTH
    row = lambda i: (i, 0)
    const = lambda i: (0, 0)
    slab = pl.BlockSpec((NSLAB, tm, LANES), lambda i: (0, i, 0))
    return pl.pallas_call(
        _outproj_kernel,
        grid=(n // tm,),
        in_specs=[pl.BlockSpec((tm, d), row), slab, slab, pl.BlockSpec((w, d), const), pl.BlockSpec((w, d), const)],
        out_specs=pl.BlockSpec((d, tm), lambda i: (0, i)),
        out_shape=jax.ShapeDtypeStruct((d, n), F32),
        compiler_params=_cparams("parallel"),
        name="out_proj",
    )(x2d, oa4, ob4, wa, wb)


BIG = 3.0e38


SUBLANES = 8


def _merge_exchange(n):
    pairs = []
    t = max(1, math.ceil(math.log2(n)))
    p = 1 << (t - 1)
    while p > 0:
        q, r, d = 1 << (t - 1), 0, p
        while d > 0:
            pairs.extend((i, i + d) for i in range(n - d) if (i & p) == r)
            d, q, r = q - p, q >> 1, p
        p >>= 1
    return pairs


def _vmax(a, b):
    if a is None:
        return b
    if b is None:
        return a
    return jnp.maximum(a, b)


def _vmin(a, b):
    if a is None or b is None:
        return None
    return jnp.minimum(a, b)


def _exchange(x, i, j):
    x[i], x[j] = _vmax(x[i], x[j]), _vmin(x[i], x[j])


def _top16(tiles):
    x = list(tiles) + [None] * (PEER_TOPK - len(tiles))
    for (i, j) in _merge_exchange(len(tiles)):
        _exchange(x, i, j)
    for shift in (4, 2, 1):
        y = [None if v is None else pltpu.roll(v, shift, 0) for v in x]
        x = [_vmax(x[k], y[PEER_TOPK - 1 - k]) for k in range(PEER_TOPK)]
        for d in (8, 4, 2, 1):
            for i in range(PEER_TOPK):
                if not i & d:
                    _exchange(x, i, i + d)
    return x


def _rows_sum(x):
    for shift in (4, 2, 1):
        x = x + pltpu.roll(x, shift, 0)
    return x


def _peer_select_kernel(ht_ref, g_ref, wq_ref, keys_ref,
                        hn_ref, r1_ref, w1_ref, cnt_ref, w0_ref, s_ref, *, nk, tn):
    h = ht_ref[...]
    ms = jnp.mean(h * h, axis=0, keepdims=True)
    hn = (h * lax.rsqrt(ms + NORM_EPS) * g_ref[...]).astype(BF16)
    hn_ref[...] = hn
    q = jnp.dot(wq_ref[...], hn, preferred_element_type=F32)
    half = q.shape[0] // (2 * PEER_HEADS)
    for k in range(2 * PEER_HEADS):
        qk = q[k * half:(k + 1) * half, :].astype(BF16)
        s_ref[k] = jnp.dot(keys_ref[k], qk, preferred_element_type=F32)
    nchunk = tn // LANES
    ntile = nk // SUBLANES
    sub = lax.broadcasted_iota(jnp.int32, (SUBLANES, LANES), 0)

    def pack(vals):
        out = vals[-1]
        for r in range(len(vals) - 2, -1, -1):
            out = jnp.where(sub == r, vals[r], out)
        return out

    def head_chunk(t, carry):
        hd = t // nchunk
        lanes = pl.ds(pl.multiple_of((t % nchunk) * LANES, LANES), LANES)
        rows = [slice(k * SUBLANES, (k + 1) * SUBLANES) for k in range(ntile)]
        s0 = [s_ref[2 * hd, r, lanes] for r in rows]
        s1 = [s_ref[2 * hd + 1, r, lanes] for r in rows]
        v0 = _top16(s0)
        v1 = _top16(s1)
        v1lo, v1hi, v0hi = pack(v1[0:8]), pack(v1[8:16]), pack(v0[8:16])
        cands = [v0[0] + v1lo, v0[0] + v1hi] + [v0[a] + v1lo for a in range(1, 8)] + [v0hi + v1[0]]
        best = _top16(cands)
        top, tau = best[0], best[PEER_TOPK - 1]
        z = _rows_sum(sum(jnp.where(c >= tau, jnp.exp(c - top), 0.0) for c in cands))
        inv_z = 1.0 / z
        for m in range(ntile // 2):
            cnt, rk1 = [], []
            for k in (2 * m, 2 * m + 1):
                c = jnp.zeros((SUBLANES, LANES), F32)
                r = jnp.zeros((SUBLANES, LANES), F32)
                for b in range(PEER_TOPK):
                    c = c + jnp.where(s0[k] + v1[b] >= tau, 1.0, 0.0)
                    r = r + jnp.where(v1[b] > s1[k], 1.0, 0.0)
                cnt.append(c)
                rk1.append(r)
            pair = slice(2 * m * SUBLANES, (2 * m + 2) * SUBLANES)
            both = lambda f: jnp.concatenate([f(2 * m), f(2 * m + 1)], axis=0)
            cnt_ref[hd, pair, lanes] = jnp.concatenate(cnt, axis=0)
            w0_ref[hd, pair, lanes] = both(lambda k: jnp.exp(s0[k] - v0[0]) * inv_z)
            r1_ref[hd, pair, lanes] = jnp.concatenate(rk1, axis=0).astype(BF16)
            w1_ref[hd, pair, lanes] = both(lambda k: jnp.exp(s1[k] - v1[0])).astype(BF16)
        return carry

    lax.fori_loop(0, PEER_HEADS * nchunk, head_chunk, 0)


def _peer_select(ht, g_col, wq_t, keys, tn):
    d, n = ht.shape
    assert tn % LANES == 0 and n % tn == 0
    nslab, nk, half = keys.shape
    tok = lambda i: (0, i)
    tok3 = lambda i: (0, 0, i)
    stat = lambda dt: jax.ShapeDtypeStruct((PEER_HEADS, nk, n), dt)
    return pl.pallas_call(
        functools.partial(_peer_select_kernel, nk=nk, tn=tn),
        grid=(n // tn,),
        in_specs=[pl.BlockSpec((d, tn), tok), pl.BlockSpec((d, 1), lambda i: (0, 0)),
                  pl.BlockSpec(wq_t.shape, lambda i: (0, 0)), pl.BlockSpec(keys.shape, lambda i: (0, 0, 0))],
        out_specs=[pl.BlockSpec((d, tn), tok)] + [pl.BlockSpec((PEER_HEADS, nk, tn), tok3)] * 4,
        out_shape=[jax.ShapeDtypeStruct((d, n), BF16), stat(BF16), stat(BF16), stat(F32), stat(F32)],
        scratch_shapes=[pltpu.VMEM((nslab, nk, tn), F32)],
        compiler_params=_cparams("parallel"),
        name="peer_select",
    )(ht, g_col, wq_t, keys)


def _gelu(x):
    return 0.5 * x * (1.0 + lax.erf(x * (2.0 ** -0.5)))


def _peer_dense_kernel(hn_ref, u_ref, vt_ref, r1_ref, w1_ref, cnt_ref, w0_ref, ht_ref, y_ref,
                       acc_ref, a0_ref, a1_ref, c0_ref, c1_ref, *, nk, te, tn, lc, rb, ne, nsteps):
    s = pl.program_id(0)

    @pl.when(s == 0)
    def _():
        for r in (a0_ref, a1_ref, c0_ref, c1_ref, acc_ref):
            r[...] = jnp.zeros_like(r)

    e = jnp.clip(s - 1, 0, nsteps - 1) % ne
    p2 = s - 2
    first = (jnp.maximum(p2, 0) % ne) == 0
    nslab = te // nk

    def stages(a_new, a_prev, c_new, c_prev):
        a_new[...] = jnp.dot(u_ref[...], hn_ref[...], preferred_element_type=F32)
        for ii in range(nslab):
            i = e * nslab + ii
            cnt_rows = [cnt_ref[hd, pl.ds(i, 1), :].astype(BF16) for hd in range(PEER_HEADS)]
            w0_rows = [w0_ref[hd, pl.ds(i, 1), :].astype(BF16) for hd in range(PEER_HEADS)]
            for c in range(tn // lc):
                ls = slice(c * lc, (c + 1) * lc)
                for jb in range(nk // rb):
                    js = slice(jb * rb, (jb + 1) * rb)
                    rs = slice(ii * nk + jb * rb, ii * nk + (jb + 1) * rb)
                    g = None
                    for hd in range(PEER_HEADS):
                        term = jnp.where(r1_ref[hd, js, ls] < cnt_rows[hd][:, ls], w1_ref[hd, js, ls],
                                         jnp.zeros((), BF16)) * w0_rows[hd][:, ls]
                        g = term if g is None else g + term
                    c_new[rs, ls] = g * _gelu(a_prev[rs, ls]).astype(BF16)
        contrib = jnp.dot(vt_ref[...], c_prev[...], preferred_element_type=F32)
        acc_ref[...] = jnp.where(first, contrib, acc_ref[...] + contrib)

    @pl.when(s % 2 == 0)
    def _():
        stages(a0_ref, a1_ref, c1_ref, c0_ref)

    @pl.when(s % 2 == 1)
    def _():
        stages(a1_ref, a0_ref, c0_ref, c1_ref)

    @pl.when((p2 >= 0) & (p2 % ne == ne - 1))
    def _():
        y_ref[...] = (ht_ref[...] + acc_ref[...]).T


def _peer_dense(hn, u, vt, r1, w1, cnt, w0, ht, *, tn, te, lc, rb):
    d, n = hn.shape
    ne = u.shape[0] // te
    nk = r1.shape[1]
    nsteps = (n // tn) * ne
    last = nsteps - 1
    pair = lambda s, lag: jnp.clip(s - lag, 0, last)
    stat = pl.BlockSpec((PEER_HEADS, nk, tn), lambda s: (0, 0, pair(s, 1) // ne))
    return pl.pallas_call(
        functools.partial(_peer_dense_kernel, nk=nk, te=te, tn=tn, lc=lc, rb=min(rb, nk), ne=ne, nsteps=nsteps),
        grid=(nsteps + 2,),
        in_specs=[pl.BlockSpec((d, tn), lambda s: (0, pair(s, 0) // ne)),
                  pl.BlockSpec((te, d), lambda s: (pair(s, 0) % ne, 0)),
                  pl.BlockSpec((d, te), lambda s: (0, pair(s, 2) % ne)),
                  stat, stat, stat, stat,
                  pl.BlockSpec((d, tn), lambda s: (0, pair(s, 2) // ne))],
        out_specs=pl.BlockSpec((tn, d), lambda s: (pair(s, 2) // ne, 0)),
        out_shape=jax.ShapeDtypeStruct((n, d), F32),
        scratch_shapes=[pltpu.VMEM((d, tn), F32), pltpu.VMEM((te, tn), F32), pltpu.VMEM((te, tn), F32),
                        pltpu.VMEM((te, tn), BF16), pltpu.VMEM((te, tn), BF16)],
        compiler_params=_cparams("arbitrary", interleave=True),
        name="peer_dense",
    )(hn, u, vt, r1, w1, cnt, w0, ht)


def _peer(ht, g_col, wq_t, keys, u, vt, *, tn_sel, tn, te, lc, rb):
    hn, r1, w1, cnt, w0 = _peer_select(ht, g_col, wq_t, keys, tn_sel)
    return _peer_dense(hn, u, vt, r1, w1, cnt, w0, ht, tn=tn, te=te, lc=lc, rb=rb)


TOKEN_TILE = 512
EXPERT_TILE = 1024
GATE_LANES = 256
GATE_ROWS = 64


def _peer_tiles(n_tokens, n_keys):
    tn = min(TOKEN_TILE, n_tokens)
    return dict(tn_sel=tn, tn=tn, te=min(EXPERT_TILE, n_keys * n_keys), lc=min(GATE_LANES, tn), rb=GATE_ROWS)


def kernel(x_prompt, x_sample, cache_a_k, cache_a_v, cache_b_k, cache_b_v, norm_attn, w_in, g_qa, g_ka, g_qb, g_kb, sinks, w_o, norm_ffn, peer_wq, peer_keys, peer_u, peer_v):
    b, l, d = x_prompt.shape
    s, ns, _ = x_sample.shape
    assert ns == NEW and w_in.shape[0] == 1
    la, lb = cache_a_k.shape[2], cache_b_k.shape[2]
    w = A_WIDTH
    perm = _qb_perm()

    wl = w_in[0]
    w_all = jnp.concatenate([wl[:, :3 * w], wl[:, 3 * w:4 * w][:, perm], wl[:, 4 * w:]], axis=1).astype(BF16)
    seg = (jnp.arange(w)[:, None] // HEAD_DIM == jnp.arange(w)[None, :] // HEAD_DIM).astype(BF16)
    t8 = lambda g: jnp.tile(g, A_HEADS)[None, :]
    gains = (t8(g_qa[0]), t8(g_ka[0]), t8(g_qb[0]), jnp.tile(g_kb[0], B_KV_HEADS)[None, :])
    sb = _alibi_slopes(B_HEADS)
    slopes_perm = [sb[(h % 2) * B_GROUP + h // 2] for h in range(B_HEADS)]
    sinks_perm = jnp.stack([sinks[0, (h % 2) * B_GROUP + h // 2] for h in range(B_HEADS)])
    wo_a = w_o[0, :w, :].astype(BF16)
    wo_b = w_o[0, w:, :][perm, :].astype(BF16)
    g_col = norm_ffn[0][:, None]
    wq_t = peer_wq[0].T.astype(BF16)
    nk = peer_keys.shape[3]
    keys = peer_keys[0].reshape(2 * PEER_HEADS, nk, peer_keys.shape[4]).astype(BF16)
    u = peer_u[0].astype(BF16)
    vt = peer_v[0].T.astype(BF16)

    xp = x_prompt.reshape(b * l, d)
    tm_p = min(TOKEN_TILE, l)
    qa4, ka4, va4, qb4, kat, vat, kb, vb, kbt, vbt = _project(xp, norm_attn, w_all, seg, *gains, tm_p, b)
    seq4 = lambda t: t.reshape(t.shape[0], b, l, LANES)
    oa4 = _mixer(seq4(qa4), seq4(ka4), seq4(va4), dils=[dl for (_, dl) in A_PATTERNS],
                 slopes=_alibi_slopes(A_HEADS))
    ob4 = _mixer(seq4(qb4), kb.reshape(1, b, l, LANES), vb.reshape(1, b, l, LANES), dils=[1],
                 slopes=slopes_perm, sinks=sinks_perm)
    ht_p = _outproj(xp, oa4.reshape(NSLAB, b * l, LANES), ob4.reshape(NSLAB, b * l, LANES), wo_a, wo_b, tm_p)
    y_p = _peer(ht_p, g_col, wq_t, keys, u, vt, **_peer_tiles(b * l, nk))
    na = min(la, l)
    nb = min(lb, l)
    def tail(t, heads, keep):
        t = t[:, :, l - keep:].reshape(b, heads, HEAD_DIM, keep)
        return jnp.transpose(t, (0, 3, 1, 2))[None]
    pak, pav = tail(kat, A_HEADS, na), tail(vat, A_HEADS, na)
    pbk, pbv = tail(kbt, B_KV_HEADS, nb), tail(vbt, B_KV_HEADS, nb)

    xs = x_sample.reshape(s * ns, d)
    tm_s = min(TOKEN_TILE, s * ns)
    qa4, _, _, qb4, kat, vat, _, _, kbt, vbt = _project(xs, norm_attn, w_all, seg, *gains, tm_s, 1)
    wide = lambda t4: jnp.transpose(t4, (1, 0, 2)).reshape(s, ns, w)
    slabs = lambda t: jnp.transpose(t.reshape(s * ns, NSLAB, LANES), (1, 0, 2))
    to_t = lambda c: jnp.transpose(c[0], (0, 2, 3, 1)).reshape(s, w, la)
    from_t = lambda t: jnp.transpose(t.reshape(s, A_HEADS, HEAD_DIM, la), (0, 3, 1, 2))[None]
    new_t = lambda t: jnp.transpose(t.reshape(t.shape[1], s, ns), (1, 0, 2))
    sak_t, sav_t, oa = _sample_at(wide(qa4), new_t(kat), new_t(vat), to_t(cache_a_k), to_t(cache_a_v))
    sak, sav = from_t(sak_t), from_t(sav_t)
    to_tb = lambda c: jnp.transpose(c[0], (0, 2, 3, 1)).reshape(s, LANES, lb)
    from_tb = lambda t: jnp.transpose(t.reshape(s, B_KV_HEADS, HEAD_DIM, lb), (0, 3, 1, 2))[None]
    sbk_t, sbv_t, ob = _sample_b(sinks_perm, slopes_perm, wide(qb4), new_t(kbt), new_t(vbt),
                                 to_tb(cache_b_k), to_tb(cache_b_v))
    ht_s = _outproj(xs, slabs(oa), slabs(ob), wo_a, wo_b, tm_s)
    y_s = _peer(ht_s, g_col, wq_t, keys, u, vt, **_peer_tiles(s * ns, nk))

    return (y_p.reshape(b, l, d), y_s.reshape(s, ns, d), pak, pav, pbk, pbv,
            sak, sav, from_tb(sbk_t), from_tb(sbv_t))
```

```python
import functools
import math

import jax
import jax.numpy as jnp
from jax import lax
from jax.experimental import pallas as pl
from jax.experimental.pallas import tpu as pltpu

HEAD_DIM = 64
A_HEADS = 8
B_HEADS = 8
B_KV_HEADS = 2
B_GROUP = B_HEADS // B_KV_HEADS
A_PATTERNS = ((128, 1), (512, 4), (2048, 16))
BAND = 128
A_WIDTH = A_HEADS * HEAD_DIM
PEER_HEADS = 8
PEER_TOPK = 16
NORM_EPS = 1e-6
NEG = -1e30
SCALE = HEAD_DIM ** -0.5
LANES = 128
VMEM_LIMIT_BYTES = 56 * 1024 * 1024

BF16 = jnp.bfloat16
F32 = jnp.float32


def _alibi_slopes(n):
    return [2.0 ** (-8.0 * (i + 1) / n) for i in range(n)]


def _cparams(*sem, interleave=False):
    del interleave
    return pltpu.CompilerParams(dimension_semantics=sem, vmem_limit_bytes=VMEM_LIMIT_BYTES)


def _head_rms(h, seg, gain):
    sq = h * h
    hi = sq.astype(BF16)
    lo = (sq - hi.astype(F32)).astype(BF16)
    ms = (jnp.dot(hi, seg, preferred_element_type=F32)
          + jnp.dot(lo, seg, preferred_element_type=F32)) * (1.0 / HEAD_DIM)
    return h * lax.rsqrt(ms + NORM_EPS) * gain


NSLAB = A_WIDTH // LANES


def _proj_kernel(x_ref, g_ref, w_ref, seg_ref, gqa_ref, gka_ref, gqb_ref, gkb_ref,
                 qa4_ref, ka4_ref, va4_ref, qb4_ref, kat_ref, vat_ref, kb_ref, vb_ref, kbt_ref, vbt_ref):
    x = x_ref[...]
    ms = jnp.mean(x * x, axis=-1, keepdims=True)
    xn = (x * lax.rsqrt(ms + NORM_EPS) * g_ref[...]).astype(BF16)
    h = jnp.dot(xn, w_ref[...], preferred_element_type=F32)
    seg = seg_ref[...]
    w = A_WIDTH
    qa = _head_rms(h[:, 0:w], seg, gqa_ref[...])
    ka = _head_rms(h[:, w:2 * w], seg, gka_ref[...])
    va = h[:, 2 * w:3 * w]
    qb = _head_rms(h[:, 3 * w:4 * w], seg, gqb_ref[...])
    kb = _head_rms(h[:, 4 * w:4 * w + LANES], seg[0:LANES, 0:LANES], gkb_ref[...])
    vb = h[:, 4 * w + LANES:4 * w + 2 * LANES]
    kb_ref[...] = kb
    vb_ref[...] = vb
    kat_ref[...] = ka.T
    vat_ref[...] = va.T
    kbt_ref[...] = kb.T
    vbt_ref[...] = vb.T
    for s in range(NSLAB):
        sl = slice(s * LANES, (s + 1) * LANES)
        qa4_ref[s] = qa[:, sl]
        ka4_ref[s] = ka[:, sl]
        va4_ref[s] = va[:, sl]
        qb4_ref[s] = qb[:, sl]


def _project(x2d, norm_g, w_bf16, seg, gqa, gka, gqb, gkb, tm, nseq):
    n, d = x2d.shape
    nc = w_bf16.shape[1]
    w = A_WIDTH
    l = n // nseq
    tps = l // tm
    row = lambda i: (i, 0)
    col = lambda i: (i // tps, 0, i % tps)
    slab = lambda i: (0, i, 0)
    const = lambda i: (0, 0)
    outs = ([jax.ShapeDtypeStruct((NSLAB, n, LANES), F32)] * 4 + [jax.ShapeDtypeStruct((nseq, w, l), F32)] * 2
            + [jax.ShapeDtypeStruct((n, LANES), F32)] * 2 + [jax.ShapeDtypeStruct((nseq, LANES, l), F32)] * 2)
    return pl.pallas_call(
        _proj_kernel,
        grid=(n // tm,),
        in_specs=[pl.BlockSpec((tm, d), row), pl.BlockSpec((1, d), const), pl.BlockSpec((d, nc), const),
                  pl.BlockSpec((w, w), const), pl.BlockSpec((1, w), const), pl.BlockSpec((1, w), const),
                  pl.BlockSpec((1, w), const), pl.BlockSpec((1, LANES), const)],
        out_specs=([pl.BlockSpec((NSLAB, tm, LANES), slab)] * 4 + [pl.BlockSpec((None, w, tm), col)] * 2
                   + [pl.BlockSpec((tm, LANES), row)] * 2 + [pl.BlockSpec((None, LANES, tm), col)] * 2),
        out_shape=outs,
        compiler_params=_cparams("parallel"),
        name="qkv_proj",
    )(x2d, norm_g, w_bf16, seg, gqa, gka, gqb, gkb)


def _band_kernel(*refs, tq, dil, slopes, with_sink, with_lse):
    if with_sink:
        sink_ref, refs = refs[0], refs[1:]
    q_ref, kc_ref, kp_ref, vc_ref, vp_ref = refs[:5]
    o_ref = refs[5]
    lse_ref = refs[6] if with_lse else None
    kbuf, vbuf, bias_ref = refs[-3:]
    i = pl.program_id(2)
    nsub = tq // BAND

    @pl.when((pl.program_id(0) == 0) & (pl.program_id(1) == 0) & (i == 0))
    def _():
        r = lax.broadcasted_iota(jnp.int32, (BAND, 2 * BAND), 0)
        c = lax.broadcasted_iota(jnp.int32, (BAND, 2 * BAND), 1)
        dist = r + BAND - c
        valid = (dist >= 0) & (dist <= BAND)
        distf = (dist * dil).astype(F32)
        for h in range(A_HEADS):
            bias_ref[h] = jnp.where(valid, -slopes[h] * distf, NEG)

    kbuf[0:BAND, :] = kp_ref[0].astype(BF16)
    kbuf[BAND:BAND + tq, :] = kc_ref[0].astype(BF16)
    vbuf[0:BAND, :] = vp_ref[0].astype(BF16)
    vbuf[BAND:BAND + tq, :] = vc_ref[0].astype(BF16)

    lane = lax.broadcasted_iota(jnp.int32, (BAND, LANES), 1)
    col = lax.broadcasted_iota(jnp.int32, (1, 2 * BAND), 1)
    prev_cols = (col < BAND).astype(F32)

    def body(j, carry):
        r0 = pl.multiple_of(j * BAND, BAND)
        pen = jnp.where((i * nsub + j) == 0, NEG, 0.0) * prev_cols
        for hp in range(A_HEADS // 2):
            sl = slice(hp * LANES, (hp + 1) * LANES)
            qs = q_ref[0, pl.ds(r0, BAND), sl]
            kw = kbuf[pl.ds(r0, 2 * BAND), sl]
            vw = vbuf[pl.ds(r0, 2 * BAND), sl]
            outs, lses = [], []
            for e in range(2):
                h = 2 * hp + e
                qm = jnp.where((lane >= HEAD_DIM) == bool(e), qs, 0.0).astype(BF16)
                s = lax.dot_general(qm, kw, (((1,), (1,)), ((), ())), preferred_element_type=F32)
                s = s * SCALE + bias_ref[h] + pen
                m = jnp.max(s, axis=-1, keepdims=True)
                if with_sink:
                    m = jnp.maximum(m, sink_ref[h])
                p = jnp.exp(s - m)
                den = jnp.sum(p, axis=-1, keepdims=True)
                if with_sink:
                    den = den + jnp.exp(sink_ref[h] - m)
                o = jnp.dot(p.astype(BF16), vw, preferred_element_type=F32)
                outs.append(o / den)
                lses.append(m + jnp.log(den))
            hi = lane >= HEAD_DIM
            o_ref[0, pl.ds(r0, BAND), sl] = jnp.where(hi, outs[1], outs[0])
            if with_lse:
                lse_ref[0, pl.ds(r0, BAND), sl] = jnp.where(hi, lses[1], lses[0])
        return carry

    lax.fori_loop(0, nsub, body, 0)


def _band_attention(q, k, v, *, dil, tq, slopes, sinks=None, with_lse=True):
    b, r, cw = q.shape
    w = A_WIDTH
    c = cw // w
    assert r % tq == 0 and tq % BAND == 0
    nsub = tq // BAND
    cur = lambda bi, ci, ii: (bi, ii, ci)
    prev = lambda bi, ci, ii: (bi, jnp.maximum(ii * nsub - 1, 0), ci)
    in_specs = [pl.BlockSpec((1, tq, w), cur), pl.BlockSpec((1, tq, w), cur), pl.BlockSpec((1, BAND, w), prev),
                pl.BlockSpec((1, tq, w), cur), pl.BlockSpec((1, BAND, w), prev)]
    args = [q, k, k, v, v]
    if sinks is not None:
        in_specs = [pl.BlockSpec(memory_space=pltpu.SMEM)] + in_specs
        args = [sinks] + args
    n_out = 2 if with_lse else 1
    out = pl.pallas_call(
        functools.partial(_band_kernel, tq=tq, dil=dil, slopes=slopes, with_sink=sinks is not None,
                          with_lse=with_lse),
        grid=(b, c, r // tq),
        in_specs=in_specs,
        out_specs=[pl.BlockSpec((1, tq, w), cur)] * n_out,
        out_shape=[jax.ShapeDtypeStruct(q.shape, F32)] * n_out,
        scratch_shapes=[pltpu.VMEM((BAND + tq, w), BF16), pltpu.VMEM((BAND + tq, w), BF16),
                        pltpu.VMEM((A_HEADS, BAND, 2 * BAND), F32)],
        compiler_params=_cparams("arbitrary", "arbitrary", "arbitrary"),
        name=f"band_attn_d{dil}",
    )(*args)
    return out if with_lse else out[0]


def _prompt_mixers(qa, ka, va, qb, kbx, vbx, sinks_perm, b, l):
    w = A_WIDTH
    slopes_a = _alibi_slopes(A_HEADS)
    sb = _alibi_slopes(B_HEADS)
    slopes_b = [sb[(h % 2) * B_GROUP + h // 2] for h in range(B_HEADS)]
    branches = []
    for (_, dil) in A_PATTERNS:
        r = l // dil
        tq = min(r, 1024)
        view = lambda t: t.reshape(b, r, dil * w)
        o, lse = _band_attention(view(qa), view(ka), view(va), dil=dil, tq=tq, slopes=slopes_a)
        branches.append((o.reshape(b * l, w), lse.reshape(b * l, w)))
    view = lambda t: t.reshape(b, l, w)
    ob = _band_attention(view(qb), view(kbx), view(vbx), dil=1, tq=min(l, 1024), slopes=slopes_b,
                         sinks=sinks_perm, with_lse=False)
    return branches, ob.reshape(b * l, w)


MIXER_UNROLL = 8


def _pick(idx, values):
    out = jnp.float32(values[-1])
    for i in range(len(values) - 2, -1, -1):
        out = jnp.where(idx == i, jnp.float32(values[i]), out)
    return out


def _mixer_kernel(*refs, dils, slopes, with_sink, seq):
    if with_sink:
        sink_ref, refs = refs[0], refs[1:]
    q_ref, k_ref, v_ref, o_ref, kpad, vpad, bias_ref = refs[:7]
    ob_ref, lse_ref = refs[7:9] if len(dils) > 1 else (None, None)
    hp = pl.program_id(1)
    pad = BAND * max(dils)

    @pl.when((pl.program_id(0) == 0) & (hp == 0))
    def _():
        kpad[0:pad, :] = jnp.zeros((pad, LANES), F32)
        vpad[0:pad, :] = jnp.zeros((pad, LANES), F32)

    kpad[pad:pad + seq, :] = k_ref[0, 0]
    vpad[pad:pad + seq, :] = v_ref[0, 0]

    r = lax.broadcasted_iota(jnp.int32, (BAND, 2 * BAND), 0)
    c = lax.broadcasted_iota(jnp.int32, (BAND, 2 * BAND), 1)
    dist = r + BAND - c
    valid = (dist >= 0) & (dist <= BAND)
    for bi, dil in enumerate(dils):
        distf = (dist * dil).astype(F32)
        for e in range(2):
            slope = _pick(hp, [slopes[2 * g + e] for g in range(NSLAB)])
            bias_ref[bi, e] = jnp.where(valid, -slope * distf, NEG)

    lane = lax.broadcasted_iota(jnp.int32, (BAND, LANES), 1)
    hi = lane >= HEAD_DIM
    col = lax.broadcasted_iota(jnp.int32, (1, 2 * BAND), 1)
    prev_cols = (col < BAND).astype(F32)
    nt = (((1,), (1,)), ((), ()))

    for bi, dil in enumerate(dils):
        nblk = seq // (dil * BAND)

        def block(t, bi=bi, dil=dil, nblk=nblk):
            res = t // nblk
            j = t % nblk
            base = res + dil * BAND * j
            if dil == 1:
                base = pl.multiple_of(base, BAND)
                rows = pl.ds(base, BAND)
                win = pl.ds(pl.multiple_of(pad + base - BAND, BAND), 2 * BAND)
            else:
                rows = pl.ds(base, BAND, stride=dil)
                win = pl.ds(pad + base - dil * BAND, 2 * BAND, stride=dil)
            qs = q_ref[0, 0, rows, :]
            kw = kpad[win, :].astype(BF16)
            vw = vpad[win, :].astype(BF16)
            pen = jnp.where(j == 0, NEG, 0.0) * prev_cols
            outs, lses = [], []
            for e in range(2):
                qm = jnp.where(hi == bool(e), qs, 0.0).astype(BF16)
                s = lax.dot_general(qm, kw, nt, preferred_element_type=F32)
                s = s * SCALE + bias_ref[bi, e] + pen
                m = jnp.max(s, axis=-1, keepdims=True)
                if with_sink:
                    sink = sink_ref[2 * hp + e]
                    m = jnp.maximum(m, sink)
                p = jnp.exp(s - m)
                den = jnp.sum(p, axis=-1, keepdims=True)
                if with_sink:
                    den = den + jnp.exp(sink - m)
                o = jnp.dot(p.astype(BF16), vw, preferred_element_type=F32)
                outs.append(o / den)
                lses.append(m + jnp.log(den))
            if len(dils) == 1:
                o_ref[0, 0, rows, :] = jnp.where(hi, outs[1], outs[0])
            else:
                ob_ref[bi, rows, :] = jnp.where(hi, outs[1], outs[0])
                lse_ref[bi, rows, :] = jnp.where(hi, lses[1], lses[0])

        def blocks(t, carry, block=block):
            for uu in range(MIXER_UNROLL):
                block(t * MIXER_UNROLL + uu)
            return carry

        assert (dil * nblk) % MIXER_UNROLL == 0
        lax.fori_loop(0, dil * nblk // MIXER_UNROLL, blocks, 0)

    if len(dils) > 1:
        def merge(t, carry):
            rows = pl.ds(pl.multiple_of(t * BAND, BAND), BAND)
            ls = [lse_ref[bi, rows, :] for bi in range(len(dils))]
            mx = functools.reduce(jnp.maximum, ls)
            ws = [jnp.exp(l - mx) for l in ls]
            num = sum(w * ob_ref[bi, rows, :] for bi, w in enumerate(ws))
            o_ref[0, 0, rows, :] = num / sum(ws)
            return carry

        lax.fori_loop(0, seq // BAND, merge, 0)


def _mixer(q4, k4, v4, *, dils, slopes, sinks=None):
    nslab, b, l, _ = q4.shape
    assert l % (BAND * max(dils)) == 0
    shared = k4.shape[0] == 1
    qmap = lambda bi, hp: (hp, bi, 0, 0)
    kmap = (lambda bi, hp: (0, bi, 0, 0)) if shared else qmap
    blk = (1, 1, l, LANES)
    in_specs = [pl.BlockSpec(blk, qmap), pl.BlockSpec(blk, kmap), pl.BlockSpec(blk, kmap)]
    args = [q4, k4, v4]
    if sinks is not None:
        in_specs = [pl.BlockSpec(memory_space=pltpu.SMEM)] + in_specs
        args = [sinks] + args
    pad = BAND * max(dils)
    scratch = [pltpu.VMEM((pad + l, LANES), F32), pltpu.VMEM((pad + l, LANES), F32),
               pltpu.VMEM((len(dils), 2, BAND, 2 * BAND), F32)]
    if len(dils) > 1:
        scratch += [pltpu.VMEM((len(dils), l, LANES), F32), pltpu.VMEM((len(dils), l, LANES), F32)]
    return pl.pallas_call(
        functools.partial(_mixer_kernel, dils=tuple(dils), slopes=slopes, with_sink=sinks is not None, seq=l),
        grid=(b, nslab),
        in_specs=in_specs,
        out_specs=pl.BlockSpec(blk, qmap),
        out_shape=jax.ShapeDtypeStruct(q4.shape, F32),
        scratch_shapes=scratch,
        compiler_params=_cparams("arbitrary", "arbitrary"),
        name="mixer_b" if shared else "mixer_a",
    )(*args)


def _qb_perm():
    idx = []
    for g in range(B_GROUP):
        for kv in range(B_KV_HEADS):
            base = (kv * B_GROUP + g) * HEAD_DIM
            idx.extend(range(base, base + HEAD_DIM))
    return jnp.asarray(idx, jnp.int32)


NEW = 8
HQ = A_HEADS * NEW


def _block_diag_rows(q8, width):
    rep = jnp.concatenate([q8] * A_HEADS, axis=0)
    r = lax.broadcasted_iota(jnp.int32, (HQ, width), 0)
    c = lax.broadcasted_iota(jnp.int32, (HQ, width), 1)
    return jnp.where(r // NEW == c // HEAD_DIM, rep, 0.0)


def _sample_a_kernel(q_ref, kn_ref, vn_ref, ck_ref, cv_ref, nk_ref, nv_ref, o_ref,
                     b1_ref, b4_ref, b16_ref, bn_ref, knb, vnb, *, la, slopes):
    spans = [w for (w, _) in A_PATTERNS]
    bias_refs = [b1_ref, b4_ref, b16_ref]

    @pl.when(pl.program_id(0) == 0)
    def _():
        for bi, (wdw, dil) in enumerate(A_PATTERNS):
            ncol = spans[bi]
            r = lax.broadcasted_iota(jnp.int32, (HQ, ncol), 0)
            c = lax.broadcasted_iota(jnp.int32, (HQ, ncol), 1)
            dist = (r % NEW) + ncol - c
            valid = (dist <= wdw) & (dist % dil == 0)
            slope = jnp.zeros((HQ, ncol), F32)
            for h in range(A_HEADS):
                slope = jnp.where(r // NEW == h, slopes[h], slope)
            bias_refs[bi][...] = jnp.where(valid, -slope * dist.astype(F32), NEG)
            rn = lax.broadcasted_iota(jnp.int32, (HQ, LANES), 0)
            cn = lax.broadcasted_iota(jnp.int32, (HQ, LANES), 1)
            dn = (rn % NEW) - cn
            vn_ok = (dn >= 0) & (dn % dil == 0) & (cn < NEW)
            sl = jnp.zeros((HQ, LANES), F32)
            for h in range(A_HEADS):
                sl = jnp.where(rn // NEW == h, slopes[h], sl)
            bn_ref[bi] = jnp.where(vn_ok, -sl * dn.astype(F32), NEG)
        knb[...] = jnp.zeros_like(knb)
        vnb[...] = jnp.zeros_like(vnb)

    kn = kn_ref[0]
    vn = vn_ref[0]
    nk_ref[0, 0:la - NEW, :] = ck_ref[0, NEW:la, :]
    nk_ref[0, la - NEW:la, :] = kn
    nv_ref[0, 0:la - NEW, :] = cv_ref[0, NEW:la, :]
    nv_ref[0, la - NEW:la, :] = vn
    knb[0:NEW, :] = kn
    vnb[0:NEW, :] = vn

    qbd = _block_diag_rows(q_ref[0], A_WIDTH).astype(BF16)
    nt = (((1,), (1,)), ((), ()))
    s_new = lax.dot_general(qbd, knb[...].astype(BF16), nt, preferred_element_type=F32) * SCALE
    ms, dens, nums = [], [], []
    for bi in range(len(A_PATTERNS)):
        ncol = spans[bi]
        kc = ck_ref[0, la - ncol:la, :].astype(BF16)
        vc = cv_ref[0, la - ncol:la, :].astype(BF16)
        sc = lax.dot_general(qbd, kc, nt, preferred_element_type=F32) * SCALE + bias_refs[bi][...]
        sn = s_new + bn_ref[bi]
        m = jnp.maximum(jnp.max(sc, axis=-1, keepdims=True), jnp.max(sn, axis=-1, keepdims=True))
        pc = jnp.exp(sc - m)
        pn = jnp.exp(sn - m)
        dens.append(jnp.sum(pc, axis=-1, keepdims=True) + jnp.sum(pn, axis=-1, keepdims=True))
        nums.append(jnp.dot(pc.astype(BF16), vc, preferred_element_type=F32)
                    + jnp.dot(pn.astype(BF16), vnb[...].astype(BF16), preferred_element_type=F32))
        ms.append(m)
    mx = jnp.maximum(jnp.maximum(ms[0], ms[1]), ms[2])
    den = 0.0
    num = 0.0
    for bi in range(len(A_PATTERNS)):
        wt = jnp.exp(ms[bi] - mx)
        den = den + dens[bi] * wt
        num = num + nums[bi] * wt
    out = _block_diag_rows_keep(num / den, A_WIDTH)
    o_ref[0] = sum(out[h * NEW:(h + 1) * NEW, :] for h in range(A_HEADS))


def _block_diag_rows_keep(x, width):
    r = lax.broadcasted_iota(jnp.int32, (HQ, width), 0)
    c = lax.broadcasted_iota(jnp.int32, (HQ, width), 1)
    return jnp.where(r // NEW == c // HEAD_DIM, x, 0.0)


def _sample_a(q, kn, vn, ck, cv):
    s, la, w = ck.shape
    assert la >= max(wd for (wd, _) in A_PATTERNS)
    seq = lambda i: (i, 0, 0)
    small = pl.BlockSpec((1, NEW, w), seq)
    big = pl.BlockSpec((1, la, w), seq)
    spans = [wd for (wd, _) in A_PATTERNS]
    return pl.pallas_call(
        functools.partial(_sample_a_kernel, la=la, slopes=_alibi_slopes(A_HEADS)),
        grid=(s,),
        in_specs=[small, small, small, big, big],
        out_specs=[big, big, small],
        out_shape=[jax.ShapeDtypeStruct(ck.shape, F32), jax.ShapeDtypeStruct(cv.shape, F32),
                   jax.ShapeDtypeStruct(q.shape, F32)],
        scratch_shapes=[pltpu.VMEM((HQ, spans[0]), F32), pltpu.VMEM((HQ, spans[1]), F32),
                        pltpu.VMEM((HQ, spans[2]), F32), pltpu.VMEM((len(A_PATTERNS), HQ, LANES), F32),
                        pltpu.VMEM((LANES, w), F32), pltpu.VMEM((LANES, w), F32)],
        compiler_params=_cparams("arbitrary"),
        name="sample_attn_a",
    )(q, kn, vn, ck, cv)


SAMPLE_CHUNK = 512


def _sample_a5_kernel(q_ref, kn_ref, vn_ref, ck_ref, cv_ref, nk_hbm, nv_hbm, o_ref,
                      b16_ref, b4_ref, b1_ref, bn_ref, sem, *, la, slopes):
    b = pl.program_id(0)
    copies = [
        pltpu.make_async_copy(ck_ref.at[0, 0, pl.ds(NEW, la - NEW)], nk_hbm.at[0, b, pl.ds(0, la - NEW)], sem.at[0]),
        pltpu.make_async_copy(cv_ref.at[0, 0, pl.ds(NEW, la - NEW)], nv_hbm.at[0, b, pl.ds(0, la - NEW)], sem.at[1]),
        pltpu.make_async_copy(kn_ref.at[0], nk_hbm.at[0, b, pl.ds(la - NEW, NEW)], sem.at[2]),
        pltpu.make_async_copy(vn_ref.at[0], nv_hbm.at[0, b, pl.ds(la - NEW, NEW)], sem.at[3]),
    ]
    for cp in copies:
        cp.start()

    spans = [wd for (wd, _) in A_PATTERNS]
    bias_refs = [b1_ref, b4_ref, b16_ref]

    @pl.when(b == 0)
    def _():
        for bi, (wdw, dil) in enumerate(A_PATTERNS):
            ncol = spans[bi] * A_HEADS
            r = lax.broadcasted_iota(jnp.int32, (HQ, ncol), 0)
            c = lax.broadcasted_iota(jnp.int32, (HQ, ncol), 1)
            dist = (r % NEW) + spans[bi] - c // A_HEADS
            valid = (dist <= wdw) & (dist % dil == 0) & (c % A_HEADS == r // NEW)
            slope = jnp.zeros((HQ, ncol), F32)
            for h in range(A_HEADS):
                slope = jnp.where(r // NEW == h, slopes[h], slope)
            bias_refs[bi][...] = jnp.where(valid, -slope * dist.astype(F32), NEG)
            rn = lax.broadcasted_iota(jnp.int32, (HQ, HQ), 0)
            cn = lax.broadcasted_iota(jnp.int32, (HQ, HQ), 1)
            dn = (rn % NEW) - cn // A_HEADS
            ok = (dn >= 0) & (dn % dil == 0) & (cn % A_HEADS == rn // NEW)
            sl = jnp.zeros((HQ, HQ), F32)
            for h in range(A_HEADS):
                sl = jnp.where(rn // NEW == h, slopes[h], sl)
            bn_ref[bi] = jnp.where(ok, -sl * dn.astype(F32), NEG)

    nt = (((1,), (1,)), ((), ()))
    q = q_ref[0].astype(BF16)

    def scores(k3):
        k2 = k3.reshape(k3.shape[0] * A_HEADS, HEAD_DIM).astype(BF16)
        return lax.dot_general(q, k2, nt, preferred_element_type=F32) * SCALE

    def flat_v(v3):
        return v3.reshape(v3.shape[0] * A_HEADS, HEAD_DIM).astype(BF16)

    def fold(state, s, v2):
        m_new = jnp.max(s, axis=-1, keepdims=True)
        if state is not None:
            m_old, den, num = state
            m_new = jnp.maximum(m_new, m_old)
        p = jnp.exp(s - m_new)
        d_new = jnp.sum(p, axis=-1, keepdims=True)
        n_new = jnp.dot(p.astype(BF16), v2, preferred_element_type=F32)
        if state is not None:
            alpha = jnp.exp(m_old - m_new)
            d_new = d_new + den * alpha
            n_new = n_new + num * alpha
        return m_new, d_new, n_new

    states = [None, None, None]
    nchunk = la // SAMPLE_CHUNK
    ch_cols = SAMPLE_CHUNK * A_HEADS
    for ci in range(nchunk):
        rows = slice(ci * SAMPLE_CHUNK, (ci + 1) * SAMPLE_CHUNK)
        s = scores(ck_ref[0, 0, rows])
        v2 = flat_v(cv_ref[0, 0, rows])
        states[2] = fold(states[2], s + b16_ref[:, ci * ch_cols:(ci + 1) * ch_cols], v2)
        if ci == nchunk - 1:
            for bi in (0, 1):
                ncol = spans[bi] * A_HEADS
                states[bi] = fold(states[bi], s[:, ch_cols - ncol:] + bias_refs[bi][...], v2[ch_cols - ncol:, :])
    s_new = scores(kn_ref[0])
    v_new = flat_v(vn_ref[0])
    for bi in range(len(A_PATTERNS)):
        states[bi] = fold(states[bi], s_new + bn_ref[bi], v_new)
    mx = functools.reduce(jnp.maximum, [st[0] for st in states])
    den = 0.0
    num = 0.0
    for (m, dd, nn) in states:
        wt = jnp.exp(m - mx)
        den = den + dd * wt
        num = num + nn * wt
    o_ref[0] = num / den

    for cp in copies:
        cp.wait()


def _sample_a5(q64, kn, vn, ck5, cv5):
    _, s, la, nh, hd = ck5.shape
    spans = [wd for (wd, _) in A_PATTERNS]
    assert la == max(spans) and la % SAMPLE_CHUNK == 0 and spans[1] == SAMPLE_CHUNK and nh == A_HEADS
    seq3 = lambda i: (i, 0, 0)
    seq4 = lambda i: (i, 0, 0, 0)
    big = pl.BlockSpec((1, 1, la, nh, hd), lambda i: (0, i, 0, 0, 0))
    anyspec = pl.BlockSpec(memory_space=pl.ANY)
    return pl.pallas_call(
        functools.partial(_sample_a5_kernel, la=la, slopes=_alibi_slopes(A_HEADS)),
        grid=(s,),
        in_specs=[pl.BlockSpec((1, HQ, hd), seq3), pl.BlockSpec((1, NEW, nh, hd), seq4),
                  pl.BlockSpec((1, NEW, nh, hd), seq4), big, big],
        out_specs=[anyspec, anyspec, pl.BlockSpec((1, HQ, hd), seq3)],
        out_shape=[jax.ShapeDtypeStruct(ck5.shape, F32), jax.ShapeDtypeStruct(cv5.shape, F32),
                   jax.ShapeDtypeStruct(q64.shape, F32)],
        scratch_shapes=[pltpu.VMEM((HQ, spans[2] * nh), F32), pltpu.VMEM((HQ, spans[1] * nh), F32),
                        pltpu.VMEM((HQ, spans[0] * nh), F32), pltpu.VMEM((len(A_PATTERNS), HQ, HQ), F32),
                        pltpu.SemaphoreType.DMA((4,))],
        compiler_params=_cparams("arbitrary"),
        name="sample_attn_a5",
    )(q64, kn, vn, ck5, cv5)


ROLL_ROWS = 64


def _sample_at_kernel(q_ref, kn_ref, vn_ref, ck_ref, cv_ref, nk_ref, nv_ref, o_ref, bias_ref, bn_ref, knb, vnb,
                      *, la, slopes):
    w = A_WIDTH
    nbr = len(A_PATTERNS)

    @pl.when(pl.program_id(0) == 0)
    def _():
        r = lax.broadcasted_iota(jnp.int32, (HQ, la), 0)
        c = lax.broadcasted_iota(jnp.int32, (HQ, la), 1)
        dist = (r % NEW) + la - c
        slope = jnp.zeros((HQ, la), F32)
        rn = lax.broadcasted_iota(jnp.int32, (HQ, LANES), 0)
        cn = lax.broadcasted_iota(jnp.int32, (HQ, LANES), 1)
        dn = (rn % NEW) - cn
        sl = jnp.zeros((HQ, LANES), F32)
        for h in range(A_HEADS):
            slope = jnp.where(r // NEW == h, slopes[h], slope)
            sl = jnp.where(rn // NEW == h, slopes[h], sl)
        for bi, (wdw, dil) in enumerate(A_PATTERNS):
            bias_ref[bi] = jnp.where((dist <= wdw) & (dist % dil == 0), -slope * dist.astype(F32), NEG)
            bn_ref[bi] = jnp.where((dn >= 0) & (dn % dil == 0) & (cn < NEW), -sl * dn.astype(F32), NEG)
        knb[...] = jnp.zeros_like(knb)
        vnb[...] = jnp.zeros_like(vnb)

    knb[:, 0:NEW] = kn_ref[0]
    vnb[:, 0:NEW] = vn_ref[0]

    qbd = _block_diag_rows(q_ref[0], w).astype(BF16)
    nt = (((1,), (1,)), ((), ()))
    s_c = jnp.dot(qbd, ck_ref[0].astype(BF16), preferred_element_type=F32) * SCALE
    s_n = jnp.dot(qbd, knb[...].astype(BF16), preferred_element_type=F32) * SCALE
    ms, dens, pcs, pns = [], [], [], []
    for bi in range(nbr):
        sc = s_c + bias_ref[bi]
        sn = s_n + bn_ref[bi]
        m = jnp.maximum(jnp.max(sc, axis=-1, keepdims=True), jnp.max(sn, axis=-1, keepdims=True))
        pc = jnp.exp(sc - m)
        pn = jnp.exp(sn - m)
        dens.append(jnp.sum(pc, axis=-1, keepdims=True) + jnp.sum(pn, axis=-1, keepdims=True))
        ms.append(m)
        pcs.append(pc.astype(BF16))
        pns.append(pn.astype(BF16))
    nums = (lax.dot_general(jnp.concatenate(pcs, axis=0), cv_ref[0].astype(BF16), nt, preferred_element_type=F32)
            + lax.dot_general(jnp.concatenate(pns, axis=0), vnb[...].astype(BF16), nt, preferred_element_type=F32))
    mx = functools.reduce(jnp.maximum, ms)
    den = 0.0
    num = 0.0
    for bi in range(nbr):
        wt = jnp.exp(ms[bi] - mx)
        den = den + dens[bi] * wt
        num = num + nums[bi * HQ:(bi + 1) * HQ, :] * wt
    out = _block_diag_rows_keep(num / den, w)
    o_ref[0] = sum(out[h * NEW:(h + 1) * NEW, :] for h in range(A_HEADS))

    lane = lax.broadcasted_iota(jnp.int32, (ROLL_ROWS, LANES), 1)
    tail = lane >= LANES - NEW
    for src, newb, dst in ((ck_ref, knb, nk_ref), (cv_ref, vnb, nv_ref)):
        for r0 in range(0, w, ROLL_ROWS):
            rows = slice(r0, r0 + ROLL_ROWS)
            rolled = pltpu.roll(src[0, rows, :], la - NEW, 1)
            dst[0, rows, 0:la - LANES] = rolled[:, 0:la - LANES]
            fresh = pltpu.roll(newb[rows, :], LANES - NEW, 1)
            dst[0, rows, la - LANES:la] = jnp.where(tail, fresh, rolled[:, la - LANES:la])


def _sample_at(q, kn_t, vn_t, ck_t, cv_t):
    s, w, la = ck_t.shape
    assert la >= max(wd for (wd, _) in A_PATTERNS) and la % LANES == 0
    seq = lambda i: (i, 0, 0)
    big = pl.BlockSpec((1, w, la), seq)
    new = pl.BlockSpec((1, w, NEW), seq)
    qspec = pl.BlockSpec((1, NEW, w), seq)
    nbr = len(A_PATTERNS)
    return pl.pallas_call(
        functools.partial(_sample_at_kernel, la=la, slopes=_alibi_slopes(A_HEADS)),
        grid=(s,),
        in_specs=[qspec, new, new, big, big],
        out_specs=[big, big, qspec],
        out_shape=[jax.ShapeDtypeStruct(ck_t.shape, F32), jax.ShapeDtypeStruct(cv_t.shape, F32),
                   jax.ShapeDtypeStruct(q.shape, F32)],
        scratch_shapes=[pltpu.VMEM((nbr, HQ, la), F32), pltpu.VMEM((nbr, HQ, LANES), F32),
                        pltpu.VMEM((w, LANES), F32), pltpu.VMEM((w, LANES), F32)],
        compiler_params=_cparams("arbitrary"),
        name="sample_attn_at",
    )(q, kn_t, vn_t, ck_t, cv_t)


def _sample_b_kernel(sink_ref, q_ref, kn_ref, vn_ref, ck_ref, cv_ref, nk_ref, nv_ref, o_ref,
                     bc_ref, bn_ref, sk_ref, knb, vnb, *, lb, nseq, slopes):
    @pl.when(pl.program_id(0) == 0)
    def _():
        r = lax.broadcasted_iota(jnp.int32, (HQ, lb), 0)
        c = lax.broadcasted_iota(jnp.int32, (HQ, lb), 1)
        dist = (r % NEW) + lb - c
        slope = jnp.zeros((HQ, lb), F32)
        sink = jnp.zeros((HQ, LANES), F32)
        rs = lax.broadcasted_iota(jnp.int32, (HQ, LANES), 0)
        for h in range(B_HEADS):
            slope = jnp.where(r // NEW == h, slopes[h], slope)
            sink = jnp.where(rs // NEW == h, sink_ref[h], sink)
        bc_ref[...] = jnp.where(dist <= BAND, -slope * dist.astype(F32), NEG)
        cn = lax.broadcasted_iota(jnp.int32, (HQ, LANES), 1)
        dn = (rs % NEW) - cn
        sl = jnp.zeros((HQ, LANES), F32)
        for h in range(B_HEADS):
            sl = jnp.where(rs // NEW == h, slopes[h], sl)
        bn_ref[...] = jnp.where((dn >= 0) & (cn < NEW), -sl * dn.astype(F32), NEG)
        sk_ref[...] = sink
        knb[...] = jnp.zeros_like(knb)
        vnb[...] = jnp.zeros_like(vnb)

    nt = (((1,), (1,)), ((), ()))
    rr = lax.broadcasted_iota(jnp.int32, (HQ, LANES), 0)
    cc = lax.broadcasted_iota(jnp.int32, (HQ, LANES), 1)
    own = ((rr // NEW) % B_KV_HEADS) == (cc // HEAD_DIM)
    sink = sk_ref[:, 0:1]
    tail = cc >= LANES - NEW
    for t in range(nseq):
        knb[:, 0:NEW] = kn_ref[t]
        vnb[:, 0:NEW] = vn_ref[t]
        for src, newb, dst in ((ck_ref, knb, nk_ref), (cv_ref, vnb, nv_ref)):
            for r0 in range(0, LANES, HQ):
                rows = slice(r0, r0 + HQ)
                dst[t, rows, :] = jnp.where(tail, pltpu.roll(newb[rows, :], LANES - NEW, 1),
                                            pltpu.roll(src[t, rows, :], lb - NEW, 1))
        q = q_ref[t]
        rows = []
        for g in range(B_GROUP):
            for kv in range(B_KV_HEADS):
                rows.append(q[:, g * LANES:(g + 1) * LANES])
        qbd = jnp.where(own, jnp.concatenate(rows, axis=0), 0.0).astype(BF16)
        sc = jnp.dot(qbd, ck_ref[t].astype(BF16), preferred_element_type=F32) * SCALE + bc_ref[...]
        sn = jnp.dot(qbd, knb[...].astype(BF16), preferred_element_type=F32) * SCALE + bn_ref[...]
        m = jnp.maximum(jnp.maximum(jnp.max(sc, axis=-1, keepdims=True), jnp.max(sn, axis=-1, keepdims=True)), sink)
        pc = jnp.exp(sc - m)
        pn = jnp.exp(sn - m)
        den = jnp.sum(pc, axis=-1, keepdims=True) + jnp.sum(pn, axis=-1, keepdims=True) + jnp.exp(sink - m)
        o = (lax.dot_general(pc.astype(BF16), cv_ref[t].astype(BF16), nt, preferred_element_type=F32)
             + lax.dot_general(pn.astype(BF16), vnb[...].astype(BF16), nt, preferred_element_type=F32)) / den
        o = jnp.where(own, o, 0.0)
        for g in range(B_GROUP):
            base = g * B_KV_HEADS * NEW
            o_ref[t, :, g * LANES:(g + 1) * LANES] = o[base:base + NEW, :] + o[base + NEW:base + 2 * NEW, :]


def _sample_b(sinks_perm, slopes_perm, q, kn, vn, ck, cv, nseq=8):
    s, kw, lb = ck.shape
    assert lb == BAND and kw == LANES and lb == LANES and s % nseq == 0
    seq = lambda i: (i, 0, 0)
    return pl.pallas_call(
        functools.partial(_sample_b_kernel, lb=lb, nseq=nseq, slopes=slopes_perm),
        grid=(s // nseq,),
        in_specs=[pl.BlockSpec(memory_space=pltpu.SMEM), pl.BlockSpec((nseq, NEW, A_WIDTH), seq),
                  pl.BlockSpec((nseq, kw, NEW), seq), pl.BlockSpec((nseq, kw, NEW), seq),
                  pl.BlockSpec((nseq, kw, lb), seq), pl.BlockSpec((nseq, kw, lb), seq)],
        out_specs=[pl.BlockSpec((nseq, kw, lb), seq), pl.BlockSpec((nseq, kw, lb), seq),
                   pl.BlockSpec((nseq, NEW, A_WIDTH), seq)],
        out_shape=[jax.ShapeDtypeStruct(ck.shape, F32), jax.ShapeDtypeStruct(cv.shape, F32),
                   jax.ShapeDtypeStruct(q.shape, F32)],
        scratch_shapes=[pltpu.VMEM((HQ, lb), F32), pltpu.VMEM((HQ, LANES), F32), pltpu.VMEM((HQ, LANES), F32),
                        pltpu.VMEM((LANES, kw), F32), pltpu.VMEM((LANES, kw), F32)],
        compiler_params=_cparams("arbitrary"),
        name="sample_attn_b",
    )(sinks_perm, q, kn, vn, ck, cv)


def _outproj_kernel(x_ref, oa_ref, ob_ref, wa_ref, wb_ref, ht_ref):
    oa = jnp.concatenate([oa_ref[s] for s in range(NSLAB)], axis=1).astype(BF16)
    ob = jnp.concatenate([ob_ref[s] for s in range(NSLAB)], axis=1).astype(BF16)
    h = (x_ref[...] + jnp.dot(oa, wa_ref[...], preferred_element_type=F32)
         + jnp.dot(ob, wb_ref[...], preferred_element_type=F32))
    ht_ref[...] = h.T


def _outproj(x2d, oa4, ob4, wa, wb, tm):
    n, d = x2d.shape
    w = A_WIDTH
    row = lambda i: (i, 0)
    const = lambda i: (0, 0)
    slab = pl.BlockSpec((NSLAB, tm, LANES), lambda i: (0, i, 0))
    return pl.pallas_call(
        _outproj_kernel,
        grid=(n // tm,),
        in_specs=[pl.BlockSpec((tm, d), row), slab, slab, pl.BlockSpec((w, d), const), pl.BlockSpec((w, d), const)],
        out_specs=pl.BlockSpec((d, tm), lambda i: (0, i)),
        out_shape=jax.ShapeDtypeStruct((d, n), F32),
        compiler_params=_cparams("parallel"),
        name="out_proj",
    )(x2d, oa4, ob4, wa, wb)


BIG = 3.0e38


SUBLANES = 8


def _merge_exchange(n):
    pairs = []
    t = max(1, math.ceil(math.log2(n)))
    p = 1 << (t - 1)
    while p > 0:
        q, r, d = 1 << (t - 1), 0, p
        while d > 0:
            pairs.extend((i, i + d) for i in range(n - d) if (i & p) == r)
            d, q, r = q - p, q >> 1, p
        p >>= 1
    return pairs


def _vmax(a, b):
    if a is None:
        return b
    if b is None:
        return a
    return jnp.maximum(a, b)


def _vmin(a, b):
    if a is None or b is None:
        return None
    return jnp.minimum(a, b)


def _exchange(x, i, j):
    x[i], x[j] = _vmax(x[i], x[j]), _vmin(x[i], x[j])


def _top16(tiles):
    x = list(tiles) + [None] * (PEER_TOPK - len(tiles))
    for (i, j) in _merge_exchange(len(tiles)):
        _exchange(x, i, j)
    for shift in (4, 2, 1):
        y = [None if v is None else pltpu.roll(v, shift, 0) for v in x]
        x = [_vmax(x[k], y[PEER_TOPK - 1 - k]) for k in range(PEER_TOPK)]
        for d in (8, 4, 2, 1):
            for i in range(PEER_TOPK):
                if not i & d:
                    _exchange(x, i, i + d)
    return x


def _count_prefix(pred, vals):
    sel = jnp.where
    c16 = pred(vals[15])
    c8 = pred(vals[7])
    c4 = pred(sel(c8, vals[11], vals[3]))
    c2 = pred(sel(c8, sel(c4, vals[13], vals[9]), sel(c4, vals[5], vals[1])))
    c1 = pred(sel(c8, sel(c4, sel(c2, vals[14], vals[12]), sel(c2, vals[10], vals[8])),
                  sel(c4, sel(c2, vals[6], vals[4]), sel(c2, vals[2], vals[0]))))
    lo = sel(c8, 8.0, 0.0) + sel(c4, 4.0, 0.0) + sel(c2, 2.0, 0.0) + sel(c1, 1.0, 0.0)
    return sel(c16, float(PEER_TOPK), lo)


def _rows_sum(x):
    for shift in (4, 2, 1):
        x = x + pltpu.roll(x, shift, 0)
    return x


def _peer_select_kernel(ht_ref, g_ref, wq_ref, keys_ref,
                        hn_ref, r1_ref, w1_ref, cnt_ref, w0_ref, s_ref, *, nk, tn):
    h = ht_ref[...]
    ms = jnp.mean(h * h, axis=0, keepdims=True)
    hn = (h * lax.rsqrt(ms + NORM_EPS) * g_ref[...]).astype(BF16)
    hn_ref[...] = hn
    q = jnp.dot(wq_ref[...], hn, preferred_element_type=F32)
    half = q.shape[0] // (2 * PEER_HEADS)
    for k in range(2 * PEER_HEADS):
        qk = q[k * half:(k + 1) * half, :].astype(BF16)
        s_ref[k] = jnp.dot(keys_ref[k], qk, preferred_element_type=F32)
    nchunk = tn // LANES
    ntile = nk // SUBLANES
    sub = lax.broadcasted_iota(jnp.int32, (SUBLANES, LANES), 0)

    def pack(vals):
        out = vals[-1]
        for r in range(len(vals) - 2, -1, -1):
            out = jnp.where(sub == r, vals[r], out)
        return out

    def head_chunk(t, carry):
        hd = t // nchunk
        lanes = pl.ds(pl.multiple_of((t % nchunk) * LANES, LANES), LANES)
        rows = [slice(k * SUBLANES, (k + 1) * SUBLANES) for k in range(ntile)]
        s0 = [s_ref[2 * hd, r, lanes] for r in rows]
        s1 = [s_ref[2 * hd + 1, r, lanes] for r in rows]
        v0 = _top16(s0)
        v1 = _top16(s1)
        v1lo, v1hi, v0hi = pack(v1[0:8]), pack(v1[8:16]), pack(v0[8:16])
        cands = [v0[0] + v1lo, v0[0] + v1hi] + [v0[a] + v1lo for a in range(1, 8)] + [v0hi + v1[0]]
        best = _top16(cands)
        top, tau = best[0], best[PEER_TOPK - 1]
        z = _rows_sum(sum(jnp.where(c >= tau, jnp.exp(c - top), 0.0) for c in cands))
        inv_z = 1.0 / z
        for m in range(ntile // 2):
            cnt, rk1 = [], []
            for k in (2 * m, 2 * m + 1):
                cnt.append(_count_prefix(lambda t, k=k: s0[k] + t >= tau, v1))
                rk1.append(_count_prefix(lambda t, k=k: t > s1[k], v1))
            pair = slice(2 * m * SUBLANES, (2 * m + 2) * SUBLANES)
            both = lambda f: jnp.concatenate([f(2 * m), f(2 * m + 1)], axis=0)
            cnt_ref[hd, pair, lanes] = jnp.concatenate(cnt, axis=0)
            w0_ref[hd, pair, lanes] = both(lambda k: jnp.exp(s0[k] - v0[0]) * inv_z)
            r1_ref[hd, pair, lanes] = jnp.concatenate(rk1, axis=0).astype(BF16)
            w1_ref[hd, pair, lanes] = both(lambda k: jnp.exp(s1[k] - v1[0])).astype(BF16)
        return carry

    lax.fori_loop(0, PEER_HEADS * nchunk, head_chunk, 0)


def _peer_select(ht, g_col, wq_t, keys, tn):
    d, n = ht.shape
    assert tn % LANES == 0 and n % tn == 0
    nslab, nk, half = keys.shape
    tok = lambda i: (0, i)
    tok3 = lambda i: (0, 0, i)
    stat = lambda dt: jax.ShapeDtypeStruct((PEER_HEADS, nk, n), dt)
    return pl.pallas_call(
        functools.partial(_peer_select_kernel, nk=nk, tn=tn),
        grid=(n // tn,),
        in_specs=[pl.BlockSpec((d, tn), tok), pl.BlockSpec((d, 1), lambda i: (0, 0)),
                  pl.BlockSpec(wq_t.shape, lambda i: (0, 0)), pl.BlockSpec(keys.shape, lambda i: (0, 0, 0))],
        out_specs=[pl.BlockSpec((d, tn), tok)] + [pl.BlockSpec((PEER_HEADS, nk, tn), tok3)] * 4,
        out_shape=[jax.ShapeDtypeStruct((d, n), BF16), stat(BF16), stat(BF16), stat(F32), stat(F32)],
        scratch_shapes=[pltpu.VMEM((nslab, nk, tn), F32)],
        compiler_params=_cparams("parallel"),
        name="peer_select",
    )(ht, g_col, wq_t, keys)


def _gelu(x):
    return 0.5 * x * (1.0 + lax.erf(x * (2.0 ** -0.5)))


def _peer_dense_kernel(hn_ref, u_ref, vt_ref, r1_ref, w1_ref, cnt_ref, w0_ref, ht_ref, y_ref,
                       acc_ref, a0_ref, a1_ref, c0_ref, c1_ref, *, nk, te, tn, lc, rb, ne, nsteps):
    s = pl.program_id(0)

    @pl.when(s == 0)
    def _():
        for r in (a0_ref, a1_ref, c0_ref, c1_ref, acc_ref):
            r[...] = jnp.zeros_like(r)

    e = jnp.clip(s - 1, 0, nsteps - 1) % ne
    p2 = s - 2
    first = (jnp.maximum(p2, 0) % ne) == 0
    nslab = te // nk

    def stages(a_new, a_prev, c_new, c_prev):
        a_new[...] = jnp.dot(u_ref[...], hn_ref[...], preferred_element_type=F32)
        for ii in range(nslab):
            i = e * nslab + ii
            cnt_rows = [cnt_ref[hd, pl.ds(i, 1), :].astype(BF16) for hd in range(PEER_HEADS)]
            w0_rows = [w0_ref[hd, pl.ds(i, 1), :].astype(BF16) for hd in range(PEER_HEADS)]
            for c in range(tn // lc):
                ls = slice(c * lc, (c + 1) * lc)
                for jb in range(nk // rb):
                    js = slice(jb * rb, (jb + 1) * rb)
                    rs = slice(ii * nk + jb * rb, ii * nk + (jb + 1) * rb)
                    g = None
                    for hd in range(PEER_HEADS):
                        term = jnp.where(r1_ref[hd, js, ls] < cnt_rows[hd][:, ls], w1_ref[hd, js, ls],
                                         jnp.zeros((), BF16)) * w0_rows[hd][:, ls]
                        g = term if g is None else g + term
                    c_new[rs, ls] = g * _gelu(a_prev[rs, ls]).astype(BF16)
        contrib = jnp.dot(vt_ref[...], c_prev[...], preferred_element_type=F32)
        acc_ref[...] = jnp.where(first, contrib, acc_ref[...] + contrib)

    @pl.when(s % 2 == 0)
    def _():
        stages(a0_ref, a1_ref, c1_ref, c0_ref)

    @pl.when(s % 2 == 1)
    def _():
        stages(a1_ref, a0_ref, c0_ref, c1_ref)

    @pl.when((p2 >= 0) & (p2 % ne == ne - 1))
    def _():
        y_ref[...] = (ht_ref[...] + acc_ref[...]).T


def _peer_dense(hn, u, vt, r1, w1, cnt, w0, ht, *, tn, te, lc, rb):
    d, n = hn.shape
    ne = u.shape[0] // te
    nk = r1.shape[1]
    nsteps = (n // tn) * ne
    last = nsteps - 1
    pair = lambda s, lag: jnp.clip(s - lag, 0, last)
    stat = pl.BlockSpec((PEER_HEADS, nk, tn), lambda s: (0, 0, pair(s, 1) // ne))
    return pl.pallas_call(
        functools.partial(_peer_dense_kernel, nk=nk, te=te, tn=tn, lc=lc, rb=min(rb, nk), ne=ne, nsteps=nsteps),
        grid=(nsteps + 2,),
        in_specs=[pl.BlockSpec((d, tn), lambda s: (0, pair(s, 0) // ne)),
                  pl.BlockSpec((te, d), lambda s: (pair(s, 0) % ne, 0)),
                  pl.BlockSpec((d, te), lambda s: (0, pair(s, 2) % ne)),
                  stat, stat, stat, stat,
                  pl.BlockSpec((d, tn), lambda s: (0, pair(s, 2) // ne))],
        out_specs=pl.BlockSpec((tn, d), lambda s: (pair(s, 2) // ne, 0)),
        out_shape=jax.ShapeDtypeStruct((n, d), F32),
        scratch_shapes=[pltpu.VMEM((d, tn), F32), pltpu.VMEM((te, tn), F32), pltpu.VMEM((te, tn), F32),
                        pltpu.VMEM((te, tn), BF16), pltpu.VMEM((te, tn), BF16)],
        compiler_params=_cparams("arbitrary", interleave=True),
        name="peer_dense",
    )(hn, u, vt, r1, w1, cnt, w0, ht)


def _peer(ht, g_col, wq_t, keys, u, vt, *, tn_sel, tn, te, lc, rb):
    hn, r1, w1, cnt, w0 = _peer_select(ht, g_col, wq_t, keys, tn_sel)
    return _peer_dense(hn, u, vt, r1, w1, cnt, w0, ht, tn=tn, te=te, lc=lc, rb=rb)


TOKEN_TILE = 512
EXPERT_TILE = 1024
GATE_LANES = 256
GATE_ROWS = 64


def _peer_tiles(n_tokens, n_keys):
    tn = min(TOKEN_TILE, n_tokens)
    return dict(tn_sel=tn, tn=tn, te=min(EXPERT_TILE, n_keys * n_keys), lc=min(GATE_LANES, tn), rb=GATE_ROWS)


def kernel(x_prompt, x_sample, cache_a_k, cache_a_v, cache_b_k, cache_b_v, norm_attn, w_in, g_qa, g_ka, g_qb, g_kb, sinks, w_o, norm_ffn, peer_wq, peer_keys, peer_u, peer_v):
    b, l, d = x_prompt.shape
    s, ns, _ = x_sample.shape
    assert ns == NEW and w_in.shape[0] == 1
    la, lb = cache_a_k.shape[2], cache_b_k.shape[2]
    w = A_WIDTH
    perm = _qb_perm()

    wl = w_in[0]
    w_all = jnp.concatenate([wl[:, :3 * w], wl[:, 3 * w:4 * w][:, perm], wl[:, 4 * w:]], axis=1).astype(BF16)
    seg = (jnp.arange(w)[:, None] // HEAD_DIM == jnp.arange(w)[None, :] // HEAD_DIM).astype(BF16)
    t8 = lambda g: jnp.tile(g, A_HEADS)[None, :]
    gains = (t8(g_qa[0]), t8(g_ka[0]), t8(g_qb[0]), jnp.tile(g_kb[0], B_KV_HEADS)[None, :])
    sb = _alibi_slopes(B_HEADS)
    slopes_perm = [sb[(h % 2) * B_GROUP + h // 2] for h in range(B_HEADS)]
    sinks_perm = jnp.stack([sinks[0, (h % 2) * B_GROUP + h // 2] for h in range(B_HEADS)])
    wo_a = w_o[0, :w, :].astype(BF16)
    wo_b = w_o[0, w:, :][perm, :].astype(BF16)
    g_col = norm_ffn[0][:, None]
    wq_t = peer_wq[0].T.astype(BF16)
    nk = peer_keys.shape[3]
    keys = peer_keys[0].reshape(2 * PEER_HEADS, nk, peer_keys.shape[4]).astype(BF16)
    u = peer_u[0].astype(BF16)
    vt = peer_v[0].T.astype(BF16)

    xp = x_prompt.reshape(b * l, d)
    tm_p = min(TOKEN_TILE, l)
    qa4, ka4, va4, qb4, kat, vat, kb, vb, kbt, vbt = _project(xp, norm_attn, w_all, seg, *gains, tm_p, b)
    seq4 = lambda t: t.reshape(t.shape[0], b, l, LANES)
    oa4 = _mixer(seq4(qa4), seq4(ka4), seq4(va4), dils=[dl for (_, dl) in A_PATTERNS],
                 slopes=_alibi_slopes(A_HEADS))
    ob4 = _mixer(seq4(qb4), kb.reshape(1, b, l, LANES), vb.reshape(1, b, l, LANES), dils=[1],
                 slopes=slopes_perm, sinks=sinks_perm)
    ht_p = _outproj(xp, oa4.reshape(NSLAB, b * l, LANES), ob4.reshape(NSLAB, b * l, LANES), wo_a, wo_b, tm_p)
    y_p = _peer(ht_p, g_col, wq_t, keys, u, vt, **_peer_tiles(b * l, nk))
    na = min(la, l)
    nb = min(lb, l)
    def tail(t, heads, keep):
        t = t[:, :, l - keep:].reshape(b, heads, HEAD_DIM, keep)
        return jnp.transpose(t, (0, 3, 1, 2))[None]
    pak, pav = tail(kat, A_HEADS, na), tail(vat, A_HEADS, na)
    pbk, pbv = tail(kbt, B_KV_HEADS, nb), tail(vbt, B_KV_HEADS, nb)

    xs = x_sample.reshape(s * ns, d)
    tm_s = min(TOKEN_TILE, s * ns)
    qa4, _, _, qb4, kat, vat, _, _, kbt, vbt = _project(xs, norm_attn, w_all, seg, *gains, tm_s, 1)
    wide = lambda t4: jnp.transpose(t4, (1, 0, 2)).reshape(s, ns, w)
    slabs = lambda t: jnp.transpose(t.reshape(s * ns, NSLAB, LANES), (1, 0, 2))
    to_t = lambda c: jnp.transpose(c[0], (0, 2, 3, 1)).reshape(s, w, la)
    from_t = lambda t: jnp.transpose(t.reshape(s, A_HEADS, HEAD_DIM, la), (0, 3, 1, 2))[None]
    new_t = lambda t: jnp.transpose(t.reshape(t.shape[1], s, ns), (1, 0, 2))
    sak_t, sav_t, oa = _sample_at(wide(qa4), new_t(kat), new_t(vat), to_t(cache_a_k), to_t(cache_a_v))
    sak, sav = from_t(sak_t), from_t(sav_t)
    to_tb = lambda c: jnp.transpose(c[0], (0, 2, 3, 1)).reshape(s, LANES, lb)
    from_tb = lambda t: jnp.transpose(t.reshape(s, B_KV_HEADS, HEAD_DIM, lb), (0, 3, 1, 2))[None]
    sbk_t, sbv_t, ob = _sample_b(sinks_perm, slopes_perm, wide(qb4), new_t(kbt), new_t(vbt),
                                 to_tb(cache_b_k), to_tb(cache_b_v))
    ht_s = _outproj(xs, slabs(oa), slabs(ob), wo_a, wo_b, tm_s)
    y_s = _peer(ht_s, g_col, wq_t, keys, u, vt, **_peer_tiles(s * ns, nk))

    return (y_p.reshape(b, l, d), y_s.reshape(s, ns, d), pak, pav, pbk, pbv,
            sak, sav, from_tb(sbk_t), from_tb(sbv_t))
```

```python
import functools
import math

import jax
import jax.numpy as jnp
from jax import lax
from jax.experimental import pallas as pl
from jax.experimental.pallas import tpu as pltpu

HEAD_DIM = 64
A_HEADS = 8
B_HEADS = 8
B_KV_HEADS = 2
B_GROUP = B_HEADS // B_KV_HEADS
A_PATTERNS = ((128, 1), (512, 4), (2048, 16))
BAND = 128
A_WIDTH = A_HEADS * HEAD_DIM
PEER_HEADS = 8
PEER_TOPK = 16
NORM_EPS = 1e-6
NEG = -1e30
SCALE = HEAD_DIM ** -0.5
LANES = 128
VMEM_LIMIT_BYTES = 56 * 1024 * 1024

BF16 = jnp.bfloat16
F32 = jnp.float32


def _alibi_slopes(n):
    return [2.0 ** (-8.0 * (i + 1) / n) for i in range(n)]


def _cparams(*sem, interleave=False):
    del interleave
    return pltpu.CompilerParams(dimension_semantics=sem, vmem_limit_bytes=VMEM_LIMIT_BYTES)


def _head_rms(h, seg, gain):
    sq = h * h
    hi = sq.astype(BF16)
    lo = (sq - hi.astype(F32)).astype(BF16)
    ms = (jnp.dot(hi, seg, preferred_element_type=F32)
          + jnp.dot(lo, seg, preferred_element_type=F32)) * (1.0 / HEAD_DIM)
    return h * lax.rsqrt(ms + NORM_EPS) * gain


NSLAB = A_WIDTH // LANES


def _proj_kernel(x_ref, g_ref, w_ref, seg_ref, gqa_ref, gka_ref, gqb_ref, gkb_ref,
                 qa4_ref, ka4_ref, va4_ref, qb4_ref, kat_ref, vat_ref, kb_ref, vb_ref, kbt_ref, vbt_ref):
    x = x_ref[...]
    ms = jnp.mean(x * x, axis=-1, keepdims=True)
    xn = (x * lax.rsqrt(ms + NORM_EPS) * g_ref[...]).astype(BF16)
    h = jnp.dot(xn, w_ref[...], preferred_element_type=F32)
    seg = seg_ref[...]
    w = A_WIDTH
    qa = _head_rms(h[:, 0:w], seg, gqa_ref[...])
    ka = _head_rms(h[:, w:2 * w], seg, gka_ref[...])
    va = h[:, 2 * w:3 * w]
    qb = _head_rms(h[:, 3 * w:4 * w], seg, gqb_ref[...])
    kb = _head_rms(h[:, 4 * w:4 * w + LANES], seg[0:LANES, 0:LANES], gkb_ref[...])
    vb = h[:, 4 * w + LANES:4 * w + 2 * LANES]
    kb_ref[...] = kb
    vb_ref[...] = vb
    kat_ref[...] = ka.T
    vat_ref[...] = va.T
    kbt_ref[...] = kb.T
    vbt_ref[...] = vb.T
    for s in range(NSLAB):
        sl = slice(s * LANES, (s + 1) * LANES)
        qa4_ref[s] = qa[:, sl]
        ka4_ref[s] = ka[:, sl]
        va4_ref[s] = va[:, sl]
        qb4_ref[s] = qb[:, sl]


def _project(x2d, norm_g, w_bf16, seg, gqa, gka, gqb, gkb, tm, nseq):
    n, d = x2d.shape
    nc = w_bf16.shape[1]
    w = A_WIDTH
    l = n // nseq
    tps = l // tm
    row = lambda i: (i, 0)
    col = lambda i: (i // tps, 0, i % tps)
    slab = lambda i: (0, i, 0)
    const = lambda i: (0, 0)
    outs = ([jax.ShapeDtypeStruct((NSLAB, n, LANES), F32)] * 4 + [jax.ShapeDtypeStruct((nseq, w, l), F32)] * 2
            + [jax.ShapeDtypeStruct((n, LANES), F32)] * 2 + [jax.ShapeDtypeStruct((nseq, LANES, l), F32)] * 2)
    return pl.pallas_call(
        _proj_kernel,
        grid=(n // tm,),
        in_specs=[pl.BlockSpec((tm, d), row), pl.BlockSpec((1, d), const), pl.BlockSpec((d, nc), const),
                  pl.BlockSpec((w, w), const), pl.BlockSpec((1, w), const), pl.BlockSpec((1, w), const),
                  pl.BlockSpec((1, w), const), pl.BlockSpec((1, LANES), const)],
        out_specs=([pl.BlockSpec((NSLAB, tm, LANES), slab)] * 4 + [pl.BlockSpec((None, w, tm), col)] * 2
                   + [pl.BlockSpec((tm, LANES), row)] * 2 + [pl.BlockSpec((None, LANES, tm), col)] * 2),
        out_shape=outs,
        compiler_params=_cparams("parallel"),
        name="qkv_proj",
    )(x2d, norm_g, w_bf16, seg, gqa, gka, gqb, gkb)


def _band_kernel(*refs, tq, dil, slopes, with_sink, with_lse):
    if with_sink:
        sink_ref, refs = refs[0], refs[1:]
    q_ref, kc_ref, kp_ref, vc_ref, vp_ref = refs[:5]
    o_ref = refs[5]
    lse_ref = refs[6] if with_lse else None
    kbuf, vbuf, bias_ref = refs[-3:]
    i = pl.program_id(2)
    nsub = tq // BAND

    @pl.when((pl.program_id(0) == 0) & (pl.program_id(1) == 0) & (i == 0))
    def _():
        r = lax.broadcasted_iota(jnp.int32, (BAND, 2 * BAND), 0)
        c = lax.broadcasted_iota(jnp.int32, (BAND, 2 * BAND), 1)
        dist = r + BAND - c
        valid = (dist >= 0) & (dist <= BAND)
        distf = (dist * dil).astype(F32)
        for h in range(A_HEADS):
            bias_ref[h] = jnp.where(valid, -slopes[h] * distf, NEG)

    kbuf[0:BAND, :] = kp_ref[0].astype(BF16)
    kbuf[BAND:BAND + tq, :] = kc_ref[0].astype(BF16)
    vbuf[0:BAND, :] = vp_ref[0].astype(BF16)
    vbuf[BAND:BAND + tq, :] = vc_ref[0].astype(BF16)

    lane = lax.broadcasted_iota(jnp.int32, (BAND, LANES), 1)
    col = lax.broadcasted_iota(jnp.int32, (1, 2 * BAND), 1)
    prev_cols = (col < BAND).astype(F32)

    def body(j, carry):
        r0 = pl.multiple_of(j * BAND, BAND)
        pen = jnp.where((i * nsub + j) == 0, NEG, 0.0) * prev_cols
        for hp in range(A_HEADS // 2):
            sl = slice(hp * LANES, (hp + 1) * LANES)
            qs = q_ref[0, pl.ds(r0, BAND), sl]
            kw = kbuf[pl.ds(r0, 2 * BAND), sl]
            vw = vbuf[pl.ds(r0, 2 * BAND), sl]
            outs, lses = [], []
            for e in range(2):
                h = 2 * hp + e
                qm = jnp.where((lane >= HEAD_DIM) == bool(e), qs, 0.0).astype(BF16)
                s = lax.dot_general(qm, kw, (((1,), (1,)), ((), ())), preferred_element_type=F32)
                s = s * SCALE + bias_ref[h] + pen
                m = jnp.max(s, axis=-1, keepdims=True)
                if with_sink:
                    m = jnp.maximum(m, sink_ref[h])
                p = jnp.exp(s - m)
                den = jnp.sum(p, axis=-1, keepdims=True)
                if with_sink:
                    den = den + jnp.exp(sink_ref[h] - m)
                o = jnp.dot(p.astype(BF16), vw, preferred_element_type=F32)
                outs.append(o / den)
                lses.append(m + jnp.log(den))
            hi = lane >= HEAD_DIM
            o_ref[0, pl.ds(r0, BAND), sl] = jnp.where(hi, outs[1], outs[0])
            if with_lse:
                lse_ref[0, pl.ds(r0, BAND), sl] = jnp.where(hi, lses[1], lses[0])
        return carry

    lax.fori_loop(0, nsub, body, 0)


def _band_attention(q, k, v, *, dil, tq, slopes, sinks=None, with_lse=True):
    b, r, cw = q.shape
    w = A_WIDTH
    c = cw // w
    assert r % tq == 0 and tq % BAND == 0
    nsub = tq // BAND
    cur = lambda bi, ci, ii: (bi, ii, ci)
    prev = lambda bi, ci, ii: (bi, jnp.maximum(ii * nsub - 1, 0), ci)
    in_specs = [pl.BlockSpec((1, tq, w), cur), pl.BlockSpec((1, tq, w), cur), pl.BlockSpec((1, BAND, w), prev),
                pl.BlockSpec((1, tq, w), cur), pl.BlockSpec((1, BAND, w), prev)]
    args = [q, k, k, v, v]
    if sinks is not None:
        in_specs = [pl.BlockSpec(memory_space=pltpu.SMEM)] + in_specs
        args = [sinks] + args
    n_out = 2 if with_lse else 1
    out = pl.pallas_call(
        functools.partial(_band_kernel, tq=tq, dil=dil, slopes=slopes, with_sink=sinks is not None,
                          with_lse=with_lse),
        grid=(b, c, r // tq),
        in_specs=in_specs,
        out_specs=[pl.BlockSpec((1, tq, w), cur)] * n_out,
        out_shape=[jax.ShapeDtypeStruct(q.shape, F32)] * n_out,
        scratch_shapes=[pltpu.VMEM((BAND + tq, w), BF16), pltpu.VMEM((BAND + tq, w), BF16),
                        pltpu.VMEM((A_HEADS, BAND, 2 * BAND), F32)],
        compiler_params=_cparams("arbitrary", "arbitrary", "arbitrary"),
        name=f"band_attn_d{dil}",
    )(*args)
    return out if with_lse else out[0]


def _prompt_mixers(qa, ka, va, qb, kbx, vbx, sinks_perm, b, l):
    w = A_WIDTH
    slopes_a = _alibi_slopes(A_HEADS)
    sb = _alibi_slopes(B_HEADS)
    slopes_b = [sb[(h % 2) * B_GROUP + h // 2] for h in range(B_HEADS)]
    branches = []
    for (_, dil) in A_PATTERNS:
        r = l // dil
        tq = min(r, 1024)
        view = lambda t: t.reshape(b, r, dil * w)
        o, lse = _band_attention(view(qa), view(ka), view(va), dil=dil, tq=tq, slopes=slopes_a)
        branches.append((o.reshape(b * l, w), lse.reshape(b * l, w)))
    view = lambda t: t.reshape(b, l, w)
    ob = _band_attention(view(qb), view(kbx), view(vbx), dil=1, tq=min(l, 1024), slopes=slopes_b,
                         sinks=sinks_perm, with_lse=False)
    return branches, ob.reshape(b * l, w)


MIXER_UNROLL = 8


def _pick(idx, values):
    out = jnp.float32(values[-1])
    for i in range(len(values) - 2, -1, -1):
        out = jnp.where(idx == i, jnp.float32(values[i]), out)
    return out


def _mixer_kernel(*refs, dils, slopes, with_sink, seq):
    if with_sink:
        sink_ref, refs = refs[0], refs[1:]
    q_ref, k_ref, v_ref, o_ref, kpad, vpad, bias_ref = refs[:7]
    ob_ref, lse_ref = refs[7:9] if len(dils) > 1 else (None, None)
    hp = pl.program_id(1)
    pad = BAND * max(dils)

    @pl.when((pl.program_id(0) == 0) & (hp == 0))
    def _():
        kpad[0:pad, :] = jnp.zeros((pad, LANES), F32)
        vpad[0:pad, :] = jnp.zeros((pad, LANES), F32)

    kpad[pad:pad + seq, :] = k_ref[0, 0]
    vpad[pad:pad + seq, :] = v_ref[0, 0]

    r = lax.broadcasted_iota(jnp.int32, (BAND, 2 * BAND), 0)
    c = lax.broadcasted_iota(jnp.int32, (BAND, 2 * BAND), 1)
    dist = r + BAND - c
    valid = (dist >= 0) & (dist <= BAND)
    for bi, dil in enumerate(dils):
        distf = (dist * dil).astype(F32)
        for e in range(2):
            slope = _pick(hp, [slopes[2 * g + e] for g in range(NSLAB)])
            bias_ref[bi, e] = jnp.where(valid, -slope * distf, NEG)

    lane = lax.broadcasted_iota(jnp.int32, (BAND, LANES), 1)
    hi = lane >= HEAD_DIM
    col = lax.broadcasted_iota(jnp.int32, (1, 2 * BAND), 1)
    prev_cols = (col < BAND).astype(F32)
    nt = (((1,), (1,)), ((), ()))

    for bi, dil in enumerate(dils):
        nblk = seq // (dil * BAND)

        def block(t, bi=bi, dil=dil, nblk=nblk):
            res = t // nblk
            j = t % nblk
            base = res + dil * BAND * j
            if dil == 1:
                base = pl.multiple_of(base, BAND)
                rows = pl.ds(base, BAND)
                win = pl.ds(pl.multiple_of(pad + base - BAND, BAND), 2 * BAND)
            else:
                rows = pl.ds(base, BAND, stride=dil)
                win = pl.ds(pad + base - dil * BAND, 2 * BAND, stride=dil)
            qs = q_ref[0, 0, rows, :]
            kw = kpad[win, :].astype(BF16)
            vw = vpad[win, :].astype(BF16)
            pen = jnp.where(j == 0, NEG, 0.0) * prev_cols
            outs, lses = [], []
            for e in range(2):
                qm = jnp.where(hi == bool(e), qs, 0.0).astype(BF16)
                s = lax.dot_general(qm, kw, nt, preferred_element_type=F32)
                s = s * SCALE + bias_ref[bi, e] + pen
                m = jnp.max(s, axis=-1, keepdims=True)
                if with_sink:
                    sink = sink_ref[2 * hp + e]
                    m = jnp.maximum(m, sink)
                p = jnp.exp(s - m)
                den = jnp.sum(p, axis=-1, keepdims=True)
                if with_sink:
                    den = den + jnp.exp(sink - m)
                o = jnp.dot(p.astype(BF16), vw, preferred_element_type=F32)
                outs.append(o / den)
                lses.append(m + jnp.log(den))
            if len(dils) == 1:
                o_ref[0, 0, rows, :] = jnp.where(hi, outs[1], outs[0])
            else:
                ob_ref[bi, rows, :] = jnp.where(hi, outs[1], outs[0])
                lse_ref[bi, rows, :] = jnp.where(hi, lses[1], lses[0])

        def blocks(t, carry, block=block):
            for uu in range(MIXER_UNROLL):
                block(t * MIXER_UNROLL + uu)
            return carry

        assert (dil * nblk) % MIXER_UNROLL == 0
        lax.fori_loop(0, dil * nblk // MIXER_UNROLL, blocks, 0)

    if len(dils) > 1:
        def merge(t, carry):
            rows = pl.ds(pl.multiple_of(t * BAND, BAND), BAND)
            ls = [lse_ref[bi, rows, :] for bi in range(len(dils))]
            mx = functools.reduce(jnp.maximum, ls)
            ws = [jnp.exp(l - mx) for l in ls]
            num = sum(w * ob_ref[bi, rows, :] for bi, w in enumerate(ws))
            o_ref[0, 0, rows, :] = num / sum(ws)
            return carry

        lax.fori_loop(0, seq // BAND, merge, 0)


def _mixer(q4, k4, v4, *, dils, slopes, sinks=None):
    nslab, b, l, _ = q4.shape
    assert l % (BAND * max(dils)) == 0
    shared = k4.shape[0] == 1
    qmap = lambda bi, hp: (hp, bi, 0, 0)
    kmap = (lambda bi, hp: (0, bi, 0, 0)) if shared else qmap
    blk = (1, 1, l, LANES)
    in_specs = [pl.BlockSpec(blk, qmap), pl.BlockSpec(blk, kmap), pl.BlockSpec(blk, kmap)]
    args = [q4, k4, v4]
    if sinks is not None:
        in_specs = [pl.BlockSpec(memory_space=pltpu.SMEM)] + in_specs
        args = [sinks] + args
    pad = BAND * max(dils)
    scratch = [pltpu.VMEM((pad + l, LANES), F32), pltpu.VMEM((pad + l, LANES), F32),
               pltpu.VMEM((len(dils), 2, BAND, 2 * BAND), F32)]
    if len(dils) > 1:
        scratch += [pltpu.VMEM((len(dils), l, LANES), F32), pltpu.VMEM((len(dils), l, LANES), F32)]
    return pl.pallas_call(
        functools.partial(_mixer_kernel, dils=tuple(dils), slopes=slopes, with_sink=sinks is not None, seq=l),
        grid=(b, nslab),
        in_specs=in_specs,
        out_specs=pl.BlockSpec(blk, qmap),
        out_shape=jax.ShapeDtypeStruct(q4.shape, F32),
        scratch_shapes=scratch,
        compiler_params=_cparams("arbitrary", "arbitrary"),
        name="mixer_b" if shared else "mixer_a",
    )(*args)


def _qb_perm():
    idx = []
    for g in range(B_GROUP):
        for kv in range(B_KV_HEADS):
            base = (kv * B_GROUP + g) * HEAD_DIM
            idx.extend(range(base, base + HEAD_DIM))
    return jnp.asarray(idx, jnp.int32)


NEW = 8
HQ = A_HEADS * NEW


def _block_diag_rows(q8, width):
    rep = jnp.concatenate([q8] * A_HEADS, axis=0)
    r = lax.broadcasted_iota(jnp.int32, (HQ, width), 0)
    c = lax.broadcasted_iota(jnp.int32, (HQ, width), 1)
    return jnp.where(r // NEW == c // HEAD_DIM, rep, 0.0)


def _sample_a_kernel(q_ref, kn_ref, vn_ref, ck_ref, cv_ref, nk_ref, nv_ref, o_ref,
                     b1_ref, b4_ref, b16_ref, bn_ref, knb, vnb, *, la, slopes):
    spans = [w for (w, _) in A_PATTERNS]
    bias_refs = [b1_ref, b4_ref, b16_ref]

    @pl.when(pl.program_id(0) == 0)
    def _():
        for bi, (wdw, dil) in enumerate(A_PATTERNS):
            ncol = spans[bi]
            r = lax.broadcasted_iota(jnp.int32, (HQ, ncol), 0)
            c = lax.broadcasted_iota(jnp.int32, (HQ, ncol), 1)
            dist = (r % NEW) + ncol - c
            valid = (dist <= wdw) & (dist % dil == 0)
            slope = jnp.zeros((HQ, ncol), F32)
            for h in range(A_HEADS):
                slope = jnp.where(r // NEW == h, slopes[h], slope)
            bias_refs[bi][...] = jnp.where(valid, -slope * dist.astype(F32), NEG)
            rn = lax.broadcasted_iota(jnp.int32, (HQ, LANES), 0)
            cn = lax.broadcasted_iota(jnp.int32, (HQ, LANES), 1)
            dn = (rn % NEW) - cn
            vn_ok = (dn >= 0) & (dn % dil == 0) & (cn < NEW)
            sl = jnp.zeros((HQ, LANES), F32)
            for h in range(A_HEADS):
                sl = jnp.where(rn // NEW == h, slopes[h], sl)
            bn_ref[bi] = jnp.where(vn_ok, -sl * dn.astype(F32), NEG)
        knb[...] = jnp.zeros_like(knb)
        vnb[...] = jnp.zeros_like(vnb)

    kn = kn_ref[0]
    vn = vn_ref[0]
    nk_ref[0, 0:la - NEW, :] = ck_ref[0, NEW:la, :]
    nk_ref[0, la - NEW:la, :] = kn
    nv_ref[0, 0:la - NEW, :] = cv_ref[0, NEW:la, :]
    nv_ref[0, la - NEW:la, :] = vn
    knb[0:NEW, :] = kn
    vnb[0:NEW, :] = vn

    qbd = _block_diag_rows(q_ref[0], A_WIDTH).astype(BF16)
    nt = (((1,), (1,)), ((), ()))
    s_new = lax.dot_general(qbd, knb[...].astype(BF16), nt, preferred_element_type=F32) * SCALE
    ms, dens, nums = [], [], []
    for bi in range(len(A_PATTERNS)):
        ncol = spans[bi]
        kc = ck_ref[0, la - ncol:la, :].astype(BF16)
        vc = cv_ref[0, la - ncol:la, :].astype(BF16)
        sc = lax.dot_general(qbd, kc, nt, preferred_element_type=F32) * SCALE + bias_refs[bi][...]
        sn = s_new + bn_ref[bi]
        m = jnp.maximum(jnp.max(sc, axis=-1, keepdims=True), jnp.max(sn, axis=-1, keepdims=True))
        pc = jnp.exp(sc - m)
        pn = jnp.exp(sn - m)
        dens.append(jnp.sum(pc, axis=-1, keepdims=True) + jnp.sum(pn, axis=-1, keepdims=True))
        nums.append(jnp.dot(pc.astype(BF16), vc, preferred_element_type=F32)
                    + jnp.dot(pn.astype(BF16), vnb[...].astype(BF16), preferred_element_type=F32))
        ms.append(m)
    mx = jnp.maximum(jnp.maximum(ms[0], ms[1]), ms[2])
    den = 0.0
    num = 0.0
    for bi in range(len(A_PATTERNS)):
        wt = jnp.exp(ms[bi] - mx)
        den = den + dens[bi] * wt
        num = num + nums[bi] * wt
    out = _block_diag_rows_keep(num / den, A_WIDTH)
    o_ref[0] = sum(out[h * NEW:(h + 1) * NEW, :] for h in range(A_HEADS))


def _block_diag_rows_keep(x, width):
    r = lax.broadcasted_iota(jnp.int32, (HQ, width), 0)
    c = lax.broadcasted_iota(jnp.int32, (HQ, width), 1)
    return jnp.where(r // NEW == c // HEAD_DIM, x, 0.0)


def _sample_a(q, kn, vn, ck, cv):
    s, la, w = ck.shape
    assert la >= max(wd for (wd, _) in A_PATTERNS)
    seq = lambda i: (i, 0, 0)
    small = pl.BlockSpec((1, NEW, w), seq)
    big = pl.BlockSpec((1, la, w), seq)
    spans = [wd for (wd, _) in A_PATTERNS]
    return pl.pallas_call(
        functools.partial(_sample_a_kernel, la=la, slopes=_alibi_slopes(A_HEADS)),
        grid=(s,),
        in_specs=[small, small, small, big, big],
        out_specs=[big, big, small],
        out_shape=[jax.ShapeDtypeStruct(ck.shape, F32), jax.ShapeDtypeStruct(cv.shape, F32),
                   jax.ShapeDtypeStruct(q.shape, F32)],
        scratch_shapes=[pltpu.VMEM((HQ, spans[0]), F32), pltpu.VMEM((HQ, spans[1]), F32),
                        pltpu.VMEM((HQ, spans[2]), F32), pltpu.VMEM((len(A_PATTERNS), HQ, LANES), F32),
                        pltpu.VMEM((LANES, w), F32), pltpu.VMEM((LANES, w), F32)],
        compiler_params=_cparams("arbitrary"),
        name="sample_attn_a",
    )(q, kn, vn, ck, cv)


SAMPLE_CHUNK = 512


def _sample_a5_kernel(q_ref, kn_ref, vn_ref, ck_ref, cv_ref, nk_hbm, nv_hbm, o_ref,
                      b16_ref, b4_ref, b1_ref, bn_ref, sem, *, la, slopes):
    b = pl.program_id(0)
    copies = [
        pltpu.make_async_copy(ck_ref.at[0, 0, pl.ds(NEW, la - NEW)], nk_hbm.at[0, b, pl.ds(0, la - NEW)], sem.at[0]),
        pltpu.make_async_copy(cv_ref.at[0, 0, pl.ds(NEW, la - NEW)], nv_hbm.at[0, b, pl.ds(0, la - NEW)], sem.at[1]),
        pltpu.make_async_copy(kn_ref.at[0], nk_hbm.at[0, b, pl.ds(la - NEW, NEW)], sem.at[2]),
        pltpu.make_async_copy(vn_ref.at[0], nv_hbm.at[0, b, pl.ds(la - NEW, NEW)], sem.at[3]),
    ]
    for cp in copies:
        cp.start()

    spans = [wd for (wd, _) in A_PATTERNS]
    bias_refs = [b1_ref, b4_ref, b16_ref]

    @pl.when(b == 0)
    def _():
        for bi, (wdw, dil) in enumerate(A_PATTERNS):
            ncol = spans[bi] * A_HEADS
            r = lax.broadcasted_iota(jnp.int32, (HQ, ncol), 0)
            c = lax.broadcasted_iota(jnp.int32, (HQ, ncol), 1)
            dist = (r % NEW) + spans[bi] - c // A_HEADS
            valid = (dist <= wdw) & (dist % dil == 0) & (c % A_HEADS == r // NEW)
            slope = jnp.zeros((HQ, ncol), F32)
            for h in range(A_HEADS):
                slope = jnp.where(r // NEW == h, slopes[h], slope)
            bias_refs[bi][...] = jnp.where(valid, -slope * dist.astype(F32), NEG)
            rn = lax.broadcasted_iota(jnp.int32, (HQ, HQ), 0)
            cn = lax.broadcasted_iota(jnp.int32, (HQ, HQ), 1)
            dn = (rn % NEW) - cn // A_HEADS
            ok = (dn >= 0) & (dn % dil == 0) & (cn % A_HEADS == rn // NEW)
            sl = jnp.zeros((HQ, HQ), F32)
            for h in range(A_HEADS):
                sl = jnp.where(rn // NEW == h, slopes[h], sl)
            bn_ref[bi] = jnp.where(ok, -sl * dn.astype(F32), NEG)

    nt = (((1,), (1,)), ((), ()))
    q = q_ref[0].astype(BF16)

    def scores(k3):
        k2 = k3.reshape(k3.shape[0] * A_HEADS, HEAD_DIM).astype(BF16)
        return lax.dot_general(q, k2, nt, preferred_element_type=F32) * SCALE

    def flat_v(v3):
        return v3.reshape(v3.shape[0] * A_HEADS, HEAD_DIM).astype(BF16)

    def fold(state, s, v2):
        m_new = jnp.max(s, axis=-1, keepdims=True)
        if state is not None:
            m_old, den, num = state
            m_new = jnp.maximum(m_new, m_old)
        p = jnp.exp(s - m_new)
        d_new = jnp.sum(p, axis=-1, keepdims=True)
        n_new = jnp.dot(p.astype(BF16), v2, preferred_element_type=F32)
        if state is not None:
            alpha = jnp.exp(m_old - m_new)
            d_new = d_new + den * alpha
            n_new = n_new + num * alpha
        return m_new, d_new, n_new

    states = [None, None, None]
    nchunk = la // SAMPLE_CHUNK
    ch_cols = SAMPLE_CHUNK * A_HEADS
    for ci in range(nchunk):
        rows = slice(ci * SAMPLE_CHUNK, (ci + 1) * SAMPLE_CHUNK)
        s = scores(ck_ref[0, 0, rows])
        v2 = flat_v(cv_ref[0, 0, rows])
        states[2] = fold(states[2], s + b16_ref[:, ci * ch_cols:(ci + 1) * ch_cols], v2)
        if ci == nchunk - 1:
            for bi in (0, 1):
                ncol = spans[bi] * A_HEADS
                states[bi] = fold(states[bi], s[:, ch_cols - ncol:] + bias_refs[bi][...], v2[ch_cols - ncol:, :])
    s_new = scores(kn_ref[0])
    v_new = flat_v(vn_ref[0])
    for bi in range(len(A_PATTERNS)):
        states[bi] = fold(states[bi], s_new + bn_ref[bi], v_new)
    mx = functools.reduce(jnp.maximum, [st[0] for st in states])
    den = 0.0
    num = 0.0
    for (m, dd, nn) in states:
        wt = jnp.exp(m - mx)
        den = den + dd * wt
        num = num + nn * wt
    o_ref[0] = num / den

    for cp in copies:
        cp.wait()


def _sample_a5(q64, kn, vn, ck5, cv5):
    _, s, la, nh, hd = ck5.shape
    spans = [wd for (wd, _) in A_PATTERNS]
    assert la == max(spans) and la % SAMPLE_CHUNK == 0 and spans[1] == SAMPLE_CHUNK and nh == A_HEADS
    seq3 = lambda i: (i, 0, 0)
    seq4 = lambda i: (i, 0, 0, 0)
    big = pl.BlockSpec((1, 1, la, nh, hd), lambda i: (0, i, 0, 0, 0))
    anyspec = pl.BlockSpec(memory_space=pl.ANY)
    return pl.pallas_call(
        functools.partial(_sample_a5_kernel, la=la, slopes=_alibi_slopes(A_HEADS)),
        grid=(s,),
        in_specs=[pl.BlockSpec((1, HQ, hd), seq3), pl.BlockSpec((1, NEW, nh, hd), seq4),
                  pl.BlockSpec((1, NEW, nh, hd), seq4), big, big],
        out_specs=[anyspec, anyspec, pl.BlockSpec((1, HQ, hd), seq3)],
        out_shape=[jax.ShapeDtypeStruct(ck5.shape, F32), jax.ShapeDtypeStruct(cv5.shape, F32),
                   jax.ShapeDtypeStruct(q64.shape, F32)],
        scratch_shapes=[pltpu.VMEM((HQ, spans[2] * nh), F32), pltpu.VMEM((HQ, spans[1] * nh), F32),
                        pltpu.VMEM((HQ, spans[0] * nh), F32), pltpu.VMEM((len(A_PATTERNS), HQ, HQ), F32),
                        pltpu.SemaphoreType.DMA((4,))],
        compiler_params=_cparams("arbitrary"),
        name="sample_attn_a5",
    )(q64, kn, vn, ck5, cv5)


ROLL_ROWS = 64


def _sample_at_kernel(q_ref, kn_ref, vn_ref, ck_ref, cv_ref, nk_ref, nv_ref, o_ref, bias_ref, bn_ref, knb, vnb,
                      *, la, slopes):
    w = A_WIDTH
    nbr = len(A_PATTERNS)

    @pl.when(pl.program_id(0) == 0)
    def _():
        r = lax.broadcasted_iota(jnp.int32, (HQ, la), 0)
        c = lax.broadcasted_iota(jnp.int32, (HQ, la), 1)
        dist = (r % NEW) + la - c
        slope = jnp.zeros((HQ, la), F32)
        rn = lax.broadcasted_iota(jnp.int32, (HQ, LANES), 0)
        cn = lax.broadcasted_iota(jnp.int32, (HQ, LANES), 1)
        dn = (rn % NEW) - cn
        sl = jnp.zeros((HQ, LANES), F32)
        for h in range(A_HEADS):
            slope = jnp.where(r // NEW == h, slopes[h], slope)
            sl = jnp.where(rn // NEW == h, slopes[h], sl)
        for bi, (wdw, dil) in enumerate(A_PATTERNS):
            bias_ref[bi] = jnp.where((dist <= wdw) & (dist % dil == 0), -slope * dist.astype(F32), NEG)
            bn_ref[bi] = jnp.where((dn >= 0) & (dn % dil == 0) & (cn < NEW), -sl * dn.astype(F32), NEG)
        knb[...] = jnp.zeros_like(knb)
        vnb[...] = jnp.zeros_like(vnb)

    knb[:, 0:NEW] = kn_ref[0]
    vnb[:, 0:NEW] = vn_ref[0]

    qbd = _block_diag_rows(q_ref[0], w).astype(BF16)
    nt = (((1,), (1,)), ((), ()))
    s_c = jnp.dot(qbd, ck_ref[0].astype(BF16), preferred_element_type=F32) * SCALE
    s_n = jnp.dot(qbd, knb[...].astype(BF16), preferred_element_type=F32) * SCALE
    ms, dens, pcs, pns = [], [], [], []
    for bi in range(nbr):
        sc = s_c + bias_ref[bi]
        sn = s_n + bn_ref[bi]
        m = jnp.maximum(jnp.max(sc, axis=-1, keepdims=True), jnp.max(sn, axis=-1, keepdims=True))
        pc = jnp.exp(sc - m)
        pn = jnp.exp(sn - m)
        dens.append(jnp.sum(pc, axis=-1, keepdims=True) + jnp.sum(pn, axis=-1, keepdims=True))
        ms.append(m)
        pcs.append(pc.astype(BF16))
        pns.append(pn.astype(BF16))
    nums = (lax.dot_general(jnp.concatenate(pcs, axis=0), cv_ref[0].astype(BF16), nt, preferred_element_type=F32)
            + lax.dot_general(jnp.concatenate(pns, axis=0), vnb[...].astype(BF16), nt, preferred_element_type=F32))
    mx = functools.reduce(jnp.maximum, ms)
    den = 0.0
    num = 0.0
    for bi in range(nbr):
        wt = jnp.exp(ms[bi] - mx)
        den = den + dens[bi] * wt
        num = num + nums[bi * HQ:(bi + 1) * HQ, :] * wt
    out = _block_diag_rows_keep(num / den, w)
    o_ref[0] = sum(out[h * NEW:(h + 1) * NEW, :] for h in range(A_HEADS))

    lane = lax.broadcasted_iota(jnp.int32, (ROLL_ROWS, LANES), 1)
    tail = lane >= LANES - NEW
    for src, newb, dst in ((ck_ref, knb, nk_ref), (cv_ref, vnb, nv_ref)):
        for r0 in range(0, w, ROLL_ROWS):
            rows = slice(r0, r0 + ROLL_ROWS)
            rolled = pltpu.roll(src[0, rows, :], la - NEW, 1)
            dst[0, rows, 0:la - LANES] = rolled[:, 0:la - LANES]
            fresh = pltpu.roll(newb[rows, :], LANES - NEW, 1)
            dst[0, rows, la - LANES:la] = jnp.where(tail, fresh, rolled[:, la - LANES:la])


def _sample_at(q, kn_t, vn_t, ck_t, cv_t):
    s, w, la = ck_t.shape
    assert la >= max(wd for (wd, _) in A_PATTERNS) and la % LANES == 0
    seq = lambda i: (i, 0, 0)
    big = pl.BlockSpec((1, w, la), seq)
    new = pl.BlockSpec((1, w, NEW), seq)
    qspec = pl.BlockSpec((1, NEW, w), seq)
    nbr = len(A_PATTERNS)
    return pl.pallas_call(
        functools.partial(_sample_at_kernel, la=la, slopes=_alibi_slopes(A_HEADS)),
        grid=(s,),
        in_specs=[qspec, new, new, big, big],
        out_specs=[big, big, qspec],
        out_shape=[jax.ShapeDtypeStruct(ck_t.shape, F32), jax.ShapeDtypeStruct(cv_t.shape, F32),
                   jax.ShapeDtypeStruct(q.shape, F32)],
        scratch_shapes=[pltpu.VMEM((nbr, HQ, la), F32), pltpu.VMEM((nbr, HQ, LANES), F32),
                        pltpu.VMEM((w, LANES), F32), pltpu.VMEM((w, LANES), F32)],
        compiler_params=_cparams("arbitrary"),
        name="sample_attn_at",
    )(q, kn_t, vn_t, ck_t, cv_t)


def _sample_b_kernel(sink_ref, q_ref, kn_ref, vn_ref, ck_ref, cv_ref, nk_ref, nv_ref, o_ref,
                     bc_ref, bn_ref, sk_ref, knb, vnb, *, lb, nseq, slopes):
    @pl.when(pl.program_id(0) == 0)
    def _():
        r = lax.broadcasted_iota(jnp.int32, (HQ, lb), 0)
        c = lax.broadcasted_iota(jnp.int32, (HQ, lb), 1)
        dist = (r % NEW) + lb - c
        slope = jnp.zeros((HQ, lb), F32)
        sink = jnp.zeros((HQ, LANES), F32)
        rs = lax.broadcasted_iota(jnp.int32, (HQ, LANES), 0)
        for h in range(B_HEADS):
            slope = jnp.where(r // NEW == h, slopes[h], slope)
            sink = jnp.where(rs // NEW == h, sink_ref[h], sink)
        bc_ref[...] = jnp.where(dist <= BAND, -slope * dist.astype(F32), NEG)
        cn = lax.broadcasted_iota(jnp.int32, (HQ, LANES), 1)
        dn = (rs % NEW) - cn
        sl = jnp.zeros((HQ, LANES), F32)
        for h in range(B_HEADS):
            sl = jnp.where(rs // NEW == h, slopes[h], sl)
        bn_ref[...] = jnp.where((dn >= 0) & (cn < NEW), -sl * dn.astype(F32), NEG)
        sk_ref[...] = sink
        knb[...] = jnp.zeros_like(knb)
        vnb[...] = jnp.zeros_like(vnb)

    nt = (((1,), (1,)), ((), ()))
    rr = lax.broadcasted_iota(jnp.int32, (HQ, LANES), 0)
    cc = lax.broadcasted_iota(jnp.int32, (HQ, LANES), 1)
    own = ((rr // NEW) % B_KV_HEADS) == (cc // HEAD_DIM)
    sink = sk_ref[:, 0:1]
    tail = cc >= LANES - NEW
    for t in range(nseq):
        knb[:, 0:NEW] = kn_ref[t]
        vnb[:, 0:NEW] = vn_ref[t]
        for src, newb, dst in ((ck_ref, knb, nk_ref), (cv_ref, vnb, nv_ref)):
            for r0 in range(0, LANES, HQ):
                rows = slice(r0, r0 + HQ)
                dst[t, rows, :] = jnp.where(tail, pltpu.roll(newb[rows, :], LANES - NEW, 1),
                                            pltpu.roll(src[t, rows, :], lb - NEW, 1))
        q = q_ref[t]
        rows = []
        for g in range(B_GROUP):
            for kv in range(B_KV_HEADS):
                rows.append(q[:, g * LANES:(g + 1) * LANES])
        qbd = jnp.where(own, jnp.concatenate(rows, axis=0), 0.0).astype(BF16)
        sc = jnp.dot(qbd, ck_ref[t].astype(BF16), preferred_element_type=F32) * SCALE + bc_ref[...]
        sn = jnp.dot(qbd, knb[...].astype(BF16), preferred_element_type=F32) * SCALE + bn_ref[...]
        m = jnp.maximum(jnp.maximum(jnp.max(sc, axis=-1, keepdims=True), jnp.max(sn, axis=-1, keepdims=True)), sink)
        pc = jnp.exp(sc - m)
        pn = jnp.exp(sn - m)
        den = jnp.sum(pc, axis=-1, keepdims=True) + jnp.sum(pn, axis=-1, keepdims=True) + jnp.exp(sink - m)
        o = (lax.dot_general(pc.astype(BF16), cv_ref[t].astype(BF16), nt, preferred_element_type=F32)
             + lax.dot_general(pn.astype(BF16), vnb[...].astype(BF16), nt, preferred_element_type=F32)) / den
        o = jnp.where(own, o, 0.0)
        for g in range(B_GROUP):
            base = g * B_KV_HEADS * NEW
            o_ref[t, :, g * LANES:(g + 1) * LANES] = o[base:base + NEW, :] + o[base + NEW:base + 2 * NEW, :]


def _sample_b(sinks_perm, slopes_perm, q, kn, vn, ck, cv, nseq=8):
    s, kw, lb = ck.shape
    assert lb == BAND and kw == LANES and lb == LANES and s % nseq == 0
    seq = lambda i: (i, 0, 0)
    return pl.pallas_call(
        functools.partial(_sample_b_kernel, lb=lb, nseq=nseq, slopes=slopes_perm),
        grid=(s // nseq,),
        in_specs=[pl.BlockSpec(memory_space=pltpu.SMEM), pl.BlockSpec((nseq, NEW, A_WIDTH), seq),
                  pl.BlockSpec((nseq, kw, NEW), seq), pl.BlockSpec((nseq, kw, NEW), seq),
                  pl.BlockSpec((nseq, kw, lb), seq), pl.BlockSpec((nseq, kw, lb), seq)],
        out_specs=[pl.BlockSpec((nseq, kw, lb), seq), pl.BlockSpec((nseq, kw, lb), seq),
                   pl.BlockSpec((nseq, NEW, A_WIDTH), seq)],
        out_shape=[jax.ShapeDtypeStruct(ck.shape, F32), jax.ShapeDtypeStruct(cv.shape, F32),
                   jax.ShapeDtypeStruct(q.shape, F32)],
        scratch_shapes=[pltpu.VMEM((HQ, lb), F32), pltpu.VMEM((HQ, LANES), F32), pltpu.VMEM((HQ, LANES), F32),
                        pltpu.VMEM((LANES, kw), F32), pltpu.VMEM((LANES, kw), F32)],
        compiler_params=_cparams("arbitrary"),
        name="sample_attn_b",
    )(sinks_perm, q, kn, vn, ck, cv)


def _outproj_kernel(x_ref, oa_ref, ob_ref, wa_ref, wb_ref, ht_ref):
    oa = jnp.concatenate([oa_ref[s] for s in range(NSLAB)], axis=1).astype(BF16)
    ob = jnp.concatenate([ob_ref[s] for s in range(NSLAB)], axis=1).astype(BF16)
    h = (x_ref[...] + jnp.dot(oa, wa_ref[...], preferred_element_type=F32)
         + jnp.dot(ob, wb_ref[...], preferred_element_type=F32))
    ht_ref[...] = h.T


def _outproj(x2d, oa4, ob4, wa, wb, tm):
    n, d = x2d.shape
    w = A_WIDTH
    row = lambda i: (i, 0)
    const = lambda i: (0, 0)
    slab = pl.BlockSpec((NSLAB, tm, LANES), lambda i: (0, i, 0))
    return pl.pallas_call(
        _outproj_kernel,
        grid=(n // tm,),
        in_specs=[pl.BlockSpec((tm, d), row), slab, slab, pl.BlockSpec((w, d), const), pl.BlockSpec((w, d), const)],
        out_specs=pl.BlockSpec((d, tm), lambda i: (0, i)),
        out_shape=jax.ShapeDtypeStruct((d, n), F32),
        compiler_params=_cparams("parallel"),
        name="out_proj",
    )(x2d, oa4, ob4, wa, wb)


BIG = 3.0e38


SUBLANES = 8


def _merge_exchange(n):
    pairs = []
    t = max(1, math.ceil(math.log2(n)))
    p = 1 << (t - 1)
    while p > 0:
        q, r, d = 1 << (t - 1), 0, p
        while d > 0:
            pairs.extend((i, i + d) for i in range(n - d) if (i & p) == r)
            d, q, r = q - p, q >> 1, p
        p >>= 1
    return pairs


def _vmax(a, b):
    if a is None:
        return b
    if b is None:
        return a
    return jnp.maximum(a, b)


def _vmin(a, b):
    if a is None or b is None:
        return None
    return jnp.minimum(a, b)


def _exchange(x, i, j):
    x[i], x[j] = _vmax(x[i], x[j]), _vmin(x[i], x[j])


def _top16(tiles):
    x = list(tiles) + [None] * (PEER_TOPK - len(tiles))
    for (i, j) in _merge_exchange(len(tiles)):
        _exchange(x, i, j)
    for shift in (4, 2, 1):
        y = [None if v is None else pltpu.roll(v, shift, 0) for v in x]
        x = [_vmax(x[k], y[PEER_TOPK - 1 - k]) for k in range(PEER_TOPK)]
        for d in (8, 4, 2, 1):
            for i in range(PEER_TOPK):
                if not i & d:
                    _exchange(x, i, i + d)
    return x


def _count_prefix(pred, vals):
    sel = jnp.where
    c16 = pred(vals[15])
    c8 = pred(vals[7])
    c4 = pred(sel(c8, vals[11], vals[3]))
    c2 = pred(sel(c8, sel(c4, vals[13], vals[9]), sel(c4, vals[5], vals[1])))
    c1 = pred(sel(c8, sel(c4, sel(c2, vals[14], vals[12]), sel(c2, vals[10], vals[8])),
                  sel(c4, sel(c2, vals[6], vals[4]), sel(c2, vals[2], vals[0]))))
    lo = sel(c8, 8.0, 0.0) + sel(c4, 4.0, 0.0) + sel(c2, 2.0, 0.0) + sel(c1, 1.0, 0.0)
    return sel(c16, float(PEER_TOPK), lo)


def _rows_sum(x):
    for shift in (4, 2, 1):
        x = x + pltpu.roll(x, shift, 0)
    return x


def _peer_select_kernel(ht_ref, g_ref, wq_ref, keys_ref,
                        hn_ref, r1_ref, w1_ref, cnt_ref, w0_ref, s_ref, *, nk, tn):
    h = ht_ref[...]
    ms = jnp.mean(h * h, axis=0, keepdims=True)
    hn = (h * lax.rsqrt(ms + NORM_EPS) * g_ref[...]).astype(BF16)
    hn_ref[...] = hn
    q = jnp.dot(wq_ref[...], hn, preferred_element_type=F32)
    half = q.shape[0] // (2 * PEER_HEADS)
    for k in range(2 * PEER_HEADS):
        qk = q[k * half:(k + 1) * half, :].astype(BF16)
        s_ref[k] = jnp.dot(keys_ref[k], qk, preferred_element_type=F32)
    nchunk = tn // LANES
    ntile = nk // SUBLANES
    sub = lax.broadcasted_iota(jnp.int32, (SUBLANES, LANES), 0)

    def pack(vals):
        out = vals[-1]
        for r in range(len(vals) - 2, -1, -1):
            out = jnp.where(sub == r, vals[r], out)
        return out

    def head_chunk(t, carry):
        hd = t // nchunk
        lanes = pl.ds(pl.multiple_of((t % nchunk) * LANES, LANES), LANES)
        rows = [slice(k * SUBLANES, (k + 1) * SUBLANES) for k in range(ntile)]
        s0 = [s_ref[2 * hd, r, lanes] for r in rows]
        s1 = [s_ref[2 * hd + 1, r, lanes] for r in rows]
        v0 = _top16(s0)
        v1 = _top16(s1)
        v1lo, v1hi, v0hi = pack(v1[0:8]), pack(v1[8:16]), pack(v0[8:16])
        cands = [v0[0] + v1lo, v0[0] + v1hi] + [v0[a] + v1lo for a in range(1, 8)] + [v0hi + v1[0]]
        best = _top16(cands)
        top, tau = best[0], best[PEER_TOPK - 1]
        z = _rows_sum(sum(jnp.where(c >= tau, jnp.exp(c - top), 0.0) for c in cands))
        inv_z = 1.0 / z
        for m in range(ntile // 2):
            cnt, rk1 = [], []
            for k in (2 * m, 2 * m + 1):
                cnt.append(_count_prefix(lambda t, k=k: s0[k] + t >= tau, v1))
                rk1.append(_count_prefix(lambda t, k=k: t > s1[k], v1))
            pair = slice(2 * m * SUBLANES, (2 * m + 2) * SUBLANES)
            both = lambda f: jnp.concatenate([f(2 * m), f(2 * m + 1)], axis=0)
            cnt_ref[hd, pair, lanes] = jnp.concatenate(cnt, axis=0)
            w0_ref[hd, pair, lanes] = both(lambda k: jnp.exp(s0[k] - v0[0]) * inv_z)
            r1_ref[hd, pair, lanes] = jnp.concatenate(rk1, axis=0).astype(BF16)
            w1_ref[hd, pair, lanes] = both(lambda k: jnp.exp(s1[k] - v1[0])).astype(BF16)
        return carry

    lax.fori_loop(0, PEER_HEADS * nchunk, head_chunk, 0)


def _peer_select(ht, g_col, wq_t, keys, tn):
    d, n = ht.shape
    assert tn % LANES == 0 and n % tn == 0
    nslab, nk, half = keys.shape
    tok = lambda i: (0, i)
    tok3 = lambda i: (0, 0, i)
    stat = lambda dt: jax.ShapeDtypeStruct((PEER_HEADS, nk, n), dt)
    return pl.pallas_call(
        functools.partial(_peer_select_kernel, nk=nk, tn=tn),
        grid=(n // tn,),
        in_specs=[pl.BlockSpec((d, tn), tok), pl.BlockSpec((d, 1), lambda i: (0, 0)),
                  pl.BlockSpec(wq_t.shape, lambda i: (0, 0)), pl.BlockSpec(keys.shape, lambda i: (0, 0, 0))],
        out_specs=[pl.BlockSpec((d, tn), tok)] + [pl.BlockSpec((PEER_HEADS, nk, tn), tok3)] * 4,
        out_shape=[jax.ShapeDtypeStruct((d, n), BF16), stat(BF16), stat(BF16), stat(F32), stat(F32)],
        scratch_shapes=[pltpu.VMEM((nslab, nk, tn), F32)],
        compiler_params=_cparams("parallel"),
        name="peer_select",
    )(ht, g_col, wq_t, keys)


def _gelu(x):
    return 0.5 * x * (1.0 + lax.erf(x * (2.0 ** -0.5)))


def _peer_dense_kernel(hn_ref, u_ref, vt_ref, r1_ref, w1_ref, cnt_ref, w0_ref, ht_ref, y_ref,
                       acc_ref, c0_ref, c1_ref, g_ref, *, nk, te, tn, lc, rb, ne, nsteps):
    s = pl.program_id(0)

    @pl.when(s == 0)
    def _():
        for r in (c0_ref, c1_ref, acc_ref):
            r[...] = jnp.zeros_like(r)

    e = jnp.minimum(s, nsteps - 1) % ne
    p1 = s - 1
    first = (jnp.maximum(p1, 0) % ne) == 0
    nslab = te // nk

    def stages(c_new, c_prev):
        ob = acc_ref.shape[0] // nslab
        for ii in range(nslab):
            orows = slice(ii * ob, (ii + 1) * ob)
            contrib = jnp.dot(vt_ref[orows, :], c_prev[...], preferred_element_type=F32)
            acc_ref[orows, :] = jnp.where(first, contrib, acc_ref[orows, :] + contrib)
            bits = lax.bitcast_convert_type(contrib[0:1, :], jnp.int32)
            zero = lax.shift_right_logical(lax.shift_right_logical(bits, 16), 16).astype(F32)
            i = e * nslab + ii
            cnt_rows = [(cnt_ref[hd, pl.ds(i, 1), :] + zero).astype(BF16) for hd in range(PEER_HEADS)]
            w0_rows = [w0_ref[hd, pl.ds(i, 1), :].astype(BF16) for hd in range(PEER_HEADS)]
            for c in range(tn // lc):
                ls = slice(c * lc, (c + 1) * lc)
                for jb in range(nk // rb):
                    js = slice(jb * rb, (jb + 1) * rb)
                    rs = slice(ii * nk + jb * rb, ii * nk + (jb + 1) * rb)
                    g = None
                    for hd in range(PEER_HEADS):
                        term = jnp.where(r1_ref[hd, js, ls] < cnt_rows[hd][:, ls], w1_ref[hd, js, ls],
                                         jnp.zeros((), BF16)) * w0_rows[hd][:, ls]
                        g = term if g is None else g + term
                    g_ref[rs, ls] = g
        for ii in range(nslab):
            rows = slice(ii * nk, (ii + 1) * nk)
            a = jnp.dot(u_ref[rows, :], hn_ref[...], preferred_element_type=F32)
            c_new[rows, :] = g_ref[rows, :] * _gelu(a).astype(BF16)

    @pl.when(s % 2 == 0)
    def _():
        stages(c0_ref, c1_ref)

    @pl.when(s % 2 == 1)
    def _():
        stages(c1_ref, c0_ref)

    @pl.when((p1 >= 0) & (p1 % ne == ne - 1))
    def _():
        y_ref[...] = (ht_ref[...] + acc_ref[...]).T


def _peer_dense(hn, u, vt, r1, w1, cnt, w0, ht, *, tn, te, lc, rb):
    d, n = hn.shape
    ne = u.shape[0] // te
    nk = r1.shape[1]
    nsteps = (n // tn) * ne
    last = nsteps - 1
    pair = lambda s, lag: jnp.clip(s - lag, 0, last)
    stat = pl.BlockSpec((PEER_HEADS, nk, tn), lambda s: (0, 0, pair(s, 0) // ne))
    return pl.pallas_call(
        functools.partial(_peer_dense_kernel, nk=nk, te=te, tn=tn, lc=lc, rb=min(rb, nk), ne=ne, nsteps=nsteps),
        grid=(nsteps + 1,),
        in_specs=[pl.BlockSpec((d, tn), lambda s: (0, pair(s, 0) // ne)),
                  pl.BlockSpec((te, d), lambda s: (pair(s, 0) % ne, 0)),
                  pl.BlockSpec((d, te), lambda s: (0, pair(s, 1) % ne)),
                  stat, stat, stat, stat,
                  pl.BlockSpec((d, tn), lambda s: (0, pair(s, 1) // ne))],
        out_specs=pl.BlockSpec((tn, d), lambda s: (pair(s, 1) // ne, 0)),
        out_shape=jax.ShapeDtypeStruct((n, d), F32),
        scratch_shapes=[pltpu.VMEM((d, tn), F32), pltpu.VMEM((te, tn), BF16), pltpu.VMEM((te, tn), BF16),
                        pltpu.VMEM((te, tn), BF16)],
        compiler_params=_cparams("arbitrary"),
        name="peer_dense",
    )(hn, u, vt, r1, w1, cnt, w0, ht)


def _peer(ht, g_col, wq_t, keys, u, vt, *, tn_sel, tn, te, lc, rb):
    hn, r1, w1, cnt, w0 = _peer_select(ht, g_col, wq_t, keys, tn_sel)
    return _peer_dense(hn, u, vt, r1, w1, cnt, w0, ht, tn=tn, te=te, lc=lc, rb=rb)


TOKEN_TILE = 512
EXPERT_TILE = 1024
GATE_LANES = 256
GATE_ROWS = 64


def _peer_tiles(n_tokens, n_keys):
    tn = min(TOKEN_TILE, n_tokens)
    return dict(tn_sel=tn, tn=tn, te=min(EXPERT_TILE, n_keys * n_keys), lc=min(GATE_LANES, tn), rb=GATE_ROWS)


def kernel(x_prompt, x_sample, cache_a_k, cache_a_v, cache_b_k, cache_b_v, norm_attn, w_in, g_qa, g_ka, g_qb, g_kb, sinks, w_o, norm_ffn, peer_wq, peer_keys, peer_u, peer_v):
    b, l, d = x_prompt.shape
    s, ns, _ = x_sample.shape
    assert ns == NEW and w_in.shape[0] == 1
    la, lb = cache_a_k.shape[2], cache_b_k.shape[2]
    w = A_WIDTH
    perm = _qb_perm()

    wl = w_in[0]
    w_all = jnp.concatenate([wl[:, :3 * w], wl[:, 3 * w:4 * w][:, perm], wl[:, 4 * w:]], axis=1).astype(BF16)
    seg = (jnp.arange(w)[:, None] // HEAD_DIM == jnp.arange(w)[None, :] // HEAD_DIM).astype(BF16)
    t8 = lambda g: jnp.tile(g, A_HEADS)[None, :]
    gains = (t8(g_qa[0]), t8(g_ka[0]), t8(g_qb[0]), jnp.tile(g_kb[0], B_KV_HEADS)[None, :])
    sb = _alibi_slopes(B_HEADS)
    slopes_perm = [sb[(h % 2) * B_GROUP + h // 2] for h in range(B_HEADS)]
    sinks_perm = jnp.stack([sinks[0, (h % 2) * B_GROUP + h // 2] for h in range(B_HEADS)])
    wo_a = w_o[0, :w, :].astype(BF16)
    wo_b = w_o[0, w:, :][perm, :].astype(BF16)
    g_col = norm_ffn[0][:, None]
    wq_t = peer_wq[0].T.astype(BF16)
    nk = peer_keys.shape[3]
    keys = peer_keys[0].reshape(2 * PEER_HEADS, nk, peer_keys.shape[4]).astype(BF16)
    u = peer_u[0].astype(BF16)
    vt = peer_v[0].T.astype(BF16)

    xp = x_prompt.reshape(b * l, d)
    tm_p = min(TOKEN_TILE, l)
    qa4, ka4, va4, qb4, kat, vat, kb, vb, kbt, vbt = _project(xp, norm_attn, w_all, seg, *gains, tm_p, b)
    seq4 = lambda t: t.reshape(t.shape[0], b, l, LANES)
    oa4 = _mixer(seq4(qa4), seq4(ka4), seq4(va4), dils=[dl for (_, dl) in A_PATTERNS],
                 slopes=_alibi_slopes(A_HEADS))
    ob4 = _mixer(seq4(qb4), kb.reshape(1, b, l, LANES), vb.reshape(1, b, l, LANES), dils=[1],
                 slopes=slopes_perm, sinks=sinks_perm)
    ht_p = _outproj(xp, oa4.reshape(NSLAB, b * l, LANES), ob4.reshape(NSLAB, b * l, LANES), wo_a, wo_b, tm_p)
    y_p = _peer(ht_p, g_col, wq_t, keys, u, vt, **_peer_tiles(b * l, nk))
    na = min(la, l)
    nb = min(lb, l)
    def tail(t, heads, keep):
        t = t[:, :, l - keep:].reshape(b, heads, HEAD_DIM, keep)
        return jnp.transpose(t, (0, 3, 1, 2))[None]
    pak, pav = tail(kat, A_HEADS, na), tail(vat, A_HEADS, na)
    pbk, pbv = tail(kbt, B_KV_HEADS, nb), tail(vbt, B_KV_HEADS, nb)

    xs = x_sample.reshape(s * ns, d)
    tm_s = min(TOKEN_TILE, s * ns)
    qa4, _, _, qb4, kat, vat, _, _, kbt, vbt = _project(xs, norm_attn, w_all, seg, *gains, tm_s, 1)
    wide = lambda t4: jnp.transpose(t4, (1, 0, 2)).reshape(s, ns, w)
    slabs = lambda t: jnp.transpose(t.reshape(s * ns, NSLAB, LANES), (1, 0, 2))
    to_t = lambda c: jnp.transpose(c[0], (0, 2, 3, 1)).reshape(s, w, la)
    from_t = lambda t: jnp.transpose(t.reshape(s, A_HEADS, HEAD_DIM, la), (0, 3, 1, 2))[None]
    new_t = lambda t: jnp.transpose(t.reshape(t.shape[1], s, ns), (1, 0, 2))
    sak_t, sav_t, oa = _sample_at(wide(qa4), new_t(kat), new_t(vat), to_t(cache_a_k), to_t(cache_a_v))
    sak, sav = from_t(sak_t), from_t(sav_t)
    to_tb = lambda c: jnp.transpose(c[0], (0, 2, 3, 1)).reshape(s, LANES, lb)
    from_tb = lambda t: jnp.transpose(t.reshape(s, B_KV_HEADS, HEAD_DIM, lb), (0, 3, 1, 2))[None]
    sbk_t, sbv_t, ob = _sample_b(sinks_perm, slopes_perm, wide(qb4), new_t(kbt), new_t(vbt),
                                 to_tb(cache_b_k), to_tb(cache_b_v))
    ht_s = _outproj(xs, slabs(oa), slabs(ob), wo_a, wo_b, tm_s)
    y_s = _peer(ht_s, g_col, wq_t, keys, u, vt, **_peer_tiles(s * ns, nk))

    return (y_p.reshape(b, l, d), y_s.reshape(s, ns, d), pak, pav, pbk, pbv,
            sak, sav, from_tb(sbk_t), from_tb(sbv_t))
```

```python
import functools
import math

import jax
import jax.numpy as jnp
from jax import lax
from jax.experimental import pallas as pl
from jax.experimental.pallas import tpu as pltpu

HEAD_DIM = 64
A_HEADS = 8
B_HEADS = 8
B_KV_HEADS = 2
B_GROUP = B_HEADS // B_KV_HEADS
A_PATTERNS = ((128, 1), (512, 4), (2048, 16))
BAND = 128
A_WIDTH = A_HEADS * HEAD_DIM
PEER_HEADS = 8
PEER_TOPK = 16
NORM_EPS = 1e-6
NEG = -1e30
SCALE = HEAD_DIM ** -0.5
LANES = 128
VMEM_LIMIT_BYTES = 56 * 1024 * 1024

BF16 = jnp.bfloat16
F32 = jnp.float32


def _alibi_slopes(n):
    return [2.0 ** (-8.0 * (i + 1) / n) for i in range(n)]


def _cparams(*sem):
    return pltpu.CompilerParams(dimension_semantics=sem, vmem_limit_bytes=VMEM_LIMIT_BYTES)


def _head_rms(h, seg, gain):
    sq = h * h
    hi = sq.astype(BF16)
    lo = (sq - hi.astype(F32)).astype(BF16)
    ms = (jnp.dot(hi, seg, preferred_element_type=F32)
          + jnp.dot(lo, seg, preferred_element_type=F32)) * (1.0 / HEAD_DIM)
    return h * lax.rsqrt(ms + NORM_EPS) * gain


NSLAB = A_WIDTH // LANES


def _proj_kernel(x_ref, g_ref, w_ref, seg_ref, gqa_ref, gka_ref, gqb_ref, gkb_ref,
                 qa4_ref, ka4_ref, va4_ref, qb4_ref, kat_ref, vat_ref, kb_ref, vb_ref, kbt_ref, vbt_ref):
    x = x_ref[...]
    ms = jnp.mean(x * x, axis=-1, keepdims=True)
    xn = (x * lax.rsqrt(ms + NORM_EPS) * g_ref[...]).astype(BF16)
    h = jnp.dot(xn, w_ref[...], preferred_element_type=F32)
    seg = seg_ref[...]
    w = A_WIDTH
    qa = _head_rms(h[:, 0:w], seg, gqa_ref[...])
    ka = _head_rms(h[:, w:2 * w], seg, gka_ref[...])
    va = h[:, 2 * w:3 * w]
    qb = _head_rms(h[:, 3 * w:4 * w], seg, gqb_ref[...])
    kb = _head_rms(h[:, 4 * w:4 * w + LANES], seg[0:LANES, 0:LANES], gkb_ref[...])
    vb = h[:, 4 * w + LANES:4 * w + 2 * LANES]
    kb_ref[...] = kb
    vb_ref[...] = vb
    kat_ref[...] = ka.T
    vat_ref[...] = va.T
    kbt_ref[...] = kb.T
    vbt_ref[...] = vb.T
    for s in range(NSLAB):
        sl = slice(s * LANES, (s + 1) * LANES)
        qa4_ref[s] = qa[:, sl]
        ka4_ref[s] = ka[:, sl]
        va4_ref[s] = va[:, sl]
        qb4_ref[s] = qb[:, sl]


def _project(x2d, norm_g, w_bf16, seg, gqa, gka, gqb, gkb, tm, nseq):
    n, d = x2d.shape
    nc = w_bf16.shape[1]
    w = A_WIDTH
    l = n // nseq
    tps = l // tm
    row = lambda i: (i, 0)
    col = lambda i: (i // tps, 0, i % tps)
    slab = lambda i: (0, i, 0)
    const = lambda i: (0, 0)
    outs = ([jax.ShapeDtypeStruct((NSLAB, n, LANES), F32)] * 4 + [jax.ShapeDtypeStruct((nseq, w, l), F32)] * 2
            + [jax.ShapeDtypeStruct((n, LANES), F32)] * 2 + [jax.ShapeDtypeStruct((nseq, LANES, l), F32)] * 2)
    return pl.pallas_call(
        _proj_kernel,
        grid=(n // tm,),
        in_specs=[pl.BlockSpec((tm, d), row), pl.BlockSpec((1, d), const), pl.BlockSpec((d, nc), const),
                  pl.BlockSpec((w, w), const), pl.BlockSpec((1, w), const), pl.BlockSpec((1, w), const),
                  pl.BlockSpec((1, w), const), pl.BlockSpec((1, LANES), const)],
        out_specs=([pl.BlockSpec((NSLAB, tm, LANES), slab)] * 4 + [pl.BlockSpec((None, w, tm), col)] * 2
                   + [pl.BlockSpec((tm, LANES), row)] * 2 + [pl.BlockSpec((None, LANES, tm), col)] * 2),
        out_shape=outs,
        compiler_params=_cparams("parallel"),
        name="qkv_proj",
    )(x2d, norm_g, w_bf16, seg, gqa, gka, gqb, gkb)


MIXER_UNROLL = 8


def _pick(idx, values):
    out = jnp.float32(values[-1])
    for i in range(len(values) - 2, -1, -1):
        out = jnp.where(idx == i, jnp.float32(values[i]), out)
    return out


def _mixer_kernel(*refs, dils, slopes, with_sink, seq):
    if with_sink:
        sink_ref, refs = refs[0], refs[1:]
    q_ref, k_ref, v_ref, o_ref, kpad, vpad, bias_ref = refs[:7]
    ob_ref, lse_ref = refs[7:9] if len(dils) > 1 else (None, None)
    hp = pl.program_id(1)
    pad = BAND * max(dils)

    @pl.when((pl.program_id(0) == 0) & (hp == 0))
    def _():
        kpad[0:pad, :] = jnp.zeros((pad, LANES), F32)
        vpad[0:pad, :] = jnp.zeros((pad, LANES), F32)

    kpad[pad:pad + seq, :] = k_ref[0, 0]
    vpad[pad:pad + seq, :] = v_ref[0, 0]

    r = lax.broadcasted_iota(jnp.int32, (BAND, 2 * BAND), 0)
    c = lax.broadcasted_iota(jnp.int32, (BAND, 2 * BAND), 1)
    dist = r + BAND - c
    valid = (dist >= 0) & (dist <= BAND)
    for bi, dil in enumerate(dils):
        distf = (dist * dil).astype(F32)
        for e in range(2):
            slope = _pick(hp, [slopes[2 * g + e] for g in range(NSLAB)])
            bias_ref[bi, e] = jnp.where(valid, -slope * distf, NEG)

    lane = lax.broadcasted_iota(jnp.int32, (BAND, LANES), 1)
    hi = lane >= HEAD_DIM
    col = lax.broadcasted_iota(jnp.int32, (1, 2 * BAND), 1)
    prev_cols = (col < BAND).astype(F32)
    nt = (((1,), (1,)), ((), ()))

    for bi, dil in enumerate(dils):
        nblk = seq // (dil * BAND)

        def block(t, bi=bi, dil=dil, nblk=nblk):
            res = t // nblk
            j = t % nblk
            base = res + dil * BAND * j
            if dil == 1:
                base = pl.multiple_of(base, BAND)
                rows = pl.ds(base, BAND)
                win = pl.ds(pl.multiple_of(pad + base - BAND, BAND), 2 * BAND)
            else:
                rows = pl.ds(base, BAND, stride=dil)
                win = pl.ds(pad + base - dil * BAND, 2 * BAND, stride=dil)
            qs = q_ref[0, 0, rows, :]
            kw = kpad[win, :].astype(BF16)
            vw = vpad[win, :].astype(BF16)
            pen = jnp.where(j == 0, NEG, 0.0) * prev_cols
            outs, lses = [], []
            for e in range(2):
                qm = jnp.where(hi == bool(e), qs, 0.0).astype(BF16)
                s = lax.dot_general(qm, kw, nt, preferred_element_type=F32)
                s = s * SCALE + bias_ref[bi, e] + pen
                m = jnp.max(s, axis=-1, keepdims=True)
                if with_sink:
                    sink = sink_ref[2 * hp + e]
                    m = jnp.maximum(m, sink)
                p = jnp.exp(s - m)
                den = jnp.sum(p, axis=-1, keepdims=True)
                if with_sink:
                    den = den + jnp.exp(sink - m)
                o = jnp.dot(p.astype(BF16), vw, preferred_element_type=F32)
                outs.append(o / den)
                lses.append(m + jnp.log(den))
            if len(dils) == 1:
                o_ref[0, 0, rows, :] = jnp.where(hi, outs[1], outs[0])
            else:
                ob_ref[bi, rows, :] = jnp.where(hi, outs[1], outs[0])
                lse_ref[bi, rows, :] = jnp.where(hi, lses[1], lses[0])

        def blocks(t, carry, block=block):
            for uu in range(MIXER_UNROLL):
                block(t * MIXER_UNROLL + uu)
            return carry

        assert (dil * nblk) % MIXER_UNROLL == 0
        lax.fori_loop(0, dil * nblk // MIXER_UNROLL, blocks, 0)

    if len(dils) > 1:
        def merge(t, carry):
            rows = pl.ds(pl.multiple_of(t * BAND, BAND), BAND)
            ls = [lse_ref[bi, rows, :] for bi in range(len(dils))]
            mx = functools.reduce(jnp.maximum, ls)
            ws = [jnp.exp(l - mx) for l in ls]
            num = sum(w * ob_ref[bi, rows, :] for bi, w in enumerate(ws))
            o_ref[0, 0, rows, :] = num / sum(ws)
            return carry

        lax.fori_loop(0, seq // BAND, merge, 0)


def _mixer(q4, k4, v4, *, dils, slopes, sinks=None):
    nslab, b, l, _ = q4.shape
    assert l % (BAND * max(dils)) == 0
    shared = k4.shape[0] == 1
    qmap = lambda bi, hp: (hp, bi, 0, 0)
    kmap = (lambda bi, hp: (0, bi, 0, 0)) if shared else qmap
    blk = (1, 1, l, LANES)
    in_specs = [pl.BlockSpec(blk, qmap), pl.BlockSpec(blk, kmap), pl.BlockSpec(blk, kmap)]
    args = [q4, k4, v4]
    if sinks is not None:
        in_specs = [pl.BlockSpec(memory_space=pltpu.SMEM)] + in_specs
        args = [sinks] + args
    pad = BAND * max(dils)
    scratch = [pltpu.VMEM((pad + l, LANES), F32), pltpu.VMEM((pad + l, LANES), F32),
               pltpu.VMEM((len(dils), 2, BAND, 2 * BAND), F32)]
    if len(dils) > 1:
        scratch += [pltpu.VMEM((len(dils), l, LANES), F32), pltpu.VMEM((len(dils), l, LANES), F32)]
    return pl.pallas_call(
        functools.partial(_mixer_kernel, dils=tuple(dils), slopes=slopes, with_sink=sinks is not None, seq=l),
        grid=(b, nslab),
        in_specs=in_specs,
        out_specs=pl.BlockSpec(blk, qmap),
        out_shape=jax.ShapeDtypeStruct(q4.shape, F32),
        scratch_shapes=scratch,
        compiler_params=_cparams("arbitrary", "arbitrary"),
        name="mixer_b" if shared else "mixer_a",
    )(*args)


def _qb_perm():
    idx = []
    for g in range(B_GROUP):
        for kv in range(B_KV_HEADS):
            base = (kv * B_GROUP + g) * HEAD_DIM
            idx.extend(range(base, base + HEAD_DIM))
    return jnp.asarray(idx, jnp.int32)


NEW = 8
HQ = A_HEADS * NEW


def _block_diag_rows(q8, width):
    rep = jnp.concatenate([q8] * A_HEADS, axis=0)
    r = lax.broadcasted_iota(jnp.int32, (HQ, width), 0)
    c = lax.broadcasted_iota(jnp.int32, (HQ, width), 1)
    return jnp.where(r // NEW == c // HEAD_DIM, rep, 0.0)


def _block_diag_rows_keep(x, width):
    r = lax.broadcasted_iota(jnp.int32, (HQ, width), 0)
    c = lax.broadcasted_iota(jnp.int32, (HQ, width), 1)
    return jnp.where(r // NEW == c // HEAD_DIM, x, 0.0)


ROLL_ROWS = 64


def _sample_at_kernel(q_ref, kn_ref, vn_ref, ck_ref, cv_ref, nk_ref, nv_ref, o_ref, bias_ref, bn_ref, knb, vnb,
                      *, la, slopes):
    w = A_WIDTH
    nbr = len(A_PATTERNS)

    @pl.when(pl.program_id(0) == 0)
    def _():
        r = lax.broadcasted_iota(jnp.int32, (HQ, la), 0)
        c = lax.broadcasted_iota(jnp.int32, (HQ, la), 1)
        dist = (r % NEW) + la - c
        slope = jnp.zeros((HQ, la), F32)
        rn = lax.broadcasted_iota(jnp.int32, (HQ, LANES), 0)
        cn = lax.broadcasted_iota(jnp.int32, (HQ, LANES), 1)
        dn = (rn % NEW) - cn
        sl = jnp.zeros((HQ, LANES), F32)
        for h in range(A_HEADS):
            slope = jnp.where(r // NEW == h, slopes[h], slope)
            sl = jnp.where(rn // NEW == h, slopes[h], sl)
        for bi, (wdw, dil) in enumerate(A_PATTERNS):
            bias_ref[bi] = jnp.where((dist <= wdw) & (dist % dil == 0), -slope * dist.astype(F32), NEG)
            bn_ref[bi] = jnp.where((dn >= 0) & (dn % dil == 0) & (cn < NEW), -sl * dn.astype(F32), NEG)
        knb[...] = jnp.zeros_like(knb)
        vnb[...] = jnp.zeros_like(vnb)

    knb[:, 0:NEW] = kn_ref[0]
    vnb[:, 0:NEW] = vn_ref[0]

    qbd = _block_diag_rows(q_ref[0], w).astype(BF16)
    nt = (((1,), (1,)), ((), ()))
    s_c = jnp.dot(qbd, ck_ref[0].astype(BF16), preferred_element_type=F32) * SCALE
    s_n = jnp.dot(qbd, knb[...].astype(BF16), preferred_element_type=F32) * SCALE
    ms, dens, pcs, pns = [], [], [], []
    for bi in range(nbr):
        sc = s_c + bias_ref[bi]
        sn = s_n + bn_ref[bi]
        m = jnp.maximum(jnp.max(sc, axis=-1, keepdims=True), jnp.max(sn, axis=-1, keepdims=True))
        pc = jnp.exp(sc - m)
        pn = jnp.exp(sn - m)
        dens.append(jnp.sum(pc, axis=-1, keepdims=True) + jnp.sum(pn, axis=-1, keepdims=True))
        ms.append(m)
        pcs.append(pc.astype(BF16))
        pns.append(pn.astype(BF16))
    nums = (lax.dot_general(jnp.concatenate(pcs, axis=0), cv_ref[0].astype(BF16), nt, preferred_element_type=F32)
            + lax.dot_general(jnp.concatenate(pns, axis=0), vnb[...].astype(BF16), nt, preferred_element_type=F32))
    mx = functools.reduce(jnp.maximum, ms)
    den = 0.0
    num = 0.0
    for bi in range(nbr):
        wt = jnp.exp(ms[bi] - mx)
        den = den + dens[bi] * wt
        num = num + nums[bi * HQ:(bi + 1) * HQ, :] * wt
    out = _block_diag_rows_keep(num / den, w)
    o_ref[0] = sum(out[h * NEW:(h + 1) * NEW, :] for h in range(A_HEADS))

    lane = lax.broadcasted_iota(jnp.int32, (ROLL_ROWS, LANES), 1)
    tail = lane >= LANES - NEW
    for src, newb, dst in ((ck_ref, knb, nk_ref), (cv_ref, vnb, nv_ref)):
        for r0 in range(0, w, ROLL_ROWS):
            rows = slice(r0, r0 + ROLL_ROWS)
            rolled = pltpu.roll(src[0, rows, :], la - NEW, 1)
            dst[0, rows, 0:la - LANES] = rolled[:, 0:la - LANES]
            fresh = pltpu.roll(newb[rows, :], LANES - NEW, 1)
            dst[0, rows, la - LANES:la] = jnp.where(tail, fresh, rolled[:, la - LANES:la])


def _sample_at(q, kn_t, vn_t, ck_t, cv_t):
    s, w, la = ck_t.shape
    assert la >= max(wd for (wd, _) in A_PATTERNS) and la % LANES == 0
    seq = lambda i: (i, 0, 0)
    big = pl.BlockSpec((1, w, la), seq)
    new = pl.BlockSpec((1, w, NEW), seq)
    qspec = pl.BlockSpec((1, NEW, w), seq)
    nbr = len(A_PATTERNS)
    return pl.pallas_call(
        functools.partial(_sample_at_kernel, la=la, slopes=_alibi_slopes(A_HEADS)),
        grid=(s,),
        in_specs=[qspec, new, new, big, big],
        out_specs=[big, big, qspec],
        out_shape=[jax.ShapeDtypeStruct(ck_t.shape, F32), jax.ShapeDtypeStruct(cv_t.shape, F32),
                   jax.ShapeDtypeStruct(q.shape, F32)],
        scratch_shapes=[pltpu.VMEM((nbr, HQ, la), F32), pltpu.VMEM((nbr, HQ, LANES), F32),
                        pltpu.VMEM((w, LANES), F32), pltpu.VMEM((w, LANES), F32)],
        compiler_params=_cparams("arbitrary"),
        name="sample_attn_at",
    )(q, kn_t, vn_t, ck_t, cv_t)


def _sample_b_kernel(sink_ref, q_ref, kn_ref, vn_ref, ck_ref, cv_ref, nk_ref, nv_ref, o_ref,
                     bc_ref, bn_ref, sk_ref, knb, vnb, *, lb, nseq, slopes):
    @pl.when(pl.program_id(0) == 0)
    def _():
        r = lax.broadcasted_iota(jnp.int32, (HQ, lb), 0)
        c = lax.broadcasted_iota(jnp.int32, (HQ, lb), 1)
        dist = (r % NEW) + lb - c
        slope = jnp.zeros((HQ, lb), F32)
        sink = jnp.zeros((HQ, LANES), F32)
        rs = lax.broadcasted_iota(jnp.int32, (HQ, LANES), 0)
        for h in range(B_HEADS):
            slope = jnp.where(r // NEW == h, slopes[h], slope)
            sink = jnp.where(rs // NEW == h, sink_ref[h], sink)
        bc_ref[...] = jnp.where(dist <= BAND, -slope * dist.astype(F32), NEG)
        cn = lax.broadcasted_iota(jnp.int32, (HQ, LANES), 1)
        dn = (rs % NEW) - cn
        sl = jnp.zeros((HQ, LANES), F32)
        for h in range(B_HEADS):
            sl = jnp.where(rs // NEW == h, slopes[h], sl)
        bn_ref[...] = jnp.where((dn >= 0) & (cn < NEW), -sl * dn.astype(F32), NEG)
        sk_ref[...] = sink
        knb[...] = jnp.zeros_like(knb)
        vnb[...] = jnp.zeros_like(vnb)

    nt = (((1,), (1,)), ((), ()))
    rr = lax.broadcasted_iota(jnp.int32, (HQ, LANES), 0)
    cc = lax.broadcasted_iota(jnp.int32, (HQ, LANES), 1)
    own = ((rr // NEW) % B_KV_HEADS) == (cc // HEAD_DIM)
    sink = sk_ref[:, 0:1]
    tail = cc >= LANES - NEW
    for t in range(nseq):
        knb[:, 0:NEW] = kn_ref[t]
        vnb[:, 0:NEW] = vn_ref[t]
        for src, newb, dst in ((ck_ref, knb, nk_ref), (cv_ref, vnb, nv_ref)):
            for r0 in range(0, LANES, HQ):
                rows = slice(r0, r0 + HQ)
                dst[t, rows, :] = jnp.where(tail, pltpu.roll(newb[rows, :], LANES - NEW, 1),
                                            pltpu.roll(src[t, rows, :], lb - NEW, 1))
        q = q_ref[t]
        rows = []
        for g in range(B_GROUP):
            for kv in range(B_KV_HEADS):
                rows.append(q[:, g * LANES:(g + 1) * LANES])
        qbd = jnp.where(own, jnp.concatenate(rows, axis=0), 0.0).astype(BF16)
        sc = jnp.dot(qbd, ck_ref[t].astype(BF16), preferred_element_type=F32) * SCALE + bc_ref[...]
        sn = jnp.dot(qbd, knb[...].astype(BF16), preferred_element_type=F32) * SCALE + bn_ref[...]
        m = jnp.maximum(jnp.maximum(jnp.max(sc, axis=-1, keepdims=True), jnp.max(sn, axis=-1, keepdims=True)), sink)
        pc = jnp.exp(sc - m)
        pn = jnp.exp(sn - m)
        den = jnp.sum(pc, axis=-1, keepdims=True) + jnp.sum(pn, axis=-1, keepdims=True) + jnp.exp(sink - m)
        o = (lax.dot_general(pc.astype(BF16), cv_ref[t].astype(BF16), nt, preferred_element_type=F32)
             + lax.dot_general(pn.astype(BF16), vnb[...].astype(BF16), nt, preferred_element_type=F32)) / den
        o = jnp.where(own, o, 0.0)
        for g in range(B_GROUP):
            base = g * B_KV_HEADS * NEW
            o_ref[t, :, g * LANES:(g + 1) * LANES] = o[base:base + NEW, :] + o[base + NEW:base + 2 * NEW, :]


def _sample_b(sinks_perm, slopes_perm, q, kn, vn, ck, cv, nseq=8):
    s, kw, lb = ck.shape
    assert lb == BAND and kw == LANES and lb == LANES and s % nseq == 0
    seq = lambda i: (i, 0, 0)
    return pl.pallas_call(
        functools.partial(_sample_b_kernel, lb=lb, nseq=nseq, slopes=slopes_perm),
        grid=(s // nseq,),
        in_specs=[pl.BlockSpec(memory_space=pltpu.SMEM), pl.BlockSpec((nseq, NEW, A_WIDTH), seq),
                  pl.BlockSpec((nseq, kw, NEW), seq), pl.BlockSpec((nseq, kw, NEW), seq),
                  pl.BlockSpec((nseq, kw, lb), seq), pl.BlockSpec((nseq, kw, lb), seq)],
        out_specs=[pl.BlockSpec((nseq, kw, lb), seq), pl.BlockSpec((nseq, kw, lb), seq),
                   pl.BlockSpec((nseq, NEW, A_WIDTH), seq)],
        out_shape=[jax.ShapeDtypeStruct(ck.shape, F32), jax.ShapeDtypeStruct(cv.shape, F32),
                   jax.ShapeDtypeStruct(q.shape, F32)],
        scratch_shapes=[pltpu.VMEM((HQ, lb), F32), pltpu.VMEM((HQ, LANES), F32), pltpu.VMEM((HQ, LANES), F32),
                        pltpu.VMEM((LANES, kw), F32), pltpu.VMEM((LANES, kw), F32)],
        compiler_params=_cparams("arbitrary"),
        name="sample_attn_b",
    )(sinks_perm, q, kn, vn, ck, cv)


def _outproj_kernel(x_ref, oa_ref, ob_ref, wa_ref, wb_ref, ht_ref):
    oa = jnp.concatenate([oa_ref[s] for s in range(NSLAB)], axis=1).astype(BF16)
    ob = jnp.concatenate([ob_ref[s] for s in range(NSLAB)], axis=1).astype(BF16)
    h = (x_ref[...] + jnp.dot(oa, wa_ref[...], preferred_element_type=F32)
         + jnp.dot(ob, wb_ref[...], preferred_element_type=F32))
    ht_ref[...] = h.T


def _outproj(x2d, oa4, ob4, wa, wb, tm):
    n, d = x2d.shape
    w = A_WIDTH
    row = lambda i: (i, 0)
    const = lambda i: (0, 0)
    slab = pl.BlockSpec((NSLAB, tm, LANES), lambda i: (0, i, 0))
    return pl.pallas_call(
        _outproj_kernel,
        grid=(n // tm,),
        in_specs=[pl.BlockSpec((tm, d), row), slab, slab, pl.BlockSpec((w, d), const), pl.BlockSpec((w, d), const)],
        out_specs=pl.BlockSpec((d, tm), lambda i: (0, i)),
        out_shape=jax.ShapeDtypeStruct((d, n), F32),
        compiler_params=_cparams("parallel"),
        name="out_proj",
    )(x2d, oa4, ob4, wa, wb)


SUBLANES = 8


def _merge_exchange(n):
    pairs = []
    t = max(1, math.ceil(math.log2(n)))
    p = 1 << (t - 1)
    while p > 0:
        q, r, d = 1 << (t - 1), 0, p
        while d > 0:
            pairs.extend((i, i + d) for i in range(n - d) if (i & p) == r)
            d, q, r = q - p, q >> 1, p
        p >>= 1
    return pairs


def _vmax(a, b):
    if a is None:
        return b
    if b is None:
        return a
    return jnp.maximum(a, b)


def _vmin(a, b):
    if a is None or b is None:
        return None
    return jnp.minimum(a, b)


def _exchange(x, i, j):
    x[i], x[j] = _vmax(x[i], x[j]), _vmin(x[i], x[j])


def _top16(tiles):
    x = list(tiles) + [None] * (PEER_TOPK - len(tiles))
    for (i, j) in _merge_exchange(len(tiles)):
        _exchange(x, i, j)
    for shift in (4, 2, 1):
        y = [None if v is None else pltpu.roll(v, shift, 0) for v in x]
        x = [_vmax(x[k], y[PEER_TOPK - 1 - k]) for k in range(PEER_TOPK)]
        for d in (8, 4, 2, 1):
            for i in range(PEER_TOPK):
                if not i & d:
                    _exchange(x, i, i + d)
    return x


def _count_prefix(pred, vals):
    sel = jnp.where
    c16 = pred(vals[15])
    c8 = pred(vals[7])
    c4 = pred(sel(c8, vals[11], vals[3]))
    c2 = pred(sel(c8, sel(c4, vals[13], vals[9]), sel(c4, vals[5], vals[1])))
    c1 = pred(sel(c8, sel(c4, sel(c2, vals[14], vals[12]), sel(c2, vals[10], vals[8])),
                  sel(c4, sel(c2, vals[6], vals[4]), sel(c2, vals[2], vals[0]))))
    lo = sel(c8, 8.0, 0.0) + sel(c4, 4.0, 0.0) + sel(c2, 2.0, 0.0) + sel(c1, 1.0, 0.0)
    return sel(c16, float(PEER_TOPK), lo)


def _rows_sum(x):
    for shift in (4, 2, 1):
        x = x + pltpu.roll(x, shift, 0)
    return x


def _peer_select_kernel(ht_ref, g_ref, wq_ref, keys_ref,
                        hn_ref, r1_ref, w1_ref, cnt_ref, w0_ref, s_ref, *, nk, tn):
    h = ht_ref[...]
    ms = jnp.mean(h * h, axis=0, keepdims=True)
    hn = (h * lax.rsqrt(ms + NORM_EPS) * g_ref[...]).astype(BF16)
    hn_ref[...] = hn
    q = jnp.dot(wq_ref[...], hn, preferred_element_type=F32)
    half = q.shape[0] // (2 * PEER_HEADS)
    for k in range(2 * PEER_HEADS):
        qk = q[k * half:(k + 1) * half, :].astype(BF16)
        s_ref[k] = jnp.dot(keys_ref[k], qk, preferred_element_type=F32)
    nchunk = tn // LANES
    ntile = nk // SUBLANES
    sub = lax.broadcasted_iota(jnp.int32, (SUBLANES, LANES), 0)

    def pack(vals):
        out = vals[-1]
        for r in range(len(vals) - 2, -1, -1):
            out = jnp.where(sub == r, vals[r], out)
        return out

    def head_chunk(t, carry):
        hd = t // nchunk
        lanes = pl.ds(pl.multiple_of((t % nchunk) * LANES, LANES), LANES)
        rows = [slice(k * SUBLANES, (k + 1) * SUBLANES) for k in range(ntile)]
        s0 = [s_ref[2 * hd, r, lanes] for r in rows]
        s1 = [s_ref[2 * hd + 1, r, lanes] for r in rows]
        v0 = _top16(s0)
        v1 = _top16(s1)
        v1lo, v1hi, v0hi = pack(v1[0:8]), pack(v1[8:16]), pack(v0[8:16])
        cands = [v0[0] + v1lo, v0[0] + v1hi] + [v0[a] + v1lo for a in range(1, 8)] + [v0hi + v1[0]]
        best = _top16(cands)
        top, tau = best[0], best[PEER_TOPK - 1]
        z = _rows_sum(sum(jnp.where(c >= tau, jnp.exp(c - top), 0.0) for c in cands))
        inv_z = 1.0 / z
        for m in range(ntile // 2):
            cnt, rk1 = [], []
            for k in (2 * m, 2 * m + 1):
                cnt.append(_count_prefix(lambda t, k=k: s0[k] + t >= tau, v1))
                rk1.append(_count_prefix(lambda t, k=k: t > s1[k], v1))
            pair = slice(2 * m * SUBLANES, (2 * m + 2) * SUBLANES)
            both = lambda f: jnp.concatenate([f(2 * m), f(2 * m + 1)], axis=0)
            cnt_ref[hd, pair, lanes] = jnp.concatenate(cnt, axis=0)
            w0_ref[hd, pair, lanes] = both(lambda k: jnp.exp(s0[k] - v0[0]) * inv_z)
            r1_ref[hd, pair, lanes] = jnp.concatenate(rk1, axis=0).astype(BF16)
            w1_ref[hd, pair, lanes] = both(lambda k: jnp.exp(s1[k] - v1[0])).astype(BF16)
        return carry

    lax.fori_loop(0, PEER_HEADS * nchunk, head_chunk, 0)


def _peer_select(ht, g_col, wq_t, keys, tn):
    d, n = ht.shape
    assert tn % LANES == 0 and n % tn == 0
    nslab, nk, half = keys.shape
    tok = lambda i: (0, i)
    tok3 = lambda i: (0, 0, i)
    stat = lambda dt: jax.ShapeDtypeStruct((PEER_HEADS, nk, n), dt)
    return pl.pallas_call(
        functools.partial(_peer_select_kernel, nk=nk, tn=tn),
        grid=(n // tn,),
        in_specs=[pl.BlockSpec((d, tn), tok), pl.BlockSpec((d, 1), lambda i: (0, 0)),
                  pl.BlockSpec(wq_t.shape, lambda i: (0, 0)), pl.BlockSpec(keys.shape, lambda i: (0, 0, 0))],
        out_specs=[pl.BlockSpec((d, tn), tok)] + [pl.BlockSpec((PEER_HEADS, nk, tn), tok3)] * 4,
        out_shape=[jax.ShapeDtypeStruct((d, n), BF16), stat(BF16), stat(BF16), stat(F32), stat(F32)],
        scratch_shapes=[pltpu.VMEM((nslab, nk, tn), F32)],
        compiler_params=_cparams("parallel"),
        name="peer_select",
    )(ht, g_col, wq_t, keys)


def _gelu(x):
    return 0.5 * x * (1.0 + lax.erf(x * (2.0 ** -0.5)))


def _peer_dense_kernel(hn_ref, u_ref, vt_ref, r1_ref, w1_ref, cnt_ref, w0_ref, ht_ref, y_ref,
                       acc_ref, c0_ref, c1_ref, g_ref, *, nk, te, tn, lc, rb, ne, nsteps):
    s = pl.program_id(0)

    @pl.when(s == 0)
    def _():
        for r in (c0_ref, c1_ref, acc_ref):
            r[...] = jnp.zeros_like(r)

    e = jnp.minimum(s, nsteps - 1) % ne
    p1 = s - 1
    first = (jnp.maximum(p1, 0) % ne) == 0
    nslab = te // nk

    def stages(c_new, c_prev):
        ob = acc_ref.shape[0] // nslab
        for i2 in range(nslab // SLAB_GROUP):
            slabs = range(i2 * SLAB_GROUP, (i2 + 1) * SLAB_GROUP)
            orows = slice(i2 * SLAB_GROUP * ob, (i2 + 1) * SLAB_GROUP * ob)
            contrib = jnp.dot(vt_ref[orows, :], c_prev[...], preferred_element_type=F32)
            acc_ref[orows, :] = jnp.where(first, contrib, acc_ref[orows, :] + contrib)
            bits = lax.bitcast_convert_type(contrib[0:1, :], jnp.int32)
            zero = lax.shift_right_logical(lax.shift_right_logical(bits, 16), 16).astype(F32)
            cnt_rows = [[(cnt_ref[hd, pl.ds(e * nslab + ii, 1), :] + zero).astype(BF16)
                         for hd in range(PEER_HEADS)] for ii in slabs]
            w0_rows = [[w0_ref[hd, pl.ds(e * nslab + ii, 1), :].astype(BF16)
                        for hd in range(PEER_HEADS)] for ii in slabs]
            for c in range(tn // lc):
                ls = slice(c * lc, (c + 1) * lc)
                for jb in range(nk // rb):
                    js = slice(jb * rb, (jb + 1) * rb)
                    gs = [None] * SLAB_GROUP
                    for hd in range(PEER_HEADS):
                        r1 = r1_ref[hd, js, ls]
                        w1 = w1_ref[hd, js, ls]
                        for k in range(SLAB_GROUP):
                            term = jnp.where(r1 < cnt_rows[k][hd][:, ls], w1, jnp.zeros((), BF16)) * w0_rows[k][hd][:, ls]
                            gs[k] = term if gs[k] is None else gs[k] + term
                    for k, ii in enumerate(slabs):
                        g_ref[ii * nk + jb * rb:ii * nk + (jb + 1) * rb, ls] = gs[k]
        for ii in range(nslab):
            rows = slice(ii * nk, (ii + 1) * nk)
            a = jnp.dot(u_ref[rows, :], hn_ref[...], preferred_element_type=F32)
            c_new[rows, :] = g_ref[rows, :] * _gelu(a).astype(BF16)

    @pl.when(s % 2 == 0)
    def _():
        stages(c0_ref, c1_ref)

    @pl.when(s % 2 == 1)
    def _():
        stages(c1_ref, c0_ref)

    @pl.when((p1 >= 0) & (p1 % ne == ne - 1))
    def _():
        y_ref[...] = (ht_ref[...] + acc_ref[...]).T


def _peer_dense(hn, u, vt, r1, w1, cnt, w0, ht, *, tn, te, lc, rb):
    d, n = hn.shape
    ne = u.shape[0] // te
    nk = r1.shape[1]
    nsteps = (n // tn) * ne
    last = nsteps - 1
    pair = lambda s, lag: jnp.clip(s - lag, 0, last)
    stat = pl.BlockSpec((PEER_HEADS, nk, tn), lambda s: (0, 0, pair(s, 0) // ne))
    return pl.pallas_call(
        functools.partial(_peer_dense_kernel, nk=nk, te=te, tn=tn, lc=lc, rb=min(rb, nk), ne=ne, nsteps=nsteps),
        grid=(nsteps + 1,),
        in_specs=[pl.BlockSpec((d, tn), lambda s: (0, pair(s, 0) // ne)),
                  pl.BlockSpec((te, d), lambda s: (pair(s, 0) % ne, 0)),
                  pl.BlockSpec((d, te), lambda s: (0, pair(s, 1) % ne)),
                  stat, stat, stat, stat,
                  pl.BlockSpec((d, tn), lambda s: (0, pair(s, 1) // ne))],
        out_specs=pl.BlockSpec((tn, d), lambda s: (pair(s, 1) // ne, 0)),
        out_shape=jax.ShapeDtypeStruct((n, d), F32),
        scratch_shapes=[pltpu.VMEM((d, tn), F32), pltpu.VMEM((te, tn), BF16), pltpu.VMEM((te, tn), BF16),
                        pltpu.VMEM((te, tn), BF16)],
        compiler_params=_cparams("arbitrary"),
        name="peer_dense",
    )(hn, u, vt, r1, w1, cnt, w0, ht)


def _peer(ht, g_col, wq_t, keys, u, vt, *, tn_sel, tn, te, lc, rb):
    hn, r1, w1, cnt, w0 = _peer_select(ht, g_col, wq_t, keys, tn_sel)
    return _peer_dense(hn, u, vt, r1, w1, cnt, w0, ht, tn=tn, te=te, lc=lc, rb=rb)


TOKEN_TILE = 512
EXPERT_TILE = 1024
GATE_LANES = 256
GATE_ROWS = 64
SLAB_GROUP = 1


def _peer_tiles(n_tokens, n_keys):
    tn = min(TOKEN_TILE, n_tokens)
    return dict(tn_sel=tn, tn=tn, te=min(EXPERT_TILE, n_keys * n_keys), lc=min(GATE_LANES, tn), rb=GATE_ROWS)


def kernel(x_prompt, x_sample, cache_a_k, cache_a_v, cache_b_k, cache_b_v, norm_attn, w_in, g_qa, g_ka, g_qb, g_kb, sinks, w_o, norm_ffn, peer_wq, peer_keys, peer_u, peer_v):
    b, l, d = x_prompt.shape
    s, ns, _ = x_sample.shape
    assert ns == NEW and w_in.shape[0] == 1
    la, lb = cache_a_k.shape[2], cache_b_k.shape[2]
    w = A_WIDTH
    perm = _qb_perm()

    wl = w_in[0]
    w_all = jnp.concatenate([wl[:, :3 * w], wl[:, 3 * w:4 * w][:, perm], wl[:, 4 * w:]], axis=1).astype(BF16)
    seg = (jnp.arange(w)[:, None] // HEAD_DIM == jnp.arange(w)[None, :] // HEAD_DIM).astype(BF16)
    t8 = lambda g: jnp.tile(g, A_HEADS)[None, :]
    gains = (t8(g_qa[0]), t8(g_ka[0]), t8(g_qb[0]), jnp.tile(g_kb[0], B_KV_HEADS)[None, :])
    sb = _alibi_slopes(B_HEADS)
    slopes_perm = [sb[(h % 2) * B_GROUP + h // 2] for h in range(B_HEADS)]
    sinks_perm = jnp.stack([sinks[0, (h % 2) * B_GROUP + h // 2] for h in range(B_HEADS)])
    wo_a = w_o[0, :w, :].astype(BF16)
    wo_b = w_o[0, w:, :][perm, :].astype(BF16)
    g_col = norm_ffn[0][:, None]
    wq_t = peer_wq[0].T.astype(BF16)
    nk = peer_keys.shape[3]
    keys = peer_keys[0].reshape(2 * PEER_HEADS, nk, peer_keys.shape[4]).astype(BF16)
    u = peer_u[0].astype(BF16)
    vt = peer_v[0].T.astype(BF16)

    xp = x_prompt.reshape(b * l, d)
    tm_p = min(TOKEN_TILE, l)
    qa4, ka4, va4, qb4, kat, vat, kb, vb, kbt, vbt = _project(xp, norm_attn, w_all, seg, *gains, tm_p, b)
    seq4 = lambda t: t.reshape(t.shape[0], b, l, LANES)
    oa4 = _mixer(seq4(qa4), seq4(ka4), seq4(va4), dils=[dl for (_, dl) in A_PATTERNS],
                 slopes=_alibi_slopes(A_HEADS))
    ob4 = _mixer(seq4(qb4), kb.reshape(1, b, l, LANES), vb.reshape(1, b, l, LANES), dils=[1],
                 slopes=slopes_perm, sinks=sinks_perm)
    ht_p = _outproj(xp, oa4.reshape(NSLAB, b * l, LANES), ob4.reshape(NSLAB, b * l, LANES), wo_a, wo_b, tm_p)
    y_p = _peer(ht_p, g_col, wq_t, keys, u, vt, **_peer_tiles(b * l, nk))
    na = min(la, l)
    nb = min(lb, l)
    def tail(t, heads, keep):
        t = t[:, :, l - keep:].reshape(b, heads, HEAD_DIM, keep)
        return jnp.transpose(t, (0, 3, 1, 2))[None]
    pak, pav = tail(kat, A_HEADS, na), tail(vat, A_HEADS, na)
    pbk, pbv = tail(kbt, B_KV_HEADS, nb), tail(vbt, B_KV_HEADS, nb)

    xs = x_sample.reshape(s * ns, d)
    tm_s = min(TOKEN_TILE, s * ns)
    qa4, _, _, qb4, kat, vat, _, _, kbt, vbt = _project(xs, norm_attn, w_all, seg, *gains, tm_s, 1)
    wide = lambda t4: jnp.transpose(t4, (1, 0, 2)).reshape(s, ns, w)
    slabs = lambda t: jnp.transpose(t.reshape(s * ns, NSLAB, LANES), (1, 0, 2))
    to_t = lambda c: jnp.transpose(c[0], (0, 2, 3, 1)).reshape(s, w, la)
    from_t = lambda t: jnp.transpose(t.reshape(s, A_HEADS, HEAD_DIM, la), (0, 3, 1, 2))[None]
    new_t = lambda t: jnp.transpose(t.reshape(t.shape[1], s, ns), (1, 0, 2))
    sak_t, sav_t, oa = _sample_at(wide(qa4), new_t(kat), new_t(vat), to_t(cache_a_k), to_t(cache_a_v))
    sak, sav = from_t(sak_t), from_t(sav_t)
    to_tb = lambda c: jnp.transpose(c[0], (0, 2, 3, 1)).reshape(s, LANES, lb)
    from_tb = lambda t: jnp.transpose(t.reshape(s, B_KV_HEADS, HEAD_DIM, lb), (0, 3, 1, 2))[None]
    sbk_t, sbv_t, ob = _sample_b(sinks_perm, slopes_perm, wide(qb4), new_t(kbt), new_t(vbt),
                                 to_tb(cache_b_k), to_tb(cache_b_v))
    ht_s = _outproj(xs, slabs(oa), slabs(ob), wo_a, wo_b, tm_s)
    y_s = _peer(ht_s, g_col, wq_t, keys, u, vt, **_peer_tiles(s * ns, nk))

    return (y_p.reshape(b, l, d), y_s.reshape(s, ns, d), pak, pav, pbk, pbv,
            sak, sav, from_tb(sbk_t), from_tb(sbv_t))
```

```python
import functools
import math

import jax
import jax.numpy as jnp
from jax import lax
from jax.experimental import pallas as pl
from jax.experimental.pallas import tpu as pltpu

HEAD_DIM = 64
A_HEADS = 8
B_HEADS = 8
B_KV_HEADS = 2
B_GROUP = B_HEADS // B_KV_HEADS
A_PATTERNS = ((128, 1), (512, 4), (2048, 16))
BAND = 128
A_WIDTH = A_HEADS * HEAD_DIM
PEER_HEADS = 8
PEER_TOPK = 16
NORM_EPS = 1e-6
NEG = -1e30
SCALE = HEAD_DIM ** -0.5
LANES = 128
VMEM_LIMIT_BYTES = 56 * 1024 * 1024

BF16 = jnp.bfloat16
F32 = jnp.float32


def _alibi_slopes(n):
    return [2.0 ** (-8.0 * (i + 1) / n) for i in range(n)]


def _cparams(*sem):
    return pltpu.CompilerParams(dimension_semantics=sem, vmem_limit_bytes=VMEM_LIMIT_BYTES)


def _head_rms(h, seg, gain):
    sq = h * h
    hi = sq.astype(BF16)
    lo = (sq - hi.astype(F32)).astype(BF16)
    ms = (jnp.dot(hi, seg, preferred_element_type=F32)
          + jnp.dot(lo, seg, preferred_element_type=F32)) * (1.0 / HEAD_DIM)
    return h * lax.rsqrt(ms + NORM_EPS) * gain


NSLAB = A_WIDTH // LANES


def _proj_kernel(x_ref, g_ref, w_ref, seg_ref, gqa_ref, gka_ref, gqb_ref, gkb_ref,
                 qa4_ref, ka4_ref, va4_ref, qb4_ref, kat_ref, vat_ref, kb_ref, vb_ref, kbt_ref, vbt_ref):
    x = x_ref[...]
    ms = jnp.mean(x * x, axis=-1, keepdims=True)
    xn = (x * lax.rsqrt(ms + NORM_EPS) * g_ref[...]).astype(BF16)
    h = jnp.dot(xn, w_ref[...], preferred_element_type=F32)
    seg = seg_ref[...]
    w = A_WIDTH
    qa = _head_rms(h[:, 0:w], seg, gqa_ref[...])
    ka = _head_rms(h[:, w:2 * w], seg, gka_ref[...])
    va = h[:, 2 * w:3 * w]
    qb = _head_rms(h[:, 3 * w:4 * w], seg, gqb_ref[...])
    kb = _head_rms(h[:, 4 * w:4 * w + LANES], seg[0:LANES, 0:LANES], gkb_ref[...])
    vb = h[:, 4 * w + LANES:4 * w + 2 * LANES]
    kb_ref[...] = kb
    vb_ref[...] = vb
    kat_ref[...] = ka.T
    vat_ref[...] = va.T
    kbt_ref[...] = kb.T
    vbt_ref[...] = vb.T
    for s in range(NSLAB):
        sl = slice(s * LANES, (s + 1) * LANES)
        qa4_ref[s] = qa[:, sl]
        ka4_ref[s] = ka[:, sl]
        va4_ref[s] = va[:, sl]
        qb4_ref[s] = qb[:, sl]


def _project(x2d, norm_g, w_bf16, seg, gqa, gka, gqb, gkb, tm, nseq):
    n, d = x2d.shape
    nc = w_bf16.shape[1]
    w = A_WIDTH
    l = n // nseq
    tps = l // tm
    row = lambda i: (i, 0)
    col = lambda i: (i // tps, 0, i % tps)
    slab = lambda i: (0, i, 0)
    const = lambda i: (0, 0)
    outs = ([jax.ShapeDtypeStruct((NSLAB, n, LANES), F32)] * 4 + [jax.ShapeDtypeStruct((nseq, w, l), F32)] * 2
            + [jax.ShapeDtypeStruct((n, LANES), F32)] * 2 + [jax.ShapeDtypeStruct((nseq, LANES, l), F32)] * 2)
    return pl.pallas_call(
        _proj_kernel,
        grid=(n // tm,),
        in_specs=[pl.BlockSpec((tm, d), row), pl.BlockSpec((1, d), const), pl.BlockSpec((d, nc), const),
                  pl.BlockSpec((w, w), const), pl.BlockSpec((1, w), const), pl.BlockSpec((1, w), const),
                  pl.BlockSpec((1, w), const), pl.BlockSpec((1, LANES), const)],
        out_specs=([pl.BlockSpec((NSLAB, tm, LANES), slab)] * 4 + [pl.BlockSpec((None, w, tm), col)] * 2
                   + [pl.BlockSpec((tm, LANES), row)] * 2 + [pl.BlockSpec((None, LANES, tm), col)] * 2),
        out_shape=outs,
        compiler_params=_cparams("parallel"),
        name="qkv_proj",
    )(x2d, norm_g, w_bf16, seg, gqa, gka, gqb, gkb)


MIXER_UNROLL = 8


def _pick(idx, values):
    out = jnp.float32(values[-1])
    for i in range(len(values) - 2, -1, -1):
        out = jnp.where(idx == i, jnp.float32(values[i]), out)
    return out


def _mixer_kernel(*refs, dils, slopes, with_sink, seq):
    if with_sink:
        sink_ref, refs = refs[0], refs[1:]
    q_ref, k_ref, v_ref, o_ref, kpad, vpad, bias_ref = refs[:7]
    ob_ref, lse_ref = refs[7:9] if len(dils) > 1 else (None, None)
    hp = pl.program_id(1)
    pad = BAND * max(dils)

    @pl.when((pl.program_id(0) == 0) & (hp == 0))
    def _():
        kpad[0:pad, :] = jnp.zeros((pad, LANES), F32)
        vpad[0:pad, :] = jnp.zeros((pad, LANES), F32)

    kpad[pad:pad + seq, :] = k_ref[0, 0]
    vpad[pad:pad + seq, :] = v_ref[0, 0]

    r = lax.broadcasted_iota(jnp.int32, (BAND, 2 * BAND), 0)
    c = lax.broadcasted_iota(jnp.int32, (BAND, 2 * BAND), 1)
    dist = r + BAND - c
    valid = (dist >= 0) & (dist <= BAND)
    for bi, dil in enumerate(dils):
        distf = (dist * dil).astype(F32)
        for e in range(2):
            slope = _pick(hp, [slopes[2 * g + e] for g in range(NSLAB)])
            bias_ref[bi, e] = jnp.where(valid, -slope * distf, NEG)

    lane = lax.broadcasted_iota(jnp.int32, (BAND, LANES), 1)
    hi = lane >= HEAD_DIM
    col = lax.broadcasted_iota(jnp.int32, (1, 2 * BAND), 1)
    prev_cols = (col < BAND).astype(F32)
    nt = (((1,), (1,)), ((), ()))

    for bi, dil in enumerate(dils):
        nblk = seq // (dil * BAND)

        def block(t, bi=bi, dil=dil, nblk=nblk):
            res = t // nblk
            j = t % nblk
            base = res + dil * BAND * j
            if dil == 1:
                base = pl.multiple_of(base, BAND)
                rows = pl.ds(base, BAND)
                win = pl.ds(pl.multiple_of(pad + base - BAND, BAND), 2 * BAND)
            else:
                rows = pl.ds(base, BAND, stride=dil)
                win = pl.ds(pad + base - dil * BAND, 2 * BAND, stride=dil)
            qs = q_ref[0, 0, rows, :]
            kw = kpad[win, :].astype(BF16)
            vw = vpad[win, :].astype(BF16)
            pen = jnp.where(j == 0, NEG, 0.0) * prev_cols
            outs, lses = [], []
            for e in range(2):
                qm = jnp.where(hi == bool(e), qs, 0.0).astype(BF16)
                s = lax.dot_general(qm, kw, nt, preferred_element_type=F32)
                s = s * SCALE + bias_ref[bi, e] + pen
                m = jnp.max(s, axis=-1, keepdims=True)
                if with_sink:
                    sink = sink_ref[2 * hp + e]
                    m = jnp.maximum(m, sink)
                p = jnp.exp(s - m)
                den = jnp.sum(p, axis=-1, keepdims=True)
                if with_sink:
                    den = den + jnp.exp(sink - m)
                o = jnp.dot(p.astype(BF16), vw, preferred_element_type=F32)
                outs.append(o / den)
                lses.append(m + jnp.log(den))
            if len(dils) == 1:
                o_ref[0, 0, rows, :] = jnp.where(hi, outs[1], outs[0])
            else:
                ob_ref[bi, rows, :] = jnp.where(hi, outs[1], outs[0])
                lse_ref[bi, rows, :] = jnp.where(hi, lses[1], lses[0])

        def blocks(t, carry, block=block):
            for uu in range(MIXER_UNROLL):
                block(t * MIXER_UNROLL + uu)
            return carry

        assert (dil * nblk) % MIXER_UNROLL == 0
        lax.fori_loop(0, dil * nblk // MIXER_UNROLL, blocks, 0)

    if len(dils) > 1:
        def merge(t, carry):
            rows = pl.ds(pl.multiple_of(t * BAND, BAND), BAND)
            ls = [lse_ref[bi, rows, :] for bi in range(len(dils))]
            mx = functools.reduce(jnp.maximum, ls)
            ws = [jnp.exp(l - mx) for l in ls]
            num = sum(w * ob_ref[bi, rows, :] for bi, w in enumerate(ws))
            o_ref[0, 0, rows, :] = num / sum(ws)
            return carry

        lax.fori_loop(0, seq // BAND, merge, 0)


def _mixer(q4, k4, v4, *, dils, slopes, sinks=None):
    nslab, b, l, _ = q4.shape
    assert l % (BAND * max(dils)) == 0
    shared = k4.shape[0] == 1
    qmap = lambda bi, hp: (hp, bi, 0, 0)
    kmap = (lambda bi, hp: (0, bi, 0, 0)) if shared else qmap
    blk = (1, 1, l, LANES)
    in_specs = [pl.BlockSpec(blk, qmap), pl.BlockSpec(blk, kmap), pl.BlockSpec(blk, kmap)]
    args = [q4, k4, v4]
    if sinks is not None:
        in_specs = [pl.BlockSpec(memory_space=pltpu.SMEM)] + in_specs
        args = [sinks] + args
    pad = BAND * max(dils)
    scratch = [pltpu.VMEM((pad + l, LANES), F32), pltpu.VMEM((pad + l, LANES), F32),
               pltpu.VMEM((len(dils), 2, BAND, 2 * BAND), F32)]
    if len(dils) > 1:
        scratch += [pltpu.VMEM((len(dils), l, LANES), F32), pltpu.VMEM((len(dils), l, LANES), F32)]
    return pl.pallas_call(
        functools.partial(_mixer_kernel, dils=tuple(dils), slopes=slopes, with_sink=sinks is not None, seq=l),
        grid=(b, nslab),
        in_specs=in_specs,
        out_specs=pl.BlockSpec(blk, qmap),
        out_shape=jax.ShapeDtypeStruct(q4.shape, F32),
        scratch_shapes=scratch,
        compiler_params=_cparams("arbitrary", "arbitrary"),
        name="mixer_b" if shared else "mixer_a",
    )(*args)


def _qb_perm():
    idx = []
    for g in range(B_GROUP):
        for kv in range(B_KV_HEADS):
            base = (kv * B_GROUP + g) * HEAD_DIM
            idx.extend(range(base, base + HEAD_DIM))
    return jnp.asarray(idx, jnp.int32)


NEW = 8
HQ = A_HEADS * NEW


def _block_diag_rows(q8, width):
    rep = jnp.concatenate([q8] * A_HEADS, axis=0)
    r = lax.broadcasted_iota(jnp.int32, (HQ, width), 0)
    c = lax.broadcasted_iota(jnp.int32, (HQ, width), 1)
    return jnp.where(r // NEW == c // HEAD_DIM, rep, 0.0)


def _block_diag_rows_keep(x, width):
    r = lax.broadcasted_iota(jnp.int32, (HQ, width), 0)
    c = lax.broadcasted_iota(jnp.int32, (HQ, width), 1)
    return jnp.where(r // NEW == c // HEAD_DIM, x, 0.0)


ROLL_ROWS = 64


def _sample_at_kernel(q_ref, kn_ref, vn_ref, ck_ref, cv_ref, nk_ref, nv_ref, o_ref, bias_ref, bn_ref, knb, vnb,
                      *, la, slopes):
    w = A_WIDTH
    nbr = len(A_PATTERNS)

    @pl.when(pl.program_id(0) == 0)
    def _():
        r = lax.broadcasted_iota(jnp.int32, (HQ, la), 0)
        c = lax.broadcasted_iota(jnp.int32, (HQ, la), 1)
        dist = (r % NEW) + la - c
        slope = jnp.zeros((HQ, la), F32)
        rn = lax.broadcasted_iota(jnp.int32, (HQ, LANES), 0)
        cn = lax.broadcasted_iota(jnp.int32, (HQ, LANES), 1)
        dn = (rn % NEW) - cn
        sl = jnp.zeros((HQ, LANES), F32)
        for h in range(A_HEADS):
            slope = jnp.where(r // NEW == h, slopes[h], slope)
            sl = jnp.where(rn // NEW == h, slopes[h], sl)
        for bi, (wdw, dil) in enumerate(A_PATTERNS):
            bias_ref[bi] = jnp.where((dist <= wdw) & (dist % dil == 0), -slope * dist.astype(F32), NEG)
            bn_ref[bi] = jnp.where((dn >= 0) & (dn % dil == 0) & (cn < NEW), -sl * dn.astype(F32), NEG)
        knb[...] = jnp.zeros_like(knb)
        vnb[...] = jnp.zeros_like(vnb)

    knb[:, 0:NEW] = kn_ref[0]
    vnb[:, 0:NEW] = vn_ref[0]

    qbd = _block_diag_rows(q_ref[0], w).astype(BF16)
    nt = (((1,), (1,)), ((), ()))
    s_c = jnp.dot(qbd, ck_ref[0].astype(BF16), preferred_element_type=F32) * SCALE
    s_n = jnp.dot(qbd, knb[...].astype(BF16), preferred_element_type=F32) * SCALE
    ms, dens, pcs, pns = [], [], [], []
    for bi in range(nbr):
        sc = s_c + bias_ref[bi]
        sn = s_n + bn_ref[bi]
        m = jnp.maximum(jnp.max(sc, axis=-1, keepdims=True), jnp.max(sn, axis=-1, keepdims=True))
        pc = jnp.exp(sc - m)
        pn = jnp.exp(sn - m)
        dens.append(jnp.sum(pc, axis=-1, keepdims=True) + jnp.sum(pn, axis=-1, keepdims=True))
        ms.append(m)
        pcs.append(pc.astype(BF16))
        pns.append(pn.astype(BF16))
    nums = (lax.dot_general(jnp.concatenate(pcs, axis=0), cv_ref[0].astype(BF16), nt, preferred_element_type=F32)
            + lax.dot_general(jnp.concatenate(pns, axis=0), vnb[...].astype(BF16), nt, preferred_element_type=F32))
    mx = functools.reduce(jnp.maximum, ms)
    den = 0.0
    num = 0.0
    for bi in range(nbr):
        wt = jnp.exp(ms[bi] - mx)
        den = den + dens[bi] * wt
        num = num + nums[bi * HQ:(bi + 1) * HQ, :] * wt
    out = _block_diag_rows_keep(num / den, w)
    o_ref[0] = sum(out[h * NEW:(h + 1) * NEW, :] for h in range(A_HEADS))

    lane = lax.broadcasted_iota(jnp.int32, (ROLL_ROWS, LANES), 1)
    tail = lane >= LANES - NEW
    for src, newb, dst in ((ck_ref, knb, nk_ref), (cv_ref, vnb, nv_ref)):
        for r0 in range(0, w, ROLL_ROWS):
            rows = slice(r0, r0 + ROLL_ROWS)
            rolled = pltpu.roll(src[0, rows, :], la - NEW, 1)
            dst[0, rows, 0:la - LANES] = rolled[:, 0:la - LANES]
            fresh = pltpu.roll(newb[rows, :], LANES - NEW, 1)
            dst[0, rows, la - LANES:la] = jnp.where(tail, fresh, rolled[:, la - LANES:la])


def _sample_at(q, kn_t, vn_t, ck_t, cv_t):
    s, w, la = ck_t.shape
    assert la >= max(wd for (wd, _) in A_PATTERNS) and la % LANES == 0
    seq = lambda i: (i, 0, 0)
    big = pl.BlockSpec((1, w, la), seq)
    new = pl.BlockSpec((1, w, NEW), seq)
    qspec = pl.BlockSpec((1, NEW, w), seq)
    nbr = len(A_PATTERNS)
    return pl.pallas_call(
        functools.partial(_sample_at_kernel, la=la, slopes=_alibi_slopes(A_HEADS)),
        grid=(s,),
        in_specs=[qspec, new, new, big, big],
        out_specs=[big, big, qspec],
        out_shape=[jax.ShapeDtypeStruct(ck_t.shape, F32), jax.ShapeDtypeStruct(cv_t.shape, F32),
                   jax.ShapeDtypeStruct(q.shape, F32)],
        scratch_shapes=[pltpu.VMEM((nbr, HQ, la), F32), pltpu.VMEM((nbr, HQ, LANES), F32),
                        pltpu.VMEM((w, LANES), F32), pltpu.VMEM((w, LANES), F32)],
        compiler_params=_cparams("arbitrary"),
        name="sample_attn_at",
    )(q, kn_t, vn_t, ck_t, cv_t)


def _sample_b_kernel(sink_ref, q_ref, kn_ref, vn_ref, ck_ref, cv_ref, nk_ref, nv_ref, o_ref,
                     bc_ref, bn_ref, sk_ref, knb, vnb, *, lb, nseq, slopes):
    @pl.when(pl.program_id(0) == 0)
    def _():
        r = lax.broadcasted_iota(jnp.int32, (HQ, lb), 0)
        c = lax.broadcasted_iota(jnp.int32, (HQ, lb), 1)
        dist = (r % NEW) + lb - c
        slope = jnp.zeros((HQ, lb), F32)
        sink = jnp.zeros((HQ, LANES), F32)
        rs = lax.broadcasted_iota(jnp.int32, (HQ, LANES), 0)
        for h in range(B_HEADS):
            slope = jnp.where(r // NEW == h, slopes[h], slope)
            sink = jnp.where(rs // NEW == h, sink_ref[h], sink)
        bc_ref[...] = jnp.where(dist <= BAND, -slope * dist.astype(F32), NEG)
        cn = lax.broadcasted_iota(jnp.int32, (HQ, LANES), 1)
        dn = (rs % NEW) - cn
        sl = jnp.zeros((HQ, LANES), F32)
        for h in range(B_HEADS):
            sl = jnp.where(rs // NEW == h, slopes[h], sl)
        bn_ref[...] = jnp.where((dn >= 0) & (cn < NEW), -sl * dn.astype(F32), NEG)
        sk_ref[...] = sink
        knb[...] = jnp.zeros_like(knb)
        vnb[...] = jnp.zeros_like(vnb)

    nt = (((1,), (1,)), ((), ()))
    rr = lax.broadcasted_iota(jnp.int32, (HQ, LANES), 0)
    cc = lax.broadcasted_iota(jnp.int32, (HQ, LANES), 1)
    own = ((rr // NEW) % B_KV_HEADS) == (cc // HEAD_DIM)
    sink = sk_ref[:, 0:1]
    tail = cc >= LANES - NEW
    for t in range(nseq):
        knb[:, 0:NEW] = kn_ref[t]
        vnb[:, 0:NEW] = vn_ref[t]
        for src, newb, dst in ((ck_ref, knb, nk_ref), (cv_ref, vnb, nv_ref)):
            for r0 in range(0, LANES, HQ):
                rows = slice(r0, r0 + HQ)
                dst[t, rows, :] = jnp.where(tail, pltpu.roll(newb[rows, :], LANES - NEW, 1),
                                            pltpu.roll(src[t, rows, :], lb - NEW, 1))
        q = q_ref[t]
        rows = []
        for g in range(B_GROUP):
            for kv in range(B_KV_HEADS):
                rows.append(q[:, g * LANES:(g + 1) * LANES])
        qbd = jnp.where(own, jnp.concatenate(rows, axis=0), 0.0).astype(BF16)
        sc = jnp.dot(qbd, ck_ref[t].astype(BF16), preferred_element_type=F32) * SCALE + bc_ref[...]
        sn = jnp.dot(qbd, knb[...].astype(BF16), preferred_element_type=F32) * SCALE + bn_ref[...]
        m = jnp.maximum(jnp.maximum(jnp.max(sc, axis=-1, keepdims=True), jnp.max(sn, axis=-1, keepdims=True)), sink)
        pc = jnp.exp(sc - m)
        pn = jnp.exp(sn - m)
        den = jnp.sum(pc, axis=-1, keepdims=True) + jnp.sum(pn, axis=-1, keepdims=True) + jnp.exp(sink - m)
        o = (lax.dot_general(pc.astype(BF16), cv_ref[t].astype(BF16), nt, preferred_element_type=F32)
             + lax.dot_general(pn.astype(BF16), vnb[...].astype(BF16), nt, preferred_element_type=F32)) / den
        o = jnp.where(own, o, 0.0)
        for g in range(B_GROUP):
            base = g * B_KV_HEADS * NEW
            o_ref[t, :, g * LANES:(g + 1) * LANES] = o[base:base + NEW, :] + o[base + NEW:base + 2 * NEW, :]


def _sample_b(sinks_perm, slopes_perm, q, kn, vn, ck, cv, nseq=8):
    s, kw, lb = ck.shape
    assert lb == BAND and kw == LANES and lb == LANES and s % nseq == 0
    seq = lambda i: (i, 0, 0)
    return pl.pallas_call(
        functools.partial(_sample_b_kernel, lb=lb, nseq=nseq, slopes=slopes_perm),
        grid=(s // nseq,),
        in_specs=[pl.BlockSpec(memory_space=pltpu.SMEM), pl.BlockSpec((nseq, NEW, A_WIDTH), seq),
                  pl.BlockSpec((nseq, kw, NEW), seq), pl.BlockSpec((nseq, kw, NEW), seq),
                  pl.BlockSpec((nseq, kw, lb), seq), pl.BlockSpec((nseq, kw, lb), seq)],
        out_specs=[pl.BlockSpec((nseq, kw, lb), seq), pl.BlockSpec((nseq, kw, lb), seq),
                   pl.BlockSpec((nseq, NEW, A_WIDTH), seq)],
        out_shape=[jax.ShapeDtypeStruct(ck.shape, F32), jax.ShapeDtypeStruct(cv.shape, F32),
                   jax.ShapeDtypeStruct(q.shape, F32)],
        scratch_shapes=[pltpu.VMEM((HQ, lb), F32), pltpu.VMEM((HQ, LANES), F32), pltpu.VMEM((HQ, LANES), F32),
                        pltpu.VMEM((LANES, kw), F32), pltpu.VMEM((LANES, kw), F32)],
        compiler_params=_cparams("arbitrary"),
        name="sample_attn_b",
    )(sinks_perm, q, kn, vn, ck, cv)


def _outproj_kernel(x_ref, oa_ref, ob_ref, wa_ref, wb_ref, ht_ref):
    oa = jnp.concatenate([oa_ref[s] for s in range(NSLAB)], axis=1).astype(BF16)
    ob = jnp.concatenate([ob_ref[s] for s in range(NSLAB)], axis=1).astype(BF16)
    h = (x_ref[...] + jnp.dot(oa, wa_ref[...], preferred_element_type=F32)
         + jnp.dot(ob, wb_ref[...], preferred_element_type=F32))
    ht_ref[...] = h.T


def _outproj(x2d, oa4, ob4, wa, wb, tm):
    n, d = x2d.shape
    w = A_WIDTH
    row = lambda i: (i, 0)
    const = lambda i: (0, 0)
    slab = pl.BlockSpec((NSLAB, tm, LANES), lambda i: (0, i, 0))
    return pl.pallas_call(
        _outproj_kernel,
        grid=(n // tm,),
        in_specs=[pl.BlockSpec((tm, d), row), slab, slab, pl.BlockSpec((w, d), const), pl.BlockSpec((w, d), const)],
        out_specs=pl.BlockSpec((d, tm), lambda i: (0, i)),
        out_shape=jax.ShapeDtypeStruct((d, n), F32),
        compiler_params=_cparams("parallel"),
        name="out_proj",
    )(x2d, oa4, ob4, wa, wb)


SUBLANES = 8


def _merge_exchange(n):
    pairs = []
    t = max(1, math.ceil(math.log2(n)))
    p = 1 << (t - 1)
    while p > 0:
        q, r, d = 1 << (t - 1), 0, p
        while d > 0:
            pairs.extend((i, i + d) for i in range(n - d) if (i & p) == r)
            d, q, r = q - p, q >> 1, p
        p >>= 1
    return pairs


def _vmax(a, b):
    if a is None:
        return b
    if b is None:
        return a
    return jnp.maximum(a, b)


def _vmin(a, b):
    if a is None or b is None:
        return None
    return jnp.minimum(a, b)


def _exchange(x, i, j):
    x[i], x[j] = _vmax(x[i], x[j]), _vmin(x[i], x[j])


def _top16(tiles):
    x = list(tiles) + [None] * (PEER_TOPK - len(tiles))
    for (i, j) in _merge_exchange(len(tiles)):
        _exchange(x, i, j)
    for shift in (4, 2, 1):
        y = [None if v is None else pltpu.roll(v, shift, 0) for v in x]
        x = [_vmax(x[k], y[PEER_TOPK - 1 - k]) for k in range(PEER_TOPK)]
        for d in (8, 4, 2, 1):
            for i in range(PEER_TOPK):
                if not i & d:
                    _exchange(x, i, i + d)
    return x


def _count_prefix(pred, vals):
    sel = jnp.where
    c16 = pred(vals[15])
    c8 = pred(vals[7])
    c4 = pred(sel(c8, vals[11], vals[3]))
    c2 = pred(sel(c8, sel(c4, vals[13], vals[9]), sel(c4, vals[5], vals[1])))
    c1 = pred(sel(c8, sel(c4, sel(c2, vals[14], vals[12]), sel(c2, vals[10], vals[8])),
                  sel(c4, sel(c2, vals[6], vals[4]), sel(c2, vals[2], vals[0]))))
    lo = sel(c8, 8.0, 0.0) + sel(c4, 4.0, 0.0) + sel(c2, 2.0, 0.0) + sel(c1, 1.0, 0.0)
    return sel(c16, float(PEER_TOPK), lo)


def _rows_sum(x):
    for shift in (4, 2, 1):
        x = x + pltpu.roll(x, shift, 0)
    return x


def _peer_select_kernel(ht_ref, g_ref, wq_ref, keys_ref,
                        hn_ref, r1_ref, w1_ref, cnt_ref, w0_ref, s_ref, *, nk, tn):
    h = ht_ref[...]
    ms = jnp.mean(h * h, axis=0, keepdims=True)
    hn = (h * lax.rsqrt(ms + NORM_EPS) * g_ref[...]).astype(BF16)
    hn_ref[...] = hn
    q = jnp.dot(wq_ref[...], hn, preferred_element_type=F32)
    half = q.shape[0] // (2 * PEER_HEADS)
    for k in range(2 * PEER_HEADS):
        qk = q[k * half:(k + 1) * half, :].astype(BF16)
        s_ref[k] = jnp.dot(keys_ref[k], qk, preferred_element_type=F32)
    nchunk = tn // LANES
    ntile = nk // SUBLANES
    sub = lax.broadcasted_iota(jnp.int32, (SUBLANES, LANES), 0)

    def pack(vals):
        out = vals[-1]
        for r in range(len(vals) - 2, -1, -1):
            out = jnp.where(sub == r, vals[r], out)
        return out

    def head_chunk(t, carry):
        hd = t // nchunk
        lanes = pl.ds(pl.multiple_of((t % nchunk) * LANES, LANES), LANES)
        rows = [slice(k * SUBLANES, (k + 1) * SUBLANES) for k in range(ntile)]
        s0 = [s_ref[2 * hd, r, lanes] for r in rows]
        s1 = [s_ref[2 * hd + 1, r, lanes] for r in rows]
        v0 = _top16(s0)
        v1 = _top16(s1)
        v1lo, v1hi, v0hi = pack(v1[0:8]), pack(v1[8:16]), pack(v0[8:16])
        cands = [v0[0] + v1lo, v0[0] + v1hi] + [v0[a] + v1lo for a in range(1, 8)] + [v0hi + v1[0]]
        best = _top16(cands)
        top, tau = best[0], best[PEER_TOPK - 1]
        z = _rows_sum(sum(jnp.where(c >= tau, jnp.exp(c - top), 0.0) for c in cands))
        inv_z = 1.0 / z
        for m in range(ntile // 2):
            cnt, rk1 = [], []
            for k in (2 * m, 2 * m + 1):
                cnt.append(_count_prefix(lambda t, k=k: s0[k] + t >= tau, v1))
                rk1.append(_count_prefix(lambda t, k=k: t > s1[k], v1))
            pair = slice(2 * m * SUBLANES, (2 * m + 2) * SUBLANES)
            both = lambda f: jnp.concatenate([f(2 * m), f(2 * m + 1)], axis=0)
            cnt_ref[hd, pair, lanes] = jnp.concatenate(cnt, axis=0)
            w0_ref[hd, pair, lanes] = both(lambda k: jnp.exp(s0[k] - v0[0]) * inv_z)
            r1_ref[hd, pair, lanes] = jnp.concatenate(rk1, axis=0).astype(BF16)
            w1_ref[hd, pair, lanes] = both(lambda k: jnp.exp(s1[k] - v1[0])).astype(BF16)
        return carry

    lax.fori_loop(0, PEER_HEADS * nchunk, head_chunk, 0)


def _peer_select(ht, g_col, wq_t, keys, tn):
    d, n = ht.shape
    assert tn % LANES == 0 and n % tn == 0
    nslab, nk, half = keys.shape
    tok = lambda i: (0, i)
    tok3 = lambda i: (0, 0, i)
    stat = lambda dt: jax.ShapeDtypeStruct((PEER_HEADS, nk, n), dt)
    return pl.pallas_call(
        functools.partial(_peer_select_kernel, nk=nk, tn=tn),
        grid=(n // tn,),
        in_specs=[pl.BlockSpec((d, tn), tok), pl.BlockSpec((d, 1), lambda i: (0, 0)),
                  pl.BlockSpec(wq_t.shape, lambda i: (0, 0)), pl.BlockSpec(keys.shape, lambda i: (0, 0, 0))],
        out_specs=[pl.BlockSpec((d, tn), tok)] + [pl.BlockSpec((PEER_HEADS, nk, tn), tok3)] * 4,
        out_shape=[jax.ShapeDtypeStruct((d, n), BF16), stat(BF16), stat(BF16), stat(F32), stat(F32)],
        scratch_shapes=[pltpu.VMEM((nslab, nk, tn), F32)],
        compiler_params=_cparams("parallel"),
        name="peer_select",
    )(ht, g_col, wq_t, keys)


def _gelu(x):
    return 0.5 * x * (1.0 + lax.erf(x * (2.0 ** -0.5)))


def _peer_dense_kernel(hn_ref, u_ref, vt_ref, r1_ref, w1_ref, cnt_ref, w0_ref, ht_ref, y_ref,
                       acc_ref, c0_ref, c1_ref, g_ref, *, nk, te, tn, lc, rb, ne, nsteps):
    s = pl.program_id(0)

    @pl.when(s == 0)
    def _():
        for r in (c0_ref, c1_ref, acc_ref):
            r[...] = jnp.zeros_like(r)

    e = jnp.minimum(s, nsteps - 1) % ne
    p1 = s - 1
    first = (jnp.maximum(p1, 0) % ne) == 0
    nslab = te // nk

    def stages(c_new, c_prev):
        ob = acc_ref.shape[0] // nslab
        for i2 in range(nslab // SLAB_GROUP):
            slabs = range(i2 * SLAB_GROUP, (i2 + 1) * SLAB_GROUP)
            orows = slice(i2 * SLAB_GROUP * ob, (i2 + 1) * SLAB_GROUP * ob)
            contrib = jnp.dot(vt_ref[orows, :], c_prev[...], preferred_element_type=F32)
            acc_ref[orows, :] = jnp.where(first, contrib, acc_ref[orows, :] + contrib)
            bits = lax.bitcast_convert_type(contrib[0:1, :], jnp.int32)
            zero = lax.shift_right_logical(lax.shift_right_logical(bits, 16), 16).astype(F32)
            cnt_rows = [[(cnt_ref[hd, pl.ds(e * nslab + ii, 1), :] + zero).astype(BF16)
                         for hd in range(PEER_HEADS)] for ii in slabs]
            w0_rows = [[w0_ref[hd, pl.ds(e * nslab + ii, 1), :].astype(BF16)
                        for hd in range(PEER_HEADS)] for ii in slabs]
            for c in range(tn // lc):
                ls = slice(c * lc, (c + 1) * lc)
                for jb in range(nk // rb):
                    js = slice(jb * rb, (jb + 1) * rb)
                    gs = [None] * SLAB_GROUP
                    for hd in range(PEER_HEADS):
                        r1 = r1_ref[hd, js, ls]
                        w1 = w1_ref[hd, js, ls]
                        for k in range(SLAB_GROUP):
                            term = jnp.where(r1 < cnt_rows[k][hd][:, ls], w1, jnp.zeros((), BF16)) * w0_rows[k][hd][:, ls]
                            gs[k] = term if gs[k] is None else gs[k] + term
                    for k, ii in enumerate(slabs):
                        g_ref[ii * nk + jb * rb:ii * nk + (jb + 1) * rb, ls] = gs[k]
        for ii in range(nslab):
            rows = slice(ii * nk, (ii + 1) * nk)
            a = jnp.dot(u_ref[rows, :], hn_ref[...], preferred_element_type=F32)
            c_new[rows, :] = g_ref[rows, :] * _gelu(a).astype(BF16)

    @pl.when(s % 2 == 0)
    def _():
        stages(c0_ref, c1_ref)

    @pl.when(s % 2 == 1)
    def _():
        stages(c1_ref, c0_ref)

    @pl.when((p1 >= 0) & (p1 % ne == ne - 1))
    def _():
        y_ref[...] = (ht_ref[...] + acc_ref[...]).T


def _peer_dense(hn, u, vt, r1, w1, cnt, w0, ht, *, tn, te, lc, rb):
    d, n = hn.shape
    ne = u.shape[0] // te
    nk = r1.shape[1]
    nsteps = (n // tn) * ne
    last = nsteps - 1
    pair = lambda s, lag: jnp.clip(s - lag, 0, last)
    stat = pl.BlockSpec((PEER_HEADS, nk, tn), lambda s: (0, 0, pair(s, 0) // ne))
    return pl.pallas_call(
        functools.partial(_peer_dense_kernel, nk=nk, te=te, tn=tn, lc=lc, rb=min(rb, nk), ne=ne, nsteps=nsteps),
        grid=(nsteps + 1,),
        in_specs=[pl.BlockSpec((d, tn), lambda s: (0, pair(s, 0) // ne)),
                  pl.BlockSpec((te, d), lambda s: (pair(s, 0) % ne, 0)),
                  pl.BlockSpec((d, te), lambda s: (0, pair(s, 1) % ne)),
                  stat, stat, stat, stat,
                  pl.BlockSpec((d, tn), lambda s: (0, pair(s, 1) // ne))],
        out_specs=pl.BlockSpec((tn, d), lambda s: (pair(s, 1) // ne, 0)),
        out_shape=jax.ShapeDtypeStruct((n, d), F32),
        scratch_shapes=[pltpu.VMEM((d, tn), F32), pltpu.VMEM((te, tn), BF16), pltpu.VMEM((te, tn), BF16),
                        pltpu.VMEM((te, tn), BF16)],
        compiler_params=_cparams("arbitrary"),
        name="peer_dense",
    )(hn, u, vt, r1, w1, cnt, w0, ht)


def _peer(ht, g_col, wq_t, keys, u, vt, *, tn_sel, tn, te, lc, rb):
    hn, r1, w1, cnt, w0 = _peer_select(ht, g_col, wq_t, keys, tn_sel)
    return _peer_dense(hn, u, vt, r1, w1, cnt, w0, ht, tn=tn, te=te, lc=lc, rb=rb)


TOKEN_TILE = 512
EXPERT_TILE = 1024
GATE_LANES = 256
GATE_ROWS = 64
SLAB_GROUP = 2


def _peer_tiles(n_tokens, n_keys):
    tn = min(TOKEN_TILE, n_tokens)
    return dict(tn_sel=tn, tn=tn, te=min(EXPERT_TILE, n_keys * n_keys), lc=min(GATE_LANES, tn), rb=GATE_ROWS)


def kernel(x_prompt, x_sample, cache_a_k, cache_a_v, cache_b_k, cache_b_v, norm_attn, w_in, g_qa, g_ka, g_qb, g_kb, sinks, w_o, norm_ffn, peer_wq, peer_keys, peer_u, peer_v):
    b, l, d = x_prompt.shape
    s, ns, _ = x_sample.shape
    assert ns == NEW and w_in.shape[0] == 1
    la, lb = cache_a_k.shape[2], cache_b_k.shape[2]
    w = A_WIDTH
    perm = _qb_perm()

    wl = w_in[0]
    w_all = jnp.concatenate([wl[:, :3 * w], wl[:, 3 * w:4 * w][:, perm], wl[:, 4 * w:]], axis=1).astype(BF16)
    seg = (jnp.arange(w)[:, None] // HEAD_DIM == jnp.arange(w)[None, :] // HEAD_DIM).astype(BF16)
    t8 = lambda g: jnp.tile(g, A_HEADS)[None, :]
    gains = (t8(g_qa[0]), t8(g_ka[0]), t8(g_qb[0]), jnp.tile(g_kb[0], B_KV_HEADS)[None, :])
    sb = _alibi_slopes(B_HEADS)
    slopes_perm = [sb[(h % 2) * B_GROUP + h // 2] for h in range(B_HEADS)]
    sinks_perm = jnp.stack([sinks[0, (h % 2) * B_GROUP + h // 2] for h in range(B_HEADS)])
    wo_a = w_o[0, :w, :].astype(BF16)
    wo_b = w_o[0, w:, :][perm, :].astype(BF16)
    g_col = norm_ffn[0][:, None]
    wq_t = peer_wq[0].T.astype(BF16)
    nk = peer_keys.shape[3]
    keys = peer_keys[0].reshape(2 * PEER_HEADS, nk, peer_keys.shape[4]).astype(BF16)
    u = peer_u[0].astype(BF16)
    vt = peer_v[0].T.astype(BF16)

    xp = x_prompt.reshape(b * l, d)
    tm_p = min(TOKEN_TILE, l)
    qa4, ka4, va4, qb4, kat, vat, kb, vb, kbt, vbt = _project(xp, norm_attn, w_all, seg, *gains, tm_p, b)
    seq4 = lambda t: t.reshape(t.shape[0], b, l, LANES)
    oa4 = _mixer(seq4(qa4), seq4(ka4), seq4(va4), dils=[dl for (_, dl) in A_PATTERNS],
                 slopes=_alibi_slopes(A_HEADS))
    ob4 = _mixer(seq4(qb4), kb.reshape(1, b, l, LANES), vb.reshape(1, b, l, LANES), dils=[1],
                 slopes=slopes_perm, sinks=sinks_perm)
    ht_p = _outproj(xp, oa4.reshape(NSLAB, b * l, LANES), ob4.reshape(NSLAB, b * l, LANES), wo_a, wo_b, tm_p)
    y_p = _peer(ht_p, g_col, wq_t, keys, u, vt, **_peer_tiles(b * l, nk))
    na = min(la, l)
    nb = min(lb, l)
    def tail(t, heads, keep):
        t = t[:, :, l - keep:].reshape(b, heads, HEAD_DIM, keep)
        return jnp.transpose(t, (0, 3, 1, 2))[None]
    pak, pav = tail(kat, A_HEADS, na), tail(vat, A_HEADS, na)
    pbk, pbv = tail(kbt, B_KV_HEADS, nb), tail(vbt, B_KV_HEADS, nb)

    xs = x_sample.reshape(s * ns, d)
    tm_s = min(TOKEN_TILE, s * ns)
    qa4, _, _, qb4, kat, vat, _, _, kbt, vbt = _project(xs, norm_attn, w_all, seg, *gains, tm_s, 1)
    wide = lambda t4: jnp.transpose(t4, (1, 0, 2)).reshape(s, ns, w)
    slabs = lambda t: jnp.transpose(t.reshape(s * ns, NSLAB, LANES), (1, 0, 2))
    to_t = lambda c: jnp.transpose(c[0], (0, 2, 3, 1)).reshape(s, w, la)
    from_t = lambda t: jnp.transpose(t.reshape(s, A_HEADS, HEAD_DIM, la), (0, 3, 1, 2))[None]
    new_t = lambda t: jnp.transpose(t.reshape(t.shape[1], s, ns), (1, 0, 2))
    sak_t, sav_t, oa = _sample_at(wide(qa4), new_t(kat), new_t(vat), to_t(cache_a_k), to_t(cache_a_v))
    sak, sav = from_t(sak_t), from_t(sav_t)
    to_tb = lambda c: jnp.transpose(c[0], (0, 2, 3, 1)).reshape(s, LANES, lb)
    from_tb = lambda t: jnp.transpose(t.reshape(s, B_KV_HEADS, HEAD_DIM, lb), (0, 3, 1, 2))[None]
    sbk_t, sbv_t, ob = _sample_b(sinks_perm, slopes_perm, wide(qb4), new_t(kbt), new_t(vbt),
                                 to_tb(cache_b_k), to_tb(cache_b_v))
    ht_s = _outproj(xs, slabs(oa), slabs(ob), wo_a, wo_b, tm_s)
    y_s = _peer(ht_s, g_col, wq_t, keys, u, vt, **_peer_tiles(s * ns, nk))

    return (y_p.reshape(b, l, d), y_s.reshape(s, ns, d), pak, pav, pbk, pbv,
            sak, sav, from_tb(sbk_t), from_tb(sbv_t))
```

```python
import functools
import math

import jax
import jax.numpy as jnp
from jax import lax
from jax.experimental import pallas as pl
from jax.experimental.pallas import tpu as pltpu

HEAD_DIM = 64
A_HEADS = 8
B_HEADS = 8
B_KV_HEADS = 2
B_GROUP = B_HEADS // B_KV_HEADS
A_PATTERNS = ((128, 1), (512, 4), (2048, 16))
BAND = 128
A_WIDTH = A_HEADS * HEAD_DIM
PEER_HEADS = 8
PEER_TOPK = 16
NORM_EPS = 1e-6
NEG = -1e30
SCALE = HEAD_DIM ** -0.5
LANES = 128
VMEM_LIMIT_BYTES = 56 * 1024 * 1024

BF16 = jnp.bfloat16
F32 = jnp.float32


def _alibi_slopes(n):
    return [2.0 ** (-8.0 * (i + 1) / n) for i in range(n)]


def _cparams(*sem):
    return pltpu.CompilerParams(dimension_semantics=sem, vmem_limit_bytes=VMEM_LIMIT_BYTES)


def _head_rms(h, seg, gain):
    sq = h * h
    hi = sq.astype(BF16)
    lo = (sq - hi.astype(F32)).astype(BF16)
    ms = (jnp.dot(hi, seg, preferred_element_type=F32)
          + jnp.dot(lo, seg, preferred_element_type=F32)) * (1.0 / HEAD_DIM)
    return h * lax.rsqrt(ms + NORM_EPS) * gain


NSLAB = A_WIDTH // LANES


def _proj_kernel(x_ref, g_ref, w_ref, seg_ref, gqa_ref, gka_ref, gqb_ref, gkb_ref,
                 qa4_ref, ka4_ref, va4_ref, qb4_ref, kat_ref, vat_ref, kb_ref, vb_ref, kbt_ref, vbt_ref):
    x = x_ref[...]
    ms = jnp.mean(x * x, axis=-1, keepdims=True)
    xn = (x * lax.rsqrt(ms + NORM_EPS) * g_ref[...]).astype(BF16)
    h = jnp.dot(xn, w_ref[...], preferred_element_type=F32)
    seg = seg_ref[...]
    w = A_WIDTH
    qa = _head_rms(h[:, 0:w], seg, gqa_ref[...])
    ka = _head_rms(h[:, w:2 * w], seg, gka_ref[...])
    va = h[:, 2 * w:3 * w]
    qb = _head_rms(h[:, 3 * w:4 * w], seg, gqb_ref[...])
    kb = _head_rms(h[:, 4 * w:4 * w + LANES], seg[0:LANES, 0:LANES], gkb_ref[...])
    vb = h[:, 4 * w + LANES:4 * w + 2 * LANES]
    kb_ref[...] = kb
    vb_ref[...] = vb
    kat_ref[...] = ka.T
    vat_ref[...] = va.T
    kbt_ref[...] = kb.T
    vbt_ref[...] = vb.T
    for s in range(NSLAB):
        sl = slice(s * LANES, (s + 1) * LANES)
        qa4_ref[s] = qa[:, sl]
        ka4_ref[s] = ka[:, sl]
        va4_ref[s] = va[:, sl]
        qb4_ref[s] = qb[:, sl]


def _project(x2d, norm_g, w_bf16, seg, gqa, gka, gqb, gkb, tm, nseq):
    n, d = x2d.shape
    nc = w_bf16.shape[1]
    w = A_WIDTH
    l = n // nseq
    tps = l // tm
    row = lambda i: (i, 0)
    col = lambda i: (i // tps, 0, i % tps)
    slab = lambda i: (0, i, 0)
    const = lambda i: (0, 0)
    outs = ([jax.ShapeDtypeStruct((NSLAB, n, LANES), F32)] * 4 + [jax.ShapeDtypeStruct((nseq, w, l), F32)] * 2
            + [jax.ShapeDtypeStruct((n, LANES), F32)] * 2 + [jax.ShapeDtypeStruct((nseq, LANES, l), F32)] * 2)
    return pl.pallas_call(
        _proj_kernel,
        grid=(n // tm,),
        in_specs=[pl.BlockSpec((tm, d), row), pl.BlockSpec((1, d), const), pl.BlockSpec((d, nc), const),
                  pl.BlockSpec((w, w), const), pl.BlockSpec((1, w), const), pl.BlockSpec((1, w), const),
                  pl.BlockSpec((1, w), const), pl.BlockSpec((1, LANES), const)],
        out_specs=([pl.BlockSpec((NSLAB, tm, LANES), slab)] * 4 + [pl.BlockSpec((None, w, tm), col)] * 2
                   + [pl.BlockSpec((tm, LANES), row)] * 2 + [pl.BlockSpec((None, LANES, tm), col)] * 2),
        out_shape=outs,
        compiler_params=_cparams("parallel"),
        name="qkv_proj",
    )(x2d, norm_g, w_bf16, seg, gqa, gka, gqb, gkb)


MIXER_UNROLL = 8


def _pick(idx, values):
    out = jnp.float32(values[-1])
    for i in range(len(values) - 2, -1, -1):
        out = jnp.where(idx == i, jnp.float32(values[i]), out)
    return out


def _mixer_kernel(*refs, dils, slopes, with_sink, seq):
    if with_sink:
        sink_ref, refs = refs[0], refs[1:]
    q_ref, k_ref, v_ref, o_ref, kpad, vpad, bias_ref = refs[:7]
    ob_ref, lse_ref = refs[7:9] if len(dils) > 1 else (None, None)
    hp = pl.program_id(1)
    pad = BAND * max(dils)

    @pl.when((pl.program_id(0) == 0) & (hp == 0))
    def _():
        kpad[0:pad, :] = jnp.zeros((pad, LANES), F32)
        vpad[0:pad, :] = jnp.zeros((pad, LANES), F32)

    kpad[pad:pad + seq, :] = k_ref[0, 0]
    vpad[pad:pad + seq, :] = v_ref[0, 0]

    r = lax.broadcasted_iota(jnp.int32, (BAND, 2 * BAND), 0)
    c = lax.broadcasted_iota(jnp.int32, (BAND, 2 * BAND), 1)
    dist = r + BAND - c
    valid = (dist >= 0) & (dist <= BAND)
    for bi, dil in enumerate(dils):
        distf = (dist * dil).astype(F32)
        for e in range(2):
            slope = _pick(hp, [slopes[2 * g + e] for g in range(NSLAB)])
            bias_ref[bi, e] = jnp.where(valid, -slope * distf, NEG)

    lane = lax.broadcasted_iota(jnp.int32, (BAND, LANES), 1)
    hi = lane >= HEAD_DIM
    col = lax.broadcasted_iota(jnp.int32, (1, 2 * BAND), 1)
    prev_cols = (col < BAND).astype(F32)
    nt = (((1,), (1,)), ((), ()))

    for bi, dil in enumerate(dils):
        nblk = seq // (dil * BAND)

        def block(t, bi=bi, dil=dil, nblk=nblk):
            res = t // nblk
            j = t % nblk
            base = res + dil * BAND * j
            if dil == 1:
                base = pl.multiple_of(base, BAND)
                rows = pl.ds(base, BAND)
                win = pl.ds(pl.multiple_of(pad + base - BAND, BAND), 2 * BAND)
            else:
                rows = pl.ds(base, BAND, stride=dil)
                win = pl.ds(pad + base - dil * BAND, 2 * BAND, stride=dil)
            qs = q_ref[0, 0, rows, :]
            kw = kpad[win, :].astype(BF16)
            vw = vpad[win, :].astype(BF16)
            pen = jnp.where(j == 0, NEG, 0.0) * prev_cols
            outs, lses = [], []
            for e in range(2):
                qm = jnp.where(hi == bool(e), qs, 0.0).astype(BF16)
                s = lax.dot_general(qm, kw, nt, preferred_element_type=F32)
                s = s * SCALE + bias_ref[bi, e] + pen
                m = jnp.max(s, axis=-1, keepdims=True)
                if with_sink:
                    sink = sink_ref[2 * hp + e]
                    m = jnp.maximum(m, sink)
                p = jnp.exp(s - m)
                den = jnp.sum(p, axis=-1, keepdims=True)
                if with_sink:
                    den = den + jnp.exp(sink - m)
                o = jnp.dot(p.astype(BF16), vw, preferred_element_type=F32)
                outs.append(o / den)
                lses.append(m + jnp.log(den))
            if len(dils) == 1:
                o_ref[0, 0, rows, :] = jnp.where(hi, outs[1], outs[0])
            else:
                ob_ref[bi, rows, :] = jnp.where(hi, outs[1], outs[0])
                lse_ref[bi, rows, :] = jnp.where(hi, lses[1], lses[0])

        def blocks(t, carry, block=block):
            for uu in range(MIXER_UNROLL):
                block(t * MIXER_UNROLL + uu)
            return carry

        assert (dil * nblk) % MIXER_UNROLL == 0
        lax.fori_loop(0, dil * nblk // MIXER_UNROLL, blocks, 0)

    if len(dils) > 1:
        def merge(t, carry):
            rows = pl.ds(pl.multiple_of(t * BAND, BAND), BAND)
            ls = [lse_ref[bi, rows, :] for bi in range(len(dils))]
            mx = functools.reduce(jnp.maximum, ls)
            ws = [jnp.exp(l - mx) for l in ls]
            num = sum(w * ob_ref[bi, rows, :] for bi, w in enumerate(ws))
            o_ref[0, 0, rows, :] = num / sum(ws)
            return carry

        lax.fori_loop(0, seq // BAND, merge, 0)


def _mixer(q4, k4, v4, *, dils, slopes, sinks=None):
    nslab, b, l, _ = q4.shape
    assert l % (BAND * max(dils)) == 0
    shared = k4.shape[0] == 1
    qmap = lambda bi, hp: (hp, bi, 0, 0)
    kmap = (lambda bi, hp: (0, bi, 0, 0)) if shared else qmap
    blk = (1, 1, l, LANES)
    in_specs = [pl.BlockSpec(blk, qmap), pl.BlockSpec(blk, kmap), pl.BlockSpec(blk, kmap)]
    args = [q4, k4, v4]
    if sinks is not None:
        in_specs = [pl.BlockSpec(memory_space=pltpu.SMEM)] + in_specs
        args = [sinks] + args
    pad = BAND * max(dils)
    scratch = [pltpu.VMEM((pad + l, LANES), F32), pltpu.VMEM((pad + l, LANES), F32),
               pltpu.VMEM((len(dils), 2, BAND, 2 * BAND), F32)]
    if len(dils) > 1:
        scratch += [pltpu.VMEM((len(dils), l, LANES), F32), pltpu.VMEM((len(dils), l, LANES), F32)]
    return pl.pallas_call(
        functools.partial(_mixer_kernel, dils=tuple(dils), slopes=slopes, with_sink=sinks is not None, seq=l),
        grid=(b, nslab),
        in_specs=in_specs,
        out_specs=pl.BlockSpec(blk, qmap),
        out_shape=jax.ShapeDtypeStruct(q4.shape, F32),
        scratch_shapes=scratch,
        compiler_params=_cparams("arbitrary", "arbitrary"),
        name="mixer_b" if shared else "mixer_a",
    )(*args)


def _qb_perm():
    idx = []
    for g in range(B_GROUP):
        for kv in range(B_KV_HEADS):
            base = (kv * B_GROUP + g) * HEAD_DIM
            idx.extend(range(base, base + HEAD_DIM))
    return jnp.asarray(idx, jnp.int32)


NEW = 8
HQ = A_HEADS * NEW


def _block_diag_rows(q8, width):
    rep = jnp.concatenate([q8] * A_HEADS, axis=0)
    r = lax.broadcasted_iota(jnp.int32, (HQ, width), 0)
    c = lax.broadcasted_iota(jnp.int32, (HQ, width), 1)
    return jnp.where(r // NEW == c // HEAD_DIM, rep, 0.0)


def _block_diag_rows_keep(x, width):
    r = lax.broadcasted_iota(jnp.int32, (HQ, width), 0)
    c = lax.broadcasted_iota(jnp.int32, (HQ, width), 1)
    return jnp.where(r // NEW == c // HEAD_DIM, x, 0.0)


ROLL_ROWS = 64


def _sample_at_kernel(q_ref, kn_ref, vn_ref, ck_ref, cv_ref, nk_ref, nv_ref, o_ref, bias_ref, bn_ref, knb, vnb,
                      *, la, slopes):
    w = A_WIDTH
    nbr = len(A_PATTERNS)

    @pl.when(pl.program_id(0) == 0)
    def _():
        r = lax.broadcasted_iota(jnp.int32, (HQ, la), 0)
        c = lax.broadcasted_iota(jnp.int32, (HQ, la), 1)
        dist = (r % NEW) + la - c
        slope = jnp.zeros((HQ, la), F32)
        rn = lax.broadcasted_iota(jnp.int32, (HQ, LANES), 0)
        cn = lax.broadcasted_iota(jnp.int32, (HQ, LANES), 1)
        dn = (rn % NEW) - cn
        sl = jnp.zeros((HQ, LANES), F32)
        for h in range(A_HEADS):
            slope = jnp.where(r // NEW == h, slopes[h], slope)
            sl = jnp.where(rn // NEW == h, slopes[h], sl)
        for bi, (wdw, dil) in enumerate(A_PATTERNS):
            bias_ref[bi] = jnp.where((dist <= wdw) & (dist % dil == 0), -slope * dist.astype(F32), NEG)
            bn_ref[bi] = jnp.where((dn >= 0) & (dn % dil == 0) & (cn < NEW), -sl * dn.astype(F32), NEG)
        knb[...] = jnp.zeros_like(knb)
        vnb[...] = jnp.zeros_like(vnb)

    knb[:, 0:NEW] = kn_ref[0]
    vnb[:, 0:NEW] = vn_ref[0]

    qbd = _block_diag_rows(q_ref[0], w).astype(BF16)
    nt = (((1,), (1,)), ((), ()))
    s_c = jnp.dot(qbd, ck_ref[0].astype(BF16), preferred_element_type=F32) * SCALE
    s_n = jnp.dot(qbd, knb[...].astype(BF16), preferred_element_type=F32) * SCALE
    ms, dens, pcs, pns = [], [], [], []
    for bi in range(nbr):
        sc = s_c + bias_ref[bi]
        sn = s_n + bn_ref[bi]
        m = jnp.maximum(jnp.max(sc, axis=-1, keepdims=True), jnp.max(sn, axis=-1, keepdims=True))
        pc = jnp.exp(sc - m)
        pn = jnp.exp(sn - m)
        dens.append(jnp.sum(pc, axis=-1, keepdims=True) + jnp.sum(pn, axis=-1, keepdims=True))
        ms.append(m)
        pcs.append(pc.astype(BF16))
        pns.append(pn.astype(BF16))
    nums = (lax.dot_general(jnp.concatenate(pcs, axis=0), cv_ref[0].astype(BF16), nt, preferred_element_type=F32)
            + lax.dot_general(jnp.concatenate(pns, axis=0), vnb[...].astype(BF16), nt, preferred_element_type=F32))
    mx = functools.reduce(jnp.maximum, ms)
    den = 0.0
    num = 0.0
    for bi in range(nbr):
        wt = jnp.exp(ms[bi] - mx)
        den = den + dens[bi] * wt
        num = num + nums[bi * HQ:(bi + 1) * HQ, :] * wt
    out = _block_diag_rows_keep(num / den, w)
    o_ref[0] = sum(out[h * NEW:(h + 1) * NEW, :] for h in range(A_HEADS))

    lane = lax.broadcasted_iota(jnp.int32, (ROLL_ROWS, LANES), 1)
    tail = lane >= LANES - NEW
    for src, newb, dst in ((ck_ref, knb, nk_ref), (cv_ref, vnb, nv_ref)):
        for r0 in range(0, w, ROLL_ROWS):
            rows = slice(r0, r0 + ROLL_ROWS)
            rolled = pltpu.roll(src[0, rows, :], la - NEW, 1)
            dst[0, rows, 0:la - LANES] = rolled[:, 0:la - LANES]
            fresh = pltpu.roll(newb[rows, :], LANES - NEW, 1)
            dst[0, rows, la - LANES:la] = jnp.where(tail, fresh, rolled[:, la - LANES:la])


def _sample_at(q, kn_t, vn_t, ck_t, cv_t):
    s, w, la = ck_t.shape
    assert la >= max(wd for (wd, _) in A_PATTERNS) and la % LANES == 0
    seq = lambda i: (i, 0, 0)
    big = pl.BlockSpec((1, w, la), seq)
    new = pl.BlockSpec((1, w, NEW), seq)
    qspec = pl.BlockSpec((1, NEW, w), seq)
    nbr = len(A_PATTERNS)
    return pl.pallas_call(
        functools.partial(_sample_at_kernel, la=la, slopes=_alibi_slopes(A_HEADS)),
        grid=(s,),
        in_specs=[qspec, new, new, big, big],
        out_specs=[big, big, qspec],
        out_shape=[jax.ShapeDtypeStruct(ck_t.shape, F32), jax.ShapeDtypeStruct(cv_t.shape, F32),
                   jax.ShapeDtypeStruct(q.shape, F32)],
        scratch_shapes=[pltpu.VMEM((nbr, HQ, la), F32), pltpu.VMEM((nbr, HQ, LANES), F32),
                        pltpu.VMEM((w, LANES), F32), pltpu.VMEM((w, LANES), F32)],
        compiler_params=_cparams("arbitrary"),
        name="sample_attn_at",
    )(q, kn_t, vn_t, ck_t, cv_t)


def _sample_b_kernel(sink_ref, q_ref, kn_ref, vn_ref, ck_ref, cv_ref, nk_ref, nv_ref, o_ref,
                     bc_ref, bn_ref, sk_ref, knb, vnb, *, lb, nseq, slopes):
    @pl.when(pl.program_id(0) == 0)
    def _():
        r = lax.broadcasted_iota(jnp.int32, (HQ, lb), 0)
        c = lax.broadcasted_iota(jnp.int32, (HQ, lb), 1)
        dist = (r % NEW) + lb - c
        slope = jnp.zeros((HQ, lb), F32)
        sink = jnp.zeros((HQ, LANES), F32)
        rs = lax.broadcasted_iota(jnp.int32, (HQ, LANES), 0)
        for h in range(B_HEADS):
            slope = jnp.where(r // NEW == h, slopes[h], slope)
            sink = jnp.where(rs // NEW == h, sink_ref[h], sink)
        bc_ref[...] = jnp.where(dist <= BAND, -slope * dist.astype(F32), NEG)
        cn = lax.broadcasted_iota(jnp.int32, (HQ, LANES), 1)
        dn = (rs % NEW) - cn
        sl = jnp.zeros((HQ, LANES), F32)
        for h in range(B_HEADS):
            sl = jnp.where(rs // NEW == h, slopes[h], sl)
        bn_ref[...] = jnp.where((dn >= 0) & (cn < NEW), -sl * dn.astype(F32), NEG)
        sk_ref[...] = sink
        knb[...] = jnp.zeros_like(knb)
        vnb[...] = jnp.zeros_like(vnb)

    nt = (((1,), (1,)), ((), ()))
    rr = lax.broadcasted_iota(jnp.int32, (HQ, LANES), 0)
    cc = lax.broadcasted_iota(jnp.int32, (HQ, LANES), 1)
    own = ((rr // NEW) % B_KV_HEADS) == (cc // HEAD_DIM)
    sink = sk_ref[:, 0:1]
    tail = cc >= LANES - NEW
    for t in range(nseq):
        knb[:, 0:NEW] = kn_ref[t]
        vnb[:, 0:NEW] = vn_ref[t]
        for src, newb, dst in ((ck_ref, knb, nk_ref), (cv_ref, vnb, nv_ref)):
            for r0 in range(0, LANES, HQ):
                rows = slice(r0, r0 + HQ)
                dst[t, rows, :] = jnp.where(tail, pltpu.roll(newb[rows, :], LANES - NEW, 1),
                                            pltpu.roll(src[t, rows, :], lb - NEW, 1))
        q = q_ref[t]
        rows = []
        for g in range(B_GROUP):
            for kv in range(B_KV_HEADS):
                rows.append(q[:, g * LANES:(g + 1) * LANES])
        qbd = jnp.where(own, jnp.concatenate(rows, axis=0), 0.0).astype(BF16)
        sc = jnp.dot(qbd, ck_ref[t].astype(BF16), preferred_element_type=F32) * SCALE + bc_ref[...]
        sn = jnp.dot(qbd, knb[...].astype(BF16), preferred_element_type=F32) * SCALE + bn_ref[...]
        m = jnp.maximum(jnp.maximum(jnp.max(sc, axis=-1, keepdims=True), jnp.max(sn, axis=-1, keepdims=True)), sink)
        pc = jnp.exp(sc - m)
        pn = jnp.exp(sn - m)
        den = jnp.sum(pc, axis=-1, keepdims=True) + jnp.sum(pn, axis=-1, keepdims=True) + jnp.exp(sink - m)
        o = (lax.dot_general(pc.astype(BF16), cv_ref[t].astype(BF16), nt, preferred_element_type=F32)
             + lax.dot_general(pn.astype(BF16), vnb[...].astype(BF16), nt, preferred_element_type=F32)) / den
        o = jnp.where(own, o, 0.0)
        for g in range(B_GROUP):
            base = g * B_KV_HEADS * NEW
            o_ref[t, :, g * LANES:(g + 1) * LANES] = o[base:base + NEW, :] + o[base + NEW:base + 2 * NEW, :]


def _sample_b(sinks_perm, slopes_perm, q, kn, vn, ck, cv, nseq=8):
    s, kw, lb = ck.shape
    assert lb == BAND and kw == LANES and lb == LANES and s % nseq == 0
    seq = lambda i: (i, 0, 0)
    return pl.pallas_call(
        functools.partial(_sample_b_kernel, lb=lb, nseq=nseq, slopes=slopes_perm),
        grid=(s // nseq,),
        in_specs=[pl.BlockSpec(memory_space=pltpu.SMEM), pl.BlockSpec((nseq, NEW, A_WIDTH), seq),
                  pl.BlockSpec((nseq, kw, NEW), seq), pl.BlockSpec((nseq, kw, NEW), seq),
                  pl.BlockSpec((nseq, kw, lb), seq), pl.BlockSpec((nseq, kw, lb), seq)],
        out_specs=[pl.BlockSpec((nseq, kw, lb), seq), pl.BlockSpec((nseq, kw, lb), seq),
                   pl.BlockSpec((nseq, NEW, A_WIDTH), seq)],
        out_shape=[jax.ShapeDtypeStruct(ck.shape, F32), jax.ShapeDtypeStruct(cv.shape, F32),
                   jax.ShapeDtypeStruct(q.shape, F32)],
        scratch_shapes=[pltpu.VMEM((HQ, lb), F32), pltpu.VMEM((HQ, LANES), F32), pltpu.VMEM((HQ, LANES), F32),
                        pltpu.VMEM((LANES, kw), F32), pltpu.VMEM((LANES, kw), F32)],
        compiler_params=_cparams("arbitrary"),
        name="sample_attn_b",
    )(sinks_perm, q, kn, vn, ck, cv)


def _outproj_kernel(x_ref, oa_ref, ob_ref, wa_ref, wb_ref, ht_ref):
    oa = jnp.concatenate([oa_ref[s] for s in range(NSLAB)], axis=1).astype(BF16)
    ob = jnp.concatenate([ob_ref[s] for s in range(NSLAB)], axis=1).astype(BF16)
    h = (x_ref[...] + jnp.dot(oa, wa_ref[...], preferred_element_type=F32)
         + jnp.dot(ob, wb_ref[...], preferred_element_type=F32))
    ht_ref[...] = h.T


def _outproj(x2d, oa4, ob4, wa, wb, tm):
    n, d = x2d.shape
    w = A_WIDTH
    row = lambda i: (i, 0)
    const = lambda i: (0, 0)
    slab = pl.BlockSpec((NSLAB, tm, LANES), lambda i: (0, i, 0))
    return pl.pallas_call(
        _outproj_kernel,
        grid=(n // tm,),
        in_specs=[pl.BlockSpec((tm, d), row), slab, slab, pl.BlockSpec((w, d), const), pl.BlockSpec((w, d), const)],
        out_specs=pl.BlockSpec((d, tm), lambda i: (0, i)),
        out_shape=jax.ShapeDtypeStruct((d, n), F32),
        compiler_params=_cparams("parallel"),
        name="out_proj",
    )(x2d, oa4, ob4, wa, wb)


SUBLANES = 8


def _merge_exchange(n):
    pairs = []
    t = max(1, math.ceil(math.log2(n)))
    p = 1 << (t - 1)
    while p > 0:
        q, r, d = 1 << (t - 1), 0, p
        while d > 0:
            pairs.extend((i, i + d) for i in range(n - d) if (i & p) == r)
            d, q, r = q - p, q >> 1, p
        p >>= 1
    return pairs


def _vmax(a, b):
    if a is None:
        return b
    if b is None:
        return a
    return jnp.maximum(a, b)


def _vmin(a, b):
    if a is None or b is None:
        return None
    return jnp.minimum(a, b)


def _exchange(x, i, j):
    x[i], x[j] = _vmax(x[i], x[j]), _vmin(x[i], x[j])


def _top16(tiles):
    x = list(tiles) + [None] * (PEER_TOPK - len(tiles))
    for (i, j) in _merge_exchange(len(tiles)):
        _exchange(x, i, j)
    for shift in (4, 2, 1):
        y = [None if v is None else pltpu.roll(v, shift, 0) for v in x]
        x = [_vmax(x[k], y[PEER_TOPK - 1 - k]) for k in range(PEER_TOPK)]
        for d in (8, 4, 2, 1):
            for i in range(PEER_TOPK):
                if not i & d:
                    _exchange(x, i, i + d)
    return x


def _count_prefix(pred, vals):
    sel = jnp.where
    c16 = pred(vals[15])
    c8 = pred(vals[7])
    c4 = pred(sel(c8, vals[11], vals[3]))
    c2 = pred(sel(c8, sel(c4, vals[13], vals[9]), sel(c4, vals[5], vals[1])))
    c1 = pred(sel(c8, sel(c4, sel(c2, vals[14], vals[12]), sel(c2, vals[10], vals[8])),
                  sel(c4, sel(c2, vals[6], vals[4]), sel(c2, vals[2], vals[0]))))
    lo = sel(c8, 8.0, 0.0) + sel(c4, 4.0, 0.0) + sel(c2, 2.0, 0.0) + sel(c1, 1.0, 0.0)
    return sel(c16, float(PEER_TOPK), lo)


def _rows_sum(x):
    for shift in (4, 2, 1):
        x = x + pltpu.roll(x, shift, 0)
    return x


def _peer_select_kernel(ht_ref, g_ref, wq_ref, keys_ref,
                        hn_ref, r1_ref, w1_ref, cnt_ref, w0_ref, s_ref, *, nk, tn):
    h = ht_ref[...]
    ms = jnp.mean(h * h, axis=0, keepdims=True)
    hn = (h * lax.rsqrt(ms + NORM_EPS) * g_ref[...]).astype(BF16)
    hn_ref[...] = hn
    q = jnp.dot(wq_ref[...], hn, preferred_element_type=F32)
    half = q.shape[0] // (2 * PEER_HEADS)
    for k in range(2 * PEER_HEADS):
        qk = q[k * half:(k + 1) * half, :].astype(BF16)
        s_ref[k] = jnp.dot(keys_ref[k], qk, preferred_element_type=F32)
    nchunk = tn // LANES
    ntile = nk // SUBLANES
    sub = lax.broadcasted_iota(jnp.int32, (SUBLANES, LANES), 0)

    def pack(vals):
        out = vals[-1]
        for r in range(len(vals) - 2, -1, -1):
            out = jnp.where(sub == r, vals[r], out)
        return out

    def head_chunk(t, carry):
        hd = t // nchunk
        lanes = pl.ds(pl.multiple_of((t % nchunk) * LANES, LANES), LANES)
        rows = [slice(k * SUBLANES, (k + 1) * SUBLANES) for k in range(ntile)]
        s0 = [s_ref[2 * hd, r, lanes] for r in rows]
        s1 = [s_ref[2 * hd + 1, r, lanes] for r in rows]
        v0 = _top16(s0)
        v1 = _top16(s1)
        v1lo, v1hi, v0hi = pack(v1[0:8]), pack(v1[8:16]), pack(v0[8:16])
        cands = [v0[0] + v1lo, v0[0] + v1hi] + [v0[a] + v1lo for a in range(1, 8)] + [v0hi + v1[0]]
        best = _top16(cands)
        top, tau = best[0], best[PEER_TOPK - 1]
        z = _rows_sum(sum(jnp.where(c >= tau, jnp.exp(c - top), 0.0) for c in cands))
        inv_z = 1.0 / z
        for m in range(ntile // 2):
            cnt, rk1 = [], []
            for k in (2 * m, 2 * m + 1):
                cnt.append(_count_prefix(lambda t, k=k: s0[k] + t >= tau, v1))
                rk1.append(_count_prefix(lambda t, k=k: t > s1[k], v1))
            pair = slice(2 * m * SUBLANES, (2 * m + 2) * SUBLANES)
            both = lambda f: jnp.concatenate([f(2 * m), f(2 * m + 1)], axis=0)
            cnt_ref[hd, pair, lanes] = jnp.concatenate(cnt, axis=0)
            w0_ref[hd, pair, lanes] = both(lambda k: jnp.exp(s0[k] - v0[0]) * inv_z)
            r1_ref[hd, pair, lanes] = jnp.concatenate(rk1, axis=0).astype(BF16)
            w1_ref[hd, pair, lanes] = both(lambda k: jnp.exp(s1[k] - v1[0])).astype(BF16)
        return carry

    lax.fori_loop(0, PEER_HEADS * nchunk, head_chunk, 0)


def _peer_select(ht, g_col, wq_t, keys, tn):
    d, n = ht.shape
    assert tn % LANES == 0 and n % tn == 0
    nslab, nk, half = keys.shape
    tok = lambda i: (0, i)
    tok3 = lambda i: (0, 0, i)
    stat = lambda dt: jax.ShapeDtypeStruct((PEER_HEADS, nk, n), dt)
    return pl.pallas_call(
        functools.partial(_peer_select_kernel, nk=nk, tn=tn),
        grid=(n // tn,),
        in_specs=[pl.BlockSpec((d, tn), tok), pl.BlockSpec((d, 1), lambda i: (0, 0)),
                  pl.BlockSpec(wq_t.shape, lambda i: (0, 0)), pl.BlockSpec(keys.shape, lambda i: (0, 0, 0))],
        out_specs=[pl.BlockSpec((d, tn), tok)] + [pl.BlockSpec((PEER_HEADS, nk, tn), tok3)] * 4,
        out_shape=[jax.ShapeDtypeStruct((d, n), BF16), stat(BF16), stat(BF16), stat(F32), stat(F32)],
        scratch_shapes=[pltpu.VMEM((nslab, nk, tn), F32)],
        compiler_params=_cparams("parallel"),
        name="peer_select",
    )(ht, g_col, wq_t, keys)


def _gelu(x):
    return 0.5 * x * (1.0 + lax.erf(x * (2.0 ** -0.5)))


def _peer_dense_kernel(hn_ref, u_ref, vt_ref, r1_ref, w1_ref, cnt_ref, w0_ref, ht_ref, y_ref,
                       acc_ref, c0_ref, c1_ref, g_ref, *, nk, te, tn, lc, rb, ne, nsteps):
    s = pl.program_id(0)

    @pl.when(s == 0)
    def _():
        for r in (c0_ref, c1_ref, acc_ref):
            r[...] = jnp.zeros_like(r)

    e = jnp.minimum(s, nsteps - 1) % ne
    p1 = s - 1
    first = (jnp.maximum(p1, 0) % ne) == 0
    nslab = te // nk

    def stages(c_new, c_prev):
        ob = acc_ref.shape[0] // nslab
        for i2 in range(nslab // SLAB_GROUP):
            slabs = range(i2 * SLAB_GROUP, (i2 + 1) * SLAB_GROUP)
            orows = slice(i2 * SLAB_GROUP * ob, (i2 + 1) * SLAB_GROUP * ob)
            contrib = jnp.dot(vt_ref[orows, :], c_prev[...], preferred_element_type=F32)
            acc_ref[orows, :] = jnp.where(first, contrib, acc_ref[orows, :] + contrib)
            bits = lax.bitcast_convert_type(contrib[0:1, :], jnp.int32)
            zero = lax.shift_right_logical(lax.shift_right_logical(bits, 16), 16).astype(F32)
            cnt_rows = [[(cnt_ref[hd, pl.ds(e * nslab + ii, 1), :] + zero).astype(BF16)
                         for hd in range(PEER_HEADS)] for ii in slabs]
            w0_rows = [[w0_ref[hd, pl.ds(e * nslab + ii, 1), :].astype(BF16)
                        for hd in range(PEER_HEADS)] for ii in slabs]
            for c in range(tn // lc):
                ls = slice(c * lc, (c + 1) * lc)
                for jb in range(nk // rb):
                    js = slice(jb * rb, (jb + 1) * rb)
                    gs = [None] * SLAB_GROUP
                    for hd in range(PEER_HEADS):
                        r1 = r1_ref[hd, js, ls]
                        w1 = w1_ref[hd, js, ls]
                        for k in range(SLAB_GROUP):
                            term = jnp.where(r1 < cnt_rows[k][hd][:, ls], w1, jnp.zeros((), BF16)) * w0_rows[k][hd][:, ls]
                            gs[k] = term if gs[k] is None else gs[k] + term
                    for k, ii in enumerate(slabs):
                        g_ref[ii * nk + jb * rb:ii * nk + (jb + 1) * rb, ls] = gs[k]
        for ii in range(nslab // SLAB_GROUP):
            rows = slice(ii * SLAB_GROUP * nk, (ii + 1) * SLAB_GROUP * nk)
            a = jnp.dot(u_ref[rows, :], hn_ref[...], preferred_element_type=F32)
            c_new[rows, :] = g_ref[rows, :] * _gelu(a).astype(BF16)

    @pl.when(s % 2 == 0)
    def _():
        stages(c0_ref, c1_ref)

    @pl.when(s % 2 == 1)
    def _():
        stages(c1_ref, c0_ref)

    @pl.when((p1 >= 0) & (p1 % ne == ne - 1))
    def _():
        y_ref[...] = (ht_ref[...] + acc_ref[...]).T


def _peer_dense(hn, u, vt, r1, w1, cnt, w0, ht, *, tn, te, lc, rb):
    d, n = hn.shape
    ne = u.shape[0] // te
    nk = r1.shape[1]
    nsteps = (n // tn) * ne
    last = nsteps - 1
    pair = lambda s, lag: jnp.clip(s - lag, 0, last)
    stat = pl.BlockSpec((PEER_HEADS, nk, tn), lambda s: (0, 0, pair(s, 0) // ne))
    return pl.pallas_call(
        functools.partial(_peer_dense_kernel, nk=nk, te=te, tn=tn, lc=lc, rb=min(rb, nk), ne=ne, nsteps=nsteps),
        grid=(nsteps + 1,),
        in_specs=[pl.BlockSpec((d, tn), lambda s: (0, pair(s, 0) // ne)),
                  pl.BlockSpec((te, d), lambda s: (pair(s, 0) % ne, 0)),
                  pl.BlockSpec((d, te), lambda s: (0, pair(s, 1) % ne)),
                  stat, stat, stat, stat,
                  pl.BlockSpec((d, tn), lambda s: (0, pair(s, 1) // ne))],
        out_specs=pl.BlockSpec((tn, d), lambda s: (pair(s, 1) // ne, 0)),
        out_shape=jax.ShapeDtypeStruct((n, d), F32),
        scratch_shapes=[pltpu.VMEM((d, tn), F32), pltpu.VMEM((te, tn), BF16), pltpu.VMEM((te, tn), BF16),
                        pltpu.VMEM((te, tn), BF16)],
        compiler_params=_cparams("arbitrary"),
        name="peer_dense",
    )(hn, u, vt, r1, w1, cnt, w0, ht)


def _peer(ht, g_col, wq_t, keys, u, vt, *, tn_sel, tn, te, lc, rb):
    hn, r1, w1, cnt, w0 = _peer_select(ht, g_col, wq_t, keys, tn_sel)
    return _peer_dense(hn, u, vt, r1, w1, cnt, w0, ht, tn=tn, te=te, lc=lc, rb=rb)


TOKEN_TILE = 512
EXPERT_TILE = 1024
GATE_LANES = 256
GATE_ROWS = 64
SLAB_GROUP = 4


def _peer_tiles(n_tokens, n_keys):
    tn = min(TOKEN_TILE, n_tokens)
    return dict(tn_sel=tn, tn=tn, te=min(EXPERT_TILE, n_keys * n_keys), lc=min(GATE_LANES, tn), rb=GATE_ROWS)


def kernel(x_prompt, x_sample, cache_a_k, cache_a_v, cache_b_k, cache_b_v, norm_attn, w_in, g_qa, g_ka, g_qb, g_kb, sinks, w_o, norm_ffn, peer_wq, peer_keys, peer_u, peer_v):
    b, l, d = x_prompt.shape
    s, ns, _ = x_sample.shape
    assert ns == NEW and w_in.shape[0] == 1
    la, lb = cache_a_k.shape[2], cache_b_k.shape[2]
    w = A_WIDTH
    perm = _qb_perm()

    wl = w_in[0]
    w_all = jnp.concatenate([wl[:, :3 * w], wl[:, 3 * w:4 * w][:, perm], wl[:, 4 * w:]], axis=1).astype(BF16)
    seg = (jnp.arange(w)[:, None] // HEAD_DIM == jnp.arange(w)[None, :] // HEAD_DIM).astype(BF16)
    t8 = lambda g: jnp.tile(g, A_HEADS)[None, :]
    gains = (t8(g_qa[0]), t8(g_ka[0]), t8(g_qb[0]), jnp.tile(g_kb[0], B_KV_HEADS)[None, :])
    sb = _alibi_slopes(B_HEADS)
    slopes_perm = [sb[(h % 2) * B_GROUP + h // 2] for h in range(B_HEADS)]
    sinks_perm = jnp.stack([sinks[0, (h % 2) * B_GROUP + h // 2] for h in range(B_HEADS)])
    wo_a = w_o[0, :w, :].astype(BF16)
    wo_b = w_o[0, w:, :][perm, :].astype(BF16)
    g_col = norm_ffn[0][:, None]
    wq_t = peer_wq[0].T.astype(BF16)
    nk = peer_keys.shape[3]
    keys = peer_keys[0].reshape(2 * PEER_HEADS, nk, peer_keys.shape[4]).astype(BF16)
    u = peer_u[0].astype(BF16)
    vt = peer_v[0].T.astype(BF16)

    xp = x_prompt.reshape(b * l, d)
    tm_p = min(TOKEN_TILE, l)
    qa4, ka4, va4, qb4, kat, vat, kb, vb, kbt, vbt = _project(xp, norm_attn, w_all, seg, *gains, tm_p, b)
    seq4 = lambda t: t.reshape(t.shape[0], b, l, LANES)
    oa4 = _mixer(seq4(qa4), seq4(ka4), seq4(va4), dils=[dl for (_, dl) in A_PATTERNS],
                 slopes=_alibi_slopes(A_HEADS))
    ob4 = _mixer(seq4(qb4), kb.reshape(1, b, l, LANES), vb.reshape(1, b, l, LANES), dils=[1],
                 slopes=slopes_perm, sinks=sinks_perm)
    ht_p = _outproj(xp, oa4.reshape(NSLAB, b * l, LANES), ob4.reshape(NSLAB, b * l, LANES), wo_a, wo_b, tm_p)
    y_p = _peer(ht_p, g_col, wq_t, keys, u, vt, **_peer_tiles(b * l, nk))
    na = min(la, l)
    nb = min(lb, l)
    def tail(t, heads, keep):
        t = t[:, :, l - keep:].reshape(b, heads, HEAD_DIM, keep)
        return jnp.transpose(t, (0, 3, 1, 2))[None]
    pak, pav = tail(kat, A_HEADS, na), tail(vat, A_HEADS, na)
    pbk, pbv = tail(kbt, B_KV_HEADS, nb), tail(vbt, B_KV_HEADS, nb)

    xs = x_sample.reshape(s * ns, d)
    tm_s = min(TOKEN_TILE, s * ns)
    qa4, _, _, qb4, kat, vat, _, _, kbt, vbt = _project(xs, norm_attn, w_all, seg, *gains, tm_s, 1)
    wide = lambda t4: jnp.transpose(t4, (1, 0, 2)).reshape(s, ns, w)
    slabs = lambda t: jnp.transpose(t.reshape(s * ns, NSLAB, LANES), (1, 0, 2))
    to_t = lambda c: jnp.transpose(c[0], (0, 2, 3, 1)).reshape(s, w, la)
    from_t = lambda t: jnp.transpose(t.reshape(s, A_HEADS, HEAD_DIM, la), (0, 3, 1, 2))[None]
    new_t = lambda t: jnp.transpose(t.reshape(t.shape[1], s, ns), (1, 0, 2))
    sak_t, sav_t, oa = _sample_at(wide(qa4), new_t(kat), new_t(vat), to_t(cache_a_k), to_t(cache_a_v))
    sak, sav = from_t(sak_t), from_t(sav_t)
    to_tb = lambda c: jnp.transpose(c[0], (0, 2, 3, 1)).reshape(s, LANES, lb)
    from_tb = lambda t: jnp.transpose(t.reshape(s, B_KV_HEADS, HEAD_DIM, lb), (0, 3, 1, 2))[None]
    sbk_t, sbv_t, ob = _sample_b(sinks_perm, slopes_perm, wide(qb4), new_t(kbt), new_t(vbt),
                                 to_tb(cache_b_k), to_tb(cache_b_v))
    ht_s = _outproj(xs, slabs(oa), slabs(ob), wo_a, wo_b, tm_s)
    y_s = _peer(ht_s, g_col, wq_t, keys, u, vt, **_peer_tiles(s * ns, nk))

    return (y_p.reshape(b, l, d), y_s.reshape(s, ns, d), pak, pav, pbk, pbv,
            sak, sav, from_tb(sbk_t), from_tb(sbv_t))
```

```python
import functools
import math

import jax
import jax.numpy as jnp
from jax import lax
from jax.experimental import pallas as pl
from jax.experimental.pallas import tpu as pltpu

HEAD_DIM = 64
A_HEADS = 8
B_HEADS = 8
B_KV_HEADS = 2
B_GROUP = B_HEADS // B_KV_HEADS
A_PATTERNS = ((128, 1), (512, 4), (2048, 16))
BAND = 128
A_WIDTH = A_HEADS * HEAD_DIM
PEER_HEADS = 8
PEER_TOPK = 16
NORM_EPS = 1e-6
NEG = -1e30
SCALE = HEAD_DIM ** -0.5
LANES = 128
VMEM_LIMIT_BYTES = 56 * 1024 * 1024

BF16 = jnp.bfloat16
F32 = jnp.float32


def _alibi_slopes(n):
    return [2.0 ** (-8.0 * (i + 1) / n) for i in range(n)]


def _cparams(*sem):
    return pltpu.CompilerParams(dimension_semantics=sem, vmem_limit_bytes=VMEM_LIMIT_BYTES)


def _head_rms(h, seg, gain):
    sq = h * h
    hi = sq.astype(BF16)
    lo = (sq - hi.astype(F32)).astype(BF16)
    ms = (jnp.dot(hi, seg, preferred_element_type=F32)
          + jnp.dot(lo, seg, preferred_element_type=F32)) * (1.0 / HEAD_DIM)
    return h * lax.rsqrt(ms + NORM_EPS) * gain


NSLAB = A_WIDTH // LANES


def _proj_kernel(x_ref, g_ref, w_ref, seg_ref, gqa_ref, gka_ref, gqb_ref, gkb_ref,
                 qa4_ref, ka4_ref, va4_ref, qb4_ref, kat_ref, vat_ref, kb_ref, vb_ref, kbt_ref, vbt_ref):
    x = x_ref[...]
    ms = jnp.mean(x * x, axis=-1, keepdims=True)
    xn = (x * lax.rsqrt(ms + NORM_EPS) * g_ref[...]).astype(BF16)
    h = jnp.dot(xn, w_ref[...], preferred_element_type=F32)
    seg = seg_ref[...]
    w = A_WIDTH
    qa = _head_rms(h[:, 0:w], seg, gqa_ref[...])
    ka = _head_rms(h[:, w:2 * w], seg, gka_ref[...])
    va = h[:, 2 * w:3 * w]
    qb = _head_rms(h[:, 3 * w:4 * w], seg, gqb_ref[...])
    kb = _head_rms(h[:, 4 * w:4 * w + LANES], seg[0:LANES, 0:LANES], gkb_ref[...])
    vb = h[:, 4 * w + LANES:4 * w + 2 * LANES]
    kb_ref[...] = kb
    vb_ref[...] = vb
    kat_ref[...] = ka.T
    vat_ref[...] = va.T
    kbt_ref[...] = kb.T
    vbt_ref[...] = vb.T
    for s in range(NSLAB):
        sl = slice(s * LANES, (s + 1) * LANES)
        qa4_ref[s] = qa[:, sl]
        ka4_ref[s] = ka[:, sl]
        va4_ref[s] = va[:, sl]
        qb4_ref[s] = qb[:, sl]


def _project(x2d, norm_g, w_bf16, seg, gqa, gka, gqb, gkb, tm, nseq):
    n, d = x2d.shape
    nc = w_bf16.shape[1]
    w = A_WIDTH
    l = n // nseq
    tps = l // tm
    row = lambda i: (i, 0)
    col = lambda i: (i // tps, 0, i % tps)
    slab = lambda i: (0, i, 0)
    const = lambda i: (0, 0)
    outs = ([jax.ShapeDtypeStruct((NSLAB, n, LANES), F32)] * 4 + [jax.ShapeDtypeStruct((nseq, w, l), F32)] * 2
            + [jax.ShapeDtypeStruct((n, LANES), F32)] * 2 + [jax.ShapeDtypeStruct((nseq, LANES, l), F32)] * 2)
    return pl.pallas_call(
        _proj_kernel,
        grid=(n // tm,),
        in_specs=[pl.BlockSpec((tm, d), row), pl.BlockSpec((1, d), const), pl.BlockSpec((d, nc), const),
                  pl.BlockSpec((w, w), const), pl.BlockSpec((1, w), const), pl.BlockSpec((1, w), const),
                  pl.BlockSpec((1, w), const), pl.BlockSpec((1, LANES), const)],
        out_specs=([pl.BlockSpec((NSLAB, tm, LANES), slab)] * 4 + [pl.BlockSpec((None, w, tm), col)] * 2
                   + [pl.BlockSpec((tm, LANES), row)] * 2 + [pl.BlockSpec((None, LANES, tm), col)] * 2),
        out_shape=outs,
        compiler_params=_cparams("parallel"),
        name="qkv_proj",
    )(x2d, norm_g, w_bf16, seg, gqa, gka, gqb, gkb)


MIXER_UNROLL = 8


def _pick(idx, values):
    out = jnp.float32(values[-1])
    for i in range(len(values) - 2, -1, -1):
        out = jnp.where(idx == i, jnp.float32(values[i]), out)
    return out


def _mixer_kernel(*refs, dils, slopes, with_sink, seq):
    if with_sink:
        sink_ref, refs = refs[0], refs[1:]
    q_ref, k_ref, v_ref, o_ref, kpad, vpad, bias_ref = refs[:7]
    ob_ref, lse_ref = refs[7:9] if len(dils) > 1 else (None, None)
    hp = pl.program_id(1)
    pad = BAND * max(dils)

    @pl.when((pl.program_id(0) == 0) & (hp == 0))
    def _():
        kpad[0:pad, :] = jnp.zeros((pad, LANES), F32)
        vpad[0:pad, :] = jnp.zeros((pad, LANES), F32)

    kpad[pad:pad + seq, :] = k_ref[0, 0]
    vpad[pad:pad + seq, :] = v_ref[0, 0]

    r = lax.broadcasted_iota(jnp.int32, (BAND, 2 * BAND), 0)
    c = lax.broadcasted_iota(jnp.int32, (BAND, 2 * BAND), 1)
    dist = r + BAND - c
    valid = (dist >= 0) & (dist <= BAND)
    for bi, dil in enumerate(dils):
        distf = (dist * dil).astype(F32)
        for e in range(2):
            slope = _pick(hp, [slopes[2 * g + e] for g in range(NSLAB)])
            bias_ref[bi, e] = jnp.where(valid, -slope * distf, NEG)

    lane = lax.broadcasted_iota(jnp.int32, (BAND, LANES), 1)
    hi = lane >= HEAD_DIM
    col = lax.broadcasted_iota(jnp.int32, (1, 2 * BAND), 1)
    prev_cols = (col < BAND).astype(F32)
    nt = (((1,), (1,)), ((), ()))

    for bi, dil in enumerate(dils):
        nblk = seq // (dil * BAND)

        def block(t, bi=bi, dil=dil, nblk=nblk):
            res = t // nblk
            j = t % nblk
            base = res + dil * BAND * j
            if dil == 1:
                base = pl.multiple_of(base, BAND)
                rows = pl.ds(base, BAND)
                win = pl.ds(pl.multiple_of(pad + base - BAND, BAND), 2 * BAND)
            else:
                rows = pl.ds(base, BAND, stride=dil)
                win = pl.ds(pad + base - dil * BAND, 2 * BAND, stride=dil)
            qs = q_ref[0, 0, rows, :]
            kw = kpad[win, :].astype(BF16)
            vw = vpad[win, :].astype(BF16)
            pen = jnp.where(j == 0, NEG, 0.0) * prev_cols
            outs, lses = [], []
            for e in range(2):
                qm = jnp.where(hi == bool(e), qs, 0.0).astype(BF16)
                s = lax.dot_general(qm, kw, nt, preferred_element_type=F32)
                s = s * SCALE + bias_ref[bi, e] + pen
                m = jnp.max(s, axis=-1, keepdims=True)
                if with_sink:
                    sink = sink_ref[2 * hp + e]
                    m = jnp.maximum(m, sink)
                p = jnp.exp(s - m)
                den = jnp.sum(p, axis=-1, keepdims=True)
                if with_sink:
                    den = den + jnp.exp(sink - m)
                o = jnp.dot(p.astype(BF16), vw, preferred_element_type=F32)
                outs.append(o / den)
                lses.append(m + jnp.log(den))
            if len(dils) == 1:
                o_ref[0, 0, rows, :] = jnp.where(hi, outs[1], outs[0])
            else:
                ob_ref[bi, rows, :] = jnp.where(hi, outs[1], outs[0])
                lse_ref[bi, rows, :] = jnp.where(hi, lses[1], lses[0])

        def blocks(t, carry, block=block):
            for uu in range(MIXER_UNROLL):
                block(t * MIXER_UNROLL + uu)
            return carry

        assert (dil * nblk) % MIXER_UNROLL == 0
        lax.fori_loop(0, dil * nblk // MIXER_UNROLL, blocks, 0)

    if len(dils) > 1:
        def merge(t, carry):
            rows = pl.ds(pl.multiple_of(t * BAND, BAND), BAND)
            ls = [lse_ref[bi, rows, :] for bi in range(len(dils))]
            mx = functools.reduce(jnp.maximum, ls)
            ws = [jnp.exp(l - mx) for l in ls]
            num = sum(w * ob_ref[bi, rows, :] for bi, w in enumerate(ws))
            o_ref[0, 0, rows, :] = num / sum(ws)
            return carry

        lax.fori_loop(0, seq // BAND, merge, 0)


def _mixer(q4, k4, v4, *, dils, slopes, sinks=None):
    nslab, b, l, _ = q4.shape
    assert l % (BAND * max(dils)) == 0
    shared = k4.shape[0] == 1
    qmap = lambda bi, hp: (hp, bi, 0, 0)
    kmap = (lambda bi, hp: (0, bi, 0, 0)) if shared else qmap
    blk = (1, 1, l, LANES)
    in_specs = [pl.BlockSpec(blk, qmap), pl.BlockSpec(blk, kmap), pl.BlockSpec(blk, kmap)]
    args = [q4, k4, v4]
    if sinks is not None:
        in_specs = [pl.BlockSpec(memory_space=pltpu.SMEM)] + in_specs
        args = [sinks] + args
    pad = BAND * max(dils)
    scratch = [pltpu.VMEM((pad + l, LANES), F32), pltpu.VMEM((pad + l, LANES), F32),
               pltpu.VMEM((len(dils), 2, BAND, 2 * BAND), F32)]
    if len(dils) > 1:
        scratch += [pltpu.VMEM((len(dils), l, LANES), F32), pltpu.VMEM((len(dils), l, LANES), F32)]
    return pl.pallas_call(
        functools.partial(_mixer_kernel, dils=tuple(dils), slopes=slopes, with_sink=sinks is not None, seq=l),
        grid=(b, nslab),
        in_specs=in_specs,
        out_specs=pl.BlockSpec(blk, qmap),
        out_shape=jax.ShapeDtypeStruct(q4.shape, F32),
        scratch_shapes=scratch,
        compiler_params=_cparams("arbitrary", "arbitrary"),
        name="mixer_b" if shared else "mixer_a",
    )(*args)


def _qb_perm():
    idx = []
    for g in range(B_GROUP):
        for kv in range(B_KV_HEADS):
            base = (kv * B_GROUP + g) * HEAD_DIM
            idx.extend(range(base, base + HEAD_DIM))
    return jnp.asarray(idx, jnp.int32)


NEW = 8
HQ = A_HEADS * NEW


def _block_diag_rows(q8, width):
    rep = jnp.concatenate([q8] * A_HEADS, axis=0)
    r = lax.broadcasted_iota(jnp.int32, (HQ, width), 0)
    c = lax.broadcasted_iota(jnp.int32, (HQ, width), 1)
    return jnp.where(r // NEW == c // HEAD_DIM, rep, 0.0)


def _block_diag_rows_keep(x, width):
    r = lax.broadcasted_iota(jnp.int32, (HQ, width), 0)
    c = lax.broadcasted_iota(jnp.int32, (HQ, width), 1)
    return jnp.where(r // NEW == c // HEAD_DIM, x, 0.0)


ROLL_ROWS = 64


def _sample_at_kernel(q_ref, kn_ref, vn_ref, ck_ref, cv_ref, nk_ref, nv_ref, o_ref, bias_ref, bn_ref, knb, vnb,
                      *, la, slopes):
    w = A_WIDTH
    nbr = len(A_PATTERNS)

    @pl.when(pl.program_id(0) == 0)
    def _():
        r = lax.broadcasted_iota(jnp.int32, (HQ, la), 0)
        c = lax.broadcasted_iota(jnp.int32, (HQ, la), 1)
        dist = (r % NEW) + la - c
        slope = jnp.zeros((HQ, la), F32)
        rn = lax.broadcasted_iota(jnp.int32, (HQ, LANES), 0)
        cn = lax.broadcasted_iota(jnp.int32, (HQ, LANES), 1)
        dn = (rn % NEW) - cn
        sl = jnp.zeros((HQ, LANES), F32)
        for h in range(A_HEADS):
            slope = jnp.where(r // NEW == h, slopes[h], slope)
            sl = jnp.where(rn // NEW == h, slopes[h], sl)
        for bi, (wdw, dil) in enumerate(A_PATTERNS):
            bias_ref[bi] = jnp.where((dist <= wdw) & (dist % dil == 0), -slope * dist.astype(F32), NEG)
            bn_ref[bi] = jnp.where((dn >= 0) & (dn % dil == 0) & (cn < NEW), -sl * dn.astype(F32), NEG)
        knb[...] = jnp.zeros_like(knb)
        vnb[...] = jnp.zeros_like(vnb)

    knb[:, 0:NEW] = kn_ref[0]
    vnb[:, 0:NEW] = vn_ref[0]

    qbd = _block_diag_rows(q_ref[0], w).astype(BF16)
    nt = (((1,), (1,)), ((), ()))
    s_c = jnp.dot(qbd, ck_ref[0].astype(BF16), preferred_element_type=F32) * SCALE
    s_n = jnp.dot(qbd, knb[...].astype(BF16), preferred_element_type=F32) * SCALE
    ms, dens, pcs, pns = [], [], [], []
    for bi in range(nbr):
        sc = s_c + bias_ref[bi]
        sn = s_n + bn_ref[bi]
        m = jnp.maximum(jnp.max(sc, axis=-1, keepdims=True), jnp.max(sn, axis=-1, keepdims=True))
        pc = jnp.exp(sc - m)
        pn = jnp.exp(sn - m)
        dens.append(jnp.sum(pc, axis=-1, keepdims=True) + jnp.sum(pn, axis=-1, keepdims=True))
        ms.append(m)
        pcs.append(pc.astype(BF16))
        pns.append(pn.astype(BF16))
    nums = (lax.dot_general(jnp.concatenate(pcs, axis=0), cv_ref[0].astype(BF16), nt, preferred_element_type=F32)
            + lax.dot_general(jnp.concatenate(pns, axis=0), vnb[...].astype(BF16), nt, preferred_element_type=F32))
    mx = functools.reduce(jnp.maximum, ms)
    den = 0.0
    num = 0.0
    for bi in range(nbr):
        wt = jnp.exp(ms[bi] - mx)
        den = den + dens[bi] * wt
        num = num + nums[bi * HQ:(bi + 1) * HQ, :] * wt
    out = _block_diag_rows_keep(num / den, w)
    o_ref[0] = sum(out[h * NEW:(h + 1) * NEW, :] for h in range(A_HEADS))

    lane = lax.broadcasted_iota(jnp.int32, (ROLL_ROWS, LANES), 1)
    tail = lane >= LANES - NEW
    for src, newb, dst in ((ck_ref, knb, nk_ref), (cv_ref, vnb, nv_ref)):
        for r0 in range(0, w, ROLL_ROWS):
            rows = slice(r0, r0 + ROLL_ROWS)
            rolled = pltpu.roll(src[0, rows, :], la - NEW, 1)
            dst[0, rows, 0:la - LANES] = rolled[:, 0:la - LANES]
            fresh = pltpu.roll(newb[rows, :], LANES - NEW, 1)
            dst[0, rows, la - LANES:la] = jnp.where(tail, fresh, rolled[:, la - LANES:la])


def _sample_at(q, kn_t, vn_t, ck_t, cv_t):
    s, w, la = ck_t.shape
    assert la >= max(wd for (wd, _) in A_PATTERNS) and la % LANES == 0
    seq = lambda i: (i, 0, 0)
    big = pl.BlockSpec((1, w, la), seq)
    new = pl.BlockSpec((1, w, NEW), seq)
    qspec = pl.BlockSpec((1, NEW, w), seq)
    nbr = len(A_PATTERNS)
    return pl.pallas_call(
        functools.partial(_sample_at_kernel, la=la, slopes=_alibi_slopes(A_HEADS)),
        grid=(s,),
        in_specs=[qspec, new, new, big, big],
        out_specs=[big, big, qspec],
        out_shape=[jax.ShapeDtypeStruct(ck_t.shape, F32), jax.ShapeDtypeStruct(cv_t.shape, F32),
                   jax.ShapeDtypeStruct(q.shape, F32)],
        scratch_shapes=[pltpu.VMEM((nbr, HQ, la), F32), pltpu.VMEM((nbr, HQ, LANES), F32),
                        pltpu.VMEM((w, LANES), F32), pltpu.VMEM((w, LANES), F32)],
        compiler_params=_cparams("arbitrary"),
        name="sample_attn_at",
    )(q, kn_t, vn_t, ck_t, cv_t)


def _sample_b_kernel(sink_ref, q_ref, kn_ref, vn_ref, ck_ref, cv_ref, nk_ref, nv_ref, o_ref,
                     bc_ref, bn_ref, sk_ref, knb, vnb, *, lb, nseq, slopes):
    @pl.when(pl.program_id(0) == 0)
    def _():
        r = lax.broadcasted_iota(jnp.int32, (HQ, lb), 0)
        c = lax.broadcasted_iota(jnp.int32, (HQ, lb), 1)
        dist = (r % NEW) + lb - c
        slope = jnp.zeros((HQ, lb), F32)
        sink = jnp.zeros((HQ, LANES), F32)
        rs = lax.broadcasted_iota(jnp.int32, (HQ, LANES), 0)
        for h in range(B_HEADS):
            slope = jnp.where(r // NEW == h, slopes[h], slope)
            sink = jnp.where(rs // NEW == h, sink_ref[h], sink)
        bc_ref[...] = jnp.where(dist <= BAND, -slope * dist.astype(F32), NEG)
        cn = lax.broadcasted_iota(jnp.int32, (HQ, LANES), 1)
        dn = (rs % NEW) - cn
        sl = jnp.zeros((HQ, LANES), F32)
        for h in range(B_HEADS):
            sl = jnp.where(rs // NEW == h, slopes[h], sl)
        bn_ref[...] = jnp.where((dn >= 0) & (cn < NEW), -sl * dn.astype(F32), NEG)
        sk_ref[...] = sink
        knb[...] = jnp.zeros_like(knb)
        vnb[...] = jnp.zeros_like(vnb)

    nt = (((1,), (1,)), ((), ()))
    rr = lax.broadcasted_iota(jnp.int32, (HQ, LANES), 0)
    cc = lax.broadcasted_iota(jnp.int32, (HQ, LANES), 1)
    own = ((rr // NEW) % B_KV_HEADS) == (cc // HEAD_DIM)
    sink = sk_ref[:, 0:1]
    tail = cc >= LANES - NEW
    for t in range(nseq):
        knb[:, 0:NEW] = kn_ref[t]
        vnb[:, 0:NEW] = vn_ref[t]
        for src, newb, dst in ((ck_ref, knb, nk_ref), (cv_ref, vnb, nv_ref)):
            for r0 in range(0, LANES, HQ):
                rows = slice(r0, r0 + HQ)
                dst[t, rows, :] = jnp.where(tail, pltpu.roll(newb[rows, :], LANES - NEW, 1),
                                            pltpu.roll(src[t, rows, :], lb - NEW, 1))
        q = q_ref[t]
        rows = []
        for g in range(B_GROUP):
            for kv in range(B_KV_HEADS):
                rows.append(q[:, g * LANES:(g + 1) * LANES])
        qbd = jnp.where(own, jnp.concatenate(rows, axis=0), 0.0).astype(BF16)
        sc = jnp.dot(qbd, ck_ref[t].astype(BF16), preferred_element_type=F32) * SCALE + bc_ref[...]
        sn = jnp.dot(qbd, knb[...].astype(BF16), preferred_element_type=F32) * SCALE + bn_ref[...]
        m = jnp.maximum(jnp.maximum(jnp.max(sc, axis=-1, keepdims=True), jnp.max(sn, axis=-1, keepdims=True)), sink)
        pc = jnp.exp(sc - m)
        pn = jnp.exp(sn - m)
        den = jnp.sum(pc, axis=-1, keepdims=True) + jnp.sum(pn, axis=-1, keepdims=True) + jnp.exp(sink - m)
        o = (lax.dot_general(pc.astype(BF16), cv_ref[t].astype(BF16), nt, preferred_element_type=F32)
             + lax.dot_general(pn.astype(BF16), vnb[...].astype(BF16), nt, preferred_element_type=F32)) / den
        o = jnp.where(own, o, 0.0)
        for g in range(B_GROUP):
            base = g * B_KV_HEADS * NEW
            o_ref[t, :, g * LANES:(g + 1) * LANES] = o[base:base + NEW, :] + o[base + NEW:base + 2 * NEW, :]


def _sample_b(sinks_perm, slopes_perm, q, kn, vn, ck, cv, nseq=8):
    s, kw, lb = ck.shape
    assert lb == BAND and kw == LANES and lb == LANES and s % nseq == 0
    seq = lambda i: (i, 0, 0)
    return pl.pallas_call(
        functools.partial(_sample_b_kernel, lb=lb, nseq=nseq, slopes=slopes_perm),
        grid=(s // nseq,),
        in_specs=[pl.BlockSpec(memory_space=pltpu.SMEM), pl.BlockSpec((nseq, NEW, A_WIDTH), seq),
                  pl.BlockSpec((nseq, kw, NEW), seq), pl.BlockSpec((nseq, kw, NEW), seq),
                  pl.BlockSpec((nseq, kw, lb), seq), pl.BlockSpec((nseq, kw, lb), seq)],
        out_specs=[pl.BlockSpec((nseq, kw, lb), seq), pl.BlockSpec((nseq, kw, lb), seq),
                   pl.BlockSpec((nseq, NEW, A_WIDTH), seq)],
        out_shape=[jax.ShapeDtypeStruct(ck.shape, F32), jax.ShapeDtypeStruct(cv.shape, F32),
                   jax.ShapeDtypeStruct(q.shape, F32)],
        scratch_shapes=[pltpu.VMEM((HQ, lb), F32), pltpu.VMEM((HQ, LANES), F32), pltpu.VMEM((HQ, LANES), F32),
                        pltpu.VMEM((LANES, kw), F32), pltpu.VMEM((LANES, kw), F32)],
        compiler_params=_cparams("arbitrary"),
        name="sample_attn_b",
    )(sinks_perm, q, kn, vn, ck, cv)


def _outproj_kernel(x_ref, oa_ref, ob_ref, wa_ref, wb_ref, ht_ref):
    oa = jnp.concatenate([oa_ref[s] for s in range(NSLAB)], axis=1).astype(BF16)
    ob = jnp.concatenate([ob_ref[s] for s in range(NSLAB)], axis=1).astype(BF16)
    h = (x_ref[...] + jnp.dot(oa, wa_ref[...], preferred_element_type=F32)
         + jnp.dot(ob, wb_ref[...], preferred_element_type=F32))
    ht_ref[...] = h.T


def _outproj(x2d, oa4, ob4, wa, wb, tm):
    n, d = x2d.shape
    w = A_WIDTH
    row = lambda i: (i, 0)
    const = lambda i: (0, 0)
    slab = pl.BlockSpec((NSLAB, tm, LANES), lambda i: (0, i, 0))
    return pl.pallas_call(
        _outproj_kernel,
        grid=(n // tm,),
        in_specs=[pl.BlockSpec((tm, d), row), slab, slab, pl.BlockSpec((w, d), const), pl.BlockSpec((w, d), const)],
        out_specs=pl.BlockSpec((d, tm), lambda i: (0, i)),
        out_shape=jax.ShapeDtypeStruct((d, n), F32),
        compiler_params=_cparams("parallel"),
        name="out_proj",
    )(x2d, oa4, ob4, wa, wb)


SUBLANES = 8


def _merge_exchange(n):
    pairs = []
    t = max(1, math.ceil(math.log2(n)))
    p = 1 << (t - 1)
    while p > 0:
        q, r, d = 1 << (t - 1), 0, p
        while d > 0:
            pairs.extend((i, i + d) for i in range(n - d) if (i & p) == r)
            d, q, r = q - p, q >> 1, p
        p >>= 1
    return pairs


def _vmax(a, b):
    if a is None:
        return b
    if b is None:
        return a
    return jnp.maximum(a, b)


def _vmin(a, b):
    if a is None or b is None:
        return None
    return jnp.minimum(a, b)


def _exchange(x, i, j):
    x[i], x[j] = _vmax(x[i], x[j]), _vmin(x[i], x[j])


def _top16(tiles):
    x = list(tiles) + [None] * (PEER_TOPK - len(tiles))
    for (i, j) in _merge_exchange(len(tiles)):
        _exchange(x, i, j)
    for shift in (4, 2, 1):
        y = [None if v is None else pltpu.roll(v, shift, 0) for v in x]
        x = [_vmax(x[k], y[PEER_TOPK - 1 - k]) for k in range(PEER_TOPK)]
        for d in (8, 4, 2, 1):
            for i in range(PEER_TOPK):
                if not i & d:
                    _exchange(x, i, i + d)
    return x


def _count_prefix(pred, vals):
    sel = jnp.where
    c16 = pred(vals[15])
    c8 = pred(vals[7])
    c4 = pred(sel(c8, vals[11], vals[3]))
    c2 = pred(sel(c8, sel(c4, vals[13], vals[9]), sel(c4, vals[5], vals[1])))
    c1 = pred(sel(c8, sel(c4, sel(c2, vals[14], vals[12]), sel(c2, vals[10], vals[8])),
                  sel(c4, sel(c2, vals[6], vals[4]), sel(c2, vals[2], vals[0]))))
    lo = sel(c8, 8.0, 0.0) + sel(c4, 4.0, 0.0) + sel(c2, 2.0, 0.0) + sel(c1, 1.0, 0.0)
    return sel(c16, float(PEER_TOPK), lo)


def _rows_sum(x):
    for shift in (4, 2, 1):
        x = x + pltpu.roll(x, shift, 0)
    return x


def _peer_select_kernel(ht_ref, g_ref, wq_ref, keys_ref,
                        hn_ref, r1_ref, w1_ref, cnt_ref, w0_ref, s_ref, *, nk, tn):
    h = ht_ref[...]
    ms = jnp.mean(h * h, axis=0, keepdims=True)
    hn = (h * lax.rsqrt(ms + NORM_EPS) * g_ref[...]).astype(BF16)
    hn_ref[...] = hn
    q = jnp.dot(wq_ref[...], hn, preferred_element_type=F32)
    half = q.shape[0] // (2 * PEER_HEADS)
    for k in range(2 * PEER_HEADS):
        qk = q[k * half:(k + 1) * half, :].astype(BF16)
        s_ref[k] = jnp.dot(keys_ref[k], qk, preferred_element_type=F32)
    nchunk = tn // LANES
    ntile = nk // SUBLANES
    sub = lax.broadcasted_iota(jnp.int32, (SUBLANES, LANES), 0)

    def pack(vals):
        out = vals[-1]
        for r in range(len(vals) - 2, -1, -1):
            out = jnp.where(sub == r, vals[r], out)
        return out

    def head_chunk(t, carry):
        hd = t // nchunk
        lanes = pl.ds(pl.multiple_of((t % nchunk) * LANES, LANES), LANES)
        rows = [slice(k * SUBLANES, (k + 1) * SUBLANES) for k in range(ntile)]
        s0 = [s_ref[2 * hd, r, lanes] for r in rows]
        s1 = [s_ref[2 * hd + 1, r, lanes] for r in rows]
        v0 = _top16(s0)
        v1 = _top16(s1)
        v1lo, v1hi, v0hi = pack(v1[0:8]), pack(v1[8:16]), pack(v0[8:16])
        cands = [v0[0] + v1lo, v0[0] + v1hi] + [v0[a] + v1lo for a in range(1, 8)] + [v0hi + v1[0]]
        best = _top16(cands)
        top, tau = best[0], best[PEER_TOPK - 1]
        z = _rows_sum(sum(jnp.where(c >= tau, jnp.exp(c - top), 0.0) for c in cands))
        inv_z = 1.0 / z
        for m in range(ntile // 2):
            cnt, rk1 = [], []
            for k in (2 * m, 2 * m + 1):
                cnt.append(_count_prefix(lambda t, k=k: s0[k] + t >= tau, v1))
                rk1.append(_count_prefix(lambda t, k=k: t > s1[k], v1))
            pair = slice(2 * m * SUBLANES, (2 * m + 2) * SUBLANES)
            both = lambda f: jnp.concatenate([f(2 * m), f(2 * m + 1)], axis=0)
            cnt_ref[hd, pair, lanes] = jnp.concatenate(cnt, axis=0)
            w0_ref[hd, pair, lanes] = both(lambda k: jnp.exp(s0[k] - v0[0]) * inv_z)
            r1_ref[hd, pair, lanes] = jnp.concatenate(rk1, axis=0).astype(BF16)
            w1_ref[hd, pair, lanes] = both(lambda k: jnp.exp(s1[k] - v1[0])).astype(BF16)
        return carry

    lax.fori_loop(0, PEER_HEADS * nchunk, head_chunk, 0)


def _peer_select(ht, g_col, wq_t, keys, tn):
    d, n = ht.shape
    assert tn % LANES == 0 and n % tn == 0
    nslab, nk, half = keys.shape
    tok = lambda i: (0, i)
    tok3 = lambda i: (0, 0, i)
    stat = lambda dt: jax.ShapeDtypeStruct((PEER_HEADS, nk, n), dt)
    return pl.pallas_call(
        functools.partial(_peer_select_kernel, nk=nk, tn=tn),
        grid=(n // tn,),
        in_specs=[pl.BlockSpec((d, tn), tok), pl.BlockSpec((d, 1), lambda i: (0, 0)),
                  pl.BlockSpec(wq_t.shape, lambda i: (0, 0)), pl.BlockSpec(keys.shape, lambda i: (0, 0, 0))],
        out_specs=[pl.BlockSpec((d, tn), tok)] + [pl.BlockSpec((PEER_HEADS, nk, tn), tok3)] * 4,
        out_shape=[jax.ShapeDtypeStruct((d, n), BF16), stat(BF16), stat(BF16), stat(F32), stat(F32)],
        scratch_shapes=[pltpu.VMEM((nslab, nk, tn), F32)],
        compiler_params=_cparams("parallel"),
        name="peer_select",
    )(ht, g_col, wq_t, keys)


def _gelu(x):
    return 0.5 * x * (1.0 + lax.erf(x * (2.0 ** -0.5)))


def _peer_dense_kernel(hn_ref, u_ref, vt_ref, r1_ref, w1_ref, cnt_ref, w0_ref, ht_ref, y_ref,
                       acc_ref, c0_ref, c1_ref, g_ref, *, nk, te, tn, lc, rb, ne, nsteps):
    s = pl.program_id(0)

    @pl.when(s == 0)
    def _():
        for r in (c0_ref, c1_ref, acc_ref):
            r[...] = jnp.zeros_like(r)

    e = jnp.minimum(s, nsteps - 1) % ne
    p1 = s - 1
    first = (jnp.maximum(p1, 0) % ne) == 0
    nslab = te // nk

    def stages(c_new, c_prev):
        ob = acc_ref.shape[0] // nslab
        for i2 in range(nslab // SLAB_GROUP):
            slabs = range(i2 * SLAB_GROUP, (i2 + 1) * SLAB_GROUP)
            orows = slice(i2 * SLAB_GROUP * ob, (i2 + 1) * SLAB_GROUP * ob)
            contrib = jnp.dot(vt_ref[orows, :], c_prev[...], preferred_element_type=F32)
            acc_ref[orows, :] = jnp.where(first, contrib, acc_ref[orows, :] + contrib)
            bits = lax.bitcast_convert_type(contrib[0:1, :], jnp.int32)
            zero = lax.shift_right_logical(lax.shift_right_logical(bits, 16), 16).astype(F32)
            cnt_rows = [[(cnt_ref[hd, pl.ds(e * nslab + ii, 1), :] + zero).astype(BF16)
                         for hd in range(PEER_HEADS)] for ii in slabs]
            w0_rows = [[w0_ref[hd, pl.ds(e * nslab + ii, 1), :].astype(BF16)
                        for hd in range(PEER_HEADS)] for ii in slabs]
            for c in range(tn // lc):
                ls = slice(c * lc, (c + 1) * lc)
                for jb in range(nk // rb):
                    js = slice(jb * rb, (jb + 1) * rb)
                    gs = [None] * SLAB_GROUP
                    for hd in range(PEER_HEADS):
                        r1 = r1_ref[hd, js, ls]
                        w1 = w1_ref[hd, js, ls]
                        for k in range(SLAB_GROUP):
                            term = jnp.where(r1 < cnt_rows[k][hd][:, ls], w1, jnp.zeros((), BF16)) * w0_rows[k][hd][:, ls]
                            gs[k] = term if gs[k] is None else gs[k] + term
                    for k, ii in enumerate(slabs):
                        g_ref[ii * nk + jb * rb:ii * nk + (jb + 1) * rb, ls] = gs[k]
        for ii in range(nslab // SLAB_GROUP):
            rows = slice(ii * SLAB_GROUP * nk, (ii + 1) * SLAB_GROUP * nk)
            a = jnp.dot(u_ref[rows, :], hn_ref[...], preferred_element_type=F32)
            c_new[rows, :] = g_ref[rows, :] * _gelu(a).astype(BF16)

    @pl.when(s % 2 == 0)
    def _():
        stages(c0_ref, c1_ref)

    @pl.when(s % 2 == 1)
    def _():
        stages(c1_ref, c0_ref)

    @pl.when((p1 >= 0) & (p1 % ne == ne - 1))
    def _():
        y_ref[...] = (ht_ref[...] + acc_ref[...]).T


def _peer_dense(hn, u, vt, r1, w1, cnt, w0, ht, *, tn, te, lc, rb):
    d, n = hn.shape
    ne = u.shape[0] // te
    nk = r1.shape[1]
    nsteps = (n // tn) * ne
    last = nsteps - 1
    pair = lambda s, lag: jnp.clip(s - lag, 0, last)
    stat = pl.BlockSpec((PEER_HEADS, nk, tn), lambda s: (0, 0, pair(s, 0) // ne), pipeline_mode=pl.Buffered(1))
    return pl.pallas_call(
        functools.partial(_peer_dense_kernel, nk=nk, te=te, tn=tn, lc=lc, rb=min(rb, nk), ne=ne, nsteps=nsteps),
        grid=(nsteps + 1,),
        in_specs=[pl.BlockSpec((d, tn), lambda s: (0, pair(s, 0) // ne)),
                  pl.BlockSpec((te, d), lambda s: (pair(s, 0) % ne, 0)),
                  pl.BlockSpec((d, te), lambda s: (0, pair(s, 1) % ne)),
                  stat, stat, stat, stat,
                  pl.BlockSpec((d, tn), lambda s: (0, pair(s, 1) // ne))],
        out_specs=pl.BlockSpec((tn, d), lambda s: (pair(s, 1) // ne, 0)),
        out_shape=jax.ShapeDtypeStruct((n, d), F32),
        scratch_shapes=[pltpu.VMEM((d, tn), F32), pltpu.VMEM((te, tn), BF16), pltpu.VMEM((te, tn), BF16),
                        pltpu.VMEM((te, tn), BF16)],
        compiler_params=_cparams("arbitrary"),
        name="peer_dense",
    )(hn, u, vt, r1, w1, cnt, w0, ht)


def _peer(ht, g_col, wq_t, keys, u, vt, *, tn_sel, tn, te, lc, rb):
    hn, r1, w1, cnt, w0 = _peer_select(ht, g_col, wq_t, keys, tn_sel)
    return _peer_dense(hn, u, vt, r1, w1, cnt, w0, ht, tn=tn, te=te, lc=lc, rb=rb)


TOKEN_TILE = 512
DENSE_TOKEN_TILE = 1024
EXPERT_TILE = 1024
GATE_LANES = 256
GATE_ROWS = 64
SLAB_GROUP = 4


def _peer_tiles(n_tokens, n_keys):
    tn = min(TOKEN_TILE, n_tokens)
    td = min(DENSE_TOKEN_TILE, n_tokens)
    return dict(tn_sel=tn, tn=td, te=min(EXPERT_TILE, n_keys * n_keys), lc=min(GATE_LANES, td), rb=GATE_ROWS)


def kernel(x_prompt, x_sample, cache_a_k, cache_a_v, cache_b_k, cache_b_v, norm_attn, w_in, g_qa, g_ka, g_qb, g_kb, sinks, w_o, norm_ffn, peer_wq, peer_keys, peer_u, peer_v):
    b, l, d = x_prompt.shape
    s, ns, _ = x_sample.shape
    assert ns == NEW and w_in.shape[0] == 1
    la, lb = cache_a_k.shape[2], cache_b_k.shape[2]
    w = A_WIDTH
    perm = _qb_perm()

    wl = w_in[0]
    w_all = jnp.concatenate([wl[:, :3 * w], wl[:, 3 * w:4 * w][:, perm], wl[:, 4 * w:]], axis=1).astype(BF16)
    seg = (jnp.arange(w)[:, None] // HEAD_DIM == jnp.arange(w)[None, :] // HEAD_DIM).astype(BF16)
    t8 = lambda g: jnp.tile(g, A_HEADS)[None, :]
    gains = (t8(g_qa[0]), t8(g_ka[0]), t8(g_qb[0]), jnp.tile(g_kb[0], B_KV_HEADS)[None, :])
    sb = _alibi_slopes(B_HEADS)
    slopes_perm = [sb[(h % 2) * B_GROUP + h // 2] for h in range(B_HEADS)]
    sinks_perm = jnp.stack([sinks[0, (h % 2) * B_GROUP + h // 2] for h in range(B_HEADS)])
    wo_a = w_o[0, :w, :].astype(BF16)
    wo_b = w_o[0, w:, :][perm, :].astype(BF16)
    g_col = norm_ffn[0][:, None]
    wq_t = peer_wq[0].T.astype(BF16)
    nk = peer_keys.shape[3]
    keys = peer_keys[0].reshape(2 * PEER_HEADS, nk, peer_keys.shape[4]).astype(BF16)
    u = peer_u[0].astype(BF16)
    vt = peer_v[0].T.astype(BF16)

    xp = x_prompt.reshape(b * l, d)
    tm_p = min(TOKEN_TILE, l)
    qa4, ka4, va4, qb4, kat, vat, kb, vb, kbt, vbt = _project(xp, norm_attn, w_all, seg, *gains, tm_p, b)
    seq4 = lambda t: t.reshape(t.shape[0], b, l, LANES)
    oa4 = _mixer(seq4(qa4), seq4(ka4), seq4(va4), dils=[dl for (_, dl) in A_PATTERNS],
                 slopes=_alibi_slopes(A_HEADS))
    ob4 = _mixer(seq4(qb4), kb.reshape(1, b, l, LANES), vb.reshape(1, b, l, LANES), dils=[1],
                 slopes=slopes_perm, sinks=sinks_perm)
    ht_p = _outproj(xp, oa4.reshape(NSLAB, b * l, LANES), ob4.reshape(NSLAB, b * l, LANES), wo_a, wo_b, tm_p)
    y_p = _peer(ht_p, g_col, wq_t, keys, u, vt, **_peer_tiles(b * l, nk))
    na = min(la, l)
    nb = min(lb, l)
    def tail(t, heads, keep):
        t = t[:, :, l - keep:].reshape(b, heads, HEAD_DIM, keep)
        return jnp.transpose(t, (0, 3, 1, 2))[None]
    pak, pav = tail(kat, A_HEADS, na), tail(vat, A_HEADS, na)
    pbk, pbv = tail(kbt, B_KV_HEADS, nb), tail(vbt, B_KV_HEADS, nb)

    xs = x_sample.reshape(s * ns, d)
    tm_s = min(TOKEN_TILE, s * ns)
    qa4, _, _, qb4, kat, vat, _, _, kbt, vbt = _project(xs, norm_attn, w_all, seg, *gains, tm_s, 1)
    wide = lambda t4: jnp.transpose(t4, (1, 0, 2)).reshape(s, ns, w)
    slabs = lambda t: jnp.transpose(t.reshape(s * ns, NSLAB, LANES), (1, 0, 2))
    to_t = lambda c: jnp.transpose(c[0], (0, 2, 3, 1)).reshape(s, w, la)
    from_t = lambda t: jnp.transpose(t.reshape(s, A_HEADS, HEAD_DIM, la), (0, 3, 1, 2))[None]
    new_t = lambda t: jnp.transpose(t.reshape(t.shape[1], s, ns), (1, 0, 2))
    sak_t, sav_t, oa = _sample_at(wide(qa4), new_t(kat), new_t(vat), to_t(cache_a_k), to_t(cache_a_v))
    sak, sav = from_t(sak_t), from_t(sav_t)
    to_tb = lambda c: jnp.transpose(c[0], (0, 2, 3, 1)).reshape(s, LANES, lb)
    from_tb = lambda t: jnp.transpose(t.reshape(s, B_KV_HEADS, HEAD_DIM, lb), (0, 3, 1, 2))[None]
    sbk_t, sbv_t, ob = _sample_b(sinks_perm, slopes_perm, wide(qb4), new_t(kbt), new_t(vbt),
                                 to_tb(cache_b_k), to_tb(cache_b_v))
    ht_s = _outproj(xs, slabs(oa), slabs(ob), wo_a, wo_b, tm_s)
    y_s = _peer(ht_s, g_col, wq_t, keys, u, vt, **_peer_tiles(s * ns, nk))

    return (y_p.reshape(b, l, d), y_s.reshape(s, ns, d), pak, pav, pbk, pbv,
            sak, sav, from_tb(sbk_t), from_tb(sbv_t))
```
